```python
import jax
import jax.numpy as jnp
from jax import lax
import numpy as np

D_MODEL = 1024
BATCH = 2
SEQ = 8192
DEPTH = 2

MIX_HALF = D_MODEL // 2
GMLP_GROUPS = 4
GMLP_DIM = MIX_HALF // GMLP_GROUPS
GMLP_CHUNK = 128
HGRN_HEADS = 4
HGRN_DK = 128
HGRN_DV = MIX_HALF // HGRN_HEADS
HGRN_CHUNK = 32
A_WIDTH = GMLP_GROUPS * GMLP_DIM
B_KWIDTH = HGRN_HEADS * HGRN_DK
B_VWIDTH = HGRN_HEADS * HGRN_DV
MIX_SPLITS = (A_WIDTH, A_WIDTH, B_KWIDTH, B_KWIDTH, B_VWIDTH, B_VWIDTH)
MIX_IN = 2 * A_WIDTH + 2 * B_KWIDTH + 2 * B_VWIDTH
MIX_OUT = A_WIDTH + B_VWIDTH
RWKV_HEAD = 64
RWKV_HEADS = D_MODEL // RWKV_HEAD
RWKV_DECAY_LORA = 64
RWKV_AAA_LORA = 64
RWKV_GATE_LORA = 128
D_FF = ((8 * D_MODEL // 3 + 127) // 128) * 128
N_EXPERTS = 8
TOP_K = 2
D_FF_EXPERT = D_FF // 2
N_EVEN = (DEPTH + 1) // 2
N_ODD = DEPTH // 2
RMS_EPS = 1e-6
LN_EPS = 1e-5
GN_EPS = 64e-5
F32 = jnp.float32

kernel_name = 'hybrid_gmlp_hgrn2_rwkv7_moe'


def rms_norm(x, g):
    xf = x.astype(F32)
    y = xf * lax.rsqrt(jnp.mean(xf * xf, axis=-1, keepdims=True) + RMS_EPS)
    return (y * g.astype(F32)).astype(x.dtype)


def swiglu(h, w_gate, w_up, w_down):
    return (jax.nn.silu(h @ w_gate) * (h @ w_up)) @ w_down


def gmlp_spatial_gate(u, v, norm_g, w_s, b_s):
    bsz, t, g, c = v.shape
    n = t // GMLP_CHUNK
    vf = v.astype(F32)
    mu = jnp.mean(vf, axis=-1, keepdims=True)
    var = jnp.mean(jnp.square(vf - mu), axis=-1, keepdims=True)
    vn = ((vf - mu) * lax.rsqrt(var + LN_EPS) * norm_g.astype(F32)).reshape(bsz, n, GMLP_CHUNK, g, c)
    causal = jnp.tril(jnp.ones((GMLP_CHUNK, GMLP_CHUNK), dtype=bool))
    w = jnp.where(causal[None], w_s.astype(F32), 0.0)
    mixed = jnp.einsum('gts,bnsgc->bntgc', w, vn) + b_s.astype(F32).T[None, None, :, :, None]
    return u * mixed.reshape(bsz, t, g, c).astype(u.dtype)


def hgrn2_chunkwise(q, k, v, log_f):
    bsz, t, h, dk = q.shape
    dv = v.shape[-1]
    c = HGRN_CHUNK
    n = t // c
    q, k, log_f = (a.reshape(bsz, n, c, h, dk) for a in (q, k, log_f))
    v = v.reshape(bsz, n, c, h, dv)
    b = jnp.cumsum(log_f, axis=2)
    b_end = b[:, :, -1:]
    q_dec = q * jnp.exp(b)
    scores = jnp.einsum('bnthk,bnshk->bnhts', q_dec, k * jnp.exp(-b))
    causal = jnp.tril(jnp.ones((c, c), dtype=bool))
    scores = jnp.where(causal, scores, 0.0)
    o_intra = jnp.einsum('bnhts,bnshv->bnthv', scores, v)
    inc = jnp.einsum('bnshk,bnshv->nbhkv', k * jnp.exp(b_end - b), v)
    chunk_decay = jnp.exp(b_end[:, :, 0]).transpose(1, 0, 2, 3)

    def step(s, inp):
        d, ds = inp
        return d[..., None] * s + ds, s

    s0 = jnp.zeros((bsz, h, dk, dv), F32)
    _, s_start = lax.scan(step, s0, (chunk_decay, inc))
    o_inter = jnp.einsum('bnthk,nbhkv->bnthv', q_dec, s_start)
    return (o_intra + o_inter).reshape(bsz, t, h, dv)


def even_mixer(h, w_in, w_out, gmlp_norm_g, gmlp_w_s, gmlp_b_s, lb, onorm_g):
    bsz, t, _ = h.shape
    z = h @ w_in
    cuts = np.cumsum(MIX_SPLITS)[:-1].tolist()
    zu, zv, zq, zf, zi, zg = jnp.split(z, cuts, axis=-1)
    u = jax.nn.gelu(zu).reshape(bsz, t, GMLP_GROUPS, GMLP_DIM)
    v = jax.nn.gelu(zv).reshape(bsz, t, GMLP_GROUPS, GMLP_DIM)
    y_a = gmlp_spatial_gate(u, v, gmlp_norm_g, gmlp_w_s, gmlp_b_s).reshape(bsz, t, A_WIDTH)
    f = lb + (1.0 - lb) * jax.nn.sigmoid(zf.astype(F32))

    def heads(a, d):
        return a.reshape(bsz, t, HGRN_HEADS, d)

    o = hgrn2_chunkwise(heads(jax.nn.silu(zq.astype(F32)), HGRN_DK), heads(1.0 - f, HGRN_DK),
                        heads(zi.astype(F32), HGRN_DV), heads(jnp.log(f), HGRN_DK))
    o = o * lax.rsqrt(jnp.mean(o * o, axis=-1, keepdims=True) + RMS_EPS) * onorm_g.astype(F32)
    o = o * jax.nn.silu(heads(zg.astype(F32), HGRN_DV))
    y = jnp.concatenate([y_a, o.reshape(bsz, t, B_VWIDTH).astype(h.dtype)], axis=-1)
    return y @ w_out


def rwkv7_time_mix(h, mix, w_r, w_k, w_v, w_o, w0, w1, w2, a0, a1, a2, g1, g2,
                   k_k, k_a, r_k, ln_g, ln_b):
    bsz, t, d = h.shape
    nh, hs = RWKV_HEADS, RWKV_HEAD
    dx = jnp.pad(h[:, :-1], ((0, 0), (1, 0), (0, 0))) - h
    xr, xw, xk, xv, xa, xg = (h + dx * mix[i] for i in range(6))
    r = xr @ w_r
    k = xk @ w_k
    v = xv @ w_v
    w_log = -jax.nn.softplus(-(w0 + jnp.tanh(xw @ w1) @ w2)) - 0.5
    a = jax.nn.sigmoid(a0 + (xa @ a1) @ a2)
    g = jax.nn.sigmoid(xg @ g1) @ g2

    def heads(z):
        return z.astype(F32).reshape(bsz, t, nh, hs)

    kk = heads(k * k_k)
    kk = kk / jnp.maximum(jnp.sqrt(jnp.sum(kk * kk, axis=-1, keepdims=True)), 1e-12)
    k = heads(k * (1.0 + (a - 1.0) * k_a))
    r, v, a = heads(r), heads(v), heads(a)
    decay = jnp.exp(-jnp.exp(heads(w_log)))

    def tm(z):
        return jnp.swapaxes(z, 0, 1)

    def step(s, inp):
        r_t, w_t, k_t, v_t, kk_t, b_t = inp
        s = (s * w_t[:, :, None, :]
             - jnp.einsum('bhvk,bhk->bhv', s, kk_t)[..., None] * b_t[:, :, None, :]
             + v_t[..., None] * k_t[:, :, None, :])
        return s, jnp.einsum('bhvk,bhk->bhv', s, r_t)

    s0 = jnp.zeros((bsz, nh, hs, hs), F32)
    _, y = lax.scan(step, s0, (tm(r), tm(decay), tm(k), tm(v), tm(kk), tm(kk * a)))
    y = tm(y)
    mu = jnp.mean(y, axis=-1, keepdims=True)
    var = jnp.mean(jnp.square(y - mu), axis=-1, keepdims=True)
    y = (y - mu) * lax.rsqrt(var + GN_EPS) * ln_g.astype(F32).reshape(nh, hs) + ln_b.astype(F32).reshape(nh, hs)
    y = y + jnp.sum(r * k * r_k.astype(F32), axis=-1, keepdims=True) * v
    return (y.reshape(bsz, t, d).astype(h.dtype) * g) @ w_o


def moe_swiglu(h, router, router_b, w_gate, w_up, w_down):
    logits = (h @ router).astype(F32) + router_b.astype(F32)
    top_logit, top_idx = lax.top_k(logits, TOP_K)
    top_w = jax.nn.softmax(top_logit, axis=-1)
    gates = jnp.sum(jax.nn.one_hot(top_idx, N_EXPERTS, dtype=F32) * top_w[..., None], axis=-2)
    y = jnp.zeros_like(h)
    for e in range(N_EXPERTS):
        y = y + gates[..., e:e + 1].astype(h.dtype) * swiglu(h, w_gate[e], w_up[e], w_down[e])
    return y


def _normal(key, shape, scale):
    return scale * jax.random.normal(key, shape, F32)


def setup_inputs(seed: int = 0) -> dict:
    key = jax.random.key(seed)
    ks = jax.random.split(key, 40)
    D = D_MODEL
    inp = {}
    inp['x'] = jax.random.normal(ks[0], (BATCH, SEQ, D), F32)
    inp['norm_mix_g'] = 1.0 + _normal(ks[1], (DEPTH, D), 0.01)
    inp['norm_ffn_g'] = 1.0 + _normal(ks[2], (DEPTH, D), 0.01)
    inp['norm_out_g'] = 1.0 + _normal(ks[3], (D,), 0.01)
    inp['mix_w_in'] = _normal(ks[4], (N_EVEN, D, MIX_IN), D ** -0.5)
    inp['mix_w_out'] = _normal(ks[5], (N_EVEN, MIX_OUT, D), MIX_OUT ** -0.5)
    inp['gmlp_norm_g'] = 1.0 + _normal(ks[6], (N_EVEN, GMLP_GROUPS, GMLP_DIM), 0.01)
    inp['gmlp_w_s'] = _normal(ks[7], (N_EVEN, GMLP_GROUPS, GMLP_CHUNK, GMLP_CHUNK), GMLP_CHUNK ** -0.5)
    inp['gmlp_b_s'] = 1.0 + _normal(ks[8], (N_EVEN, GMLP_GROUPS, GMLP_CHUNK), 0.01)
    inp['hgrn_lb_logits'] = _normal(ks[9], (N_EVEN + 1, B_KWIDTH), 0.5)
    inp['hgrn_onorm_g'] = 1.0 + _normal(ks[10], (N_EVEN, HGRN_DV), 0.01)
    inp['ffn_w_gate'] = _normal(ks[11], (N_EVEN, D, D_FF), D ** -0.5)
    inp['ffn_w_up'] = _normal(ks[12], (N_EVEN, D, D_FF), D ** -0.5)
    inp['ffn_w_down'] = _normal(ks[13], (N_EVEN, D_FF, D), D_FF ** -0.5)
    inp['rwkv_mix'] = jax.random.uniform(ks[14], (N_ODD, 6, D), F32)
    inp['rwkv_w_r'] = _normal(ks[15], (N_ODD, D, D), D ** -0.5)
    inp['rwkv_w_k'] = _normal(ks[16], (N_ODD, D, D), D ** -0.5)
    inp['rwkv_w_v'] = _normal(ks[17], (N_ODD, D, D), D ** -0.5)
    inp['rwkv_w_o'] = _normal(ks[18], (N_ODD, D, D), D ** -0.5)
    inp['rwkv_w0'] = -1.0 + _normal(ks[19], (N_ODD, D), 0.5)
    inp['rwkv_w1'] = _normal(ks[20], (N_ODD, D, RWKV_DECAY_LORA), D ** -0.5)
    inp['rwkv_w2'] = _normal(ks[21], (N_ODD, RWKV_DECAY_LORA, D), 0.1 * RWKV_DECAY_LORA ** -0.5)
    inp['rwkv_a0'] = _normal(ks[22], (N_ODD, D), 0.1)
    inp['rwkv_a1'] = _normal(ks[23], (N_ODD, D, RWKV_AAA_LORA), D ** -0.5)
    inp['rwkv_a2'] = _normal(ks[24], (N_ODD, RWKV_AAA_LORA, D), 0.1 * RWKV_AAA_LORA ** -0.5)
    inp['rwkv_g1'] = _normal(ks[25], (N_ODD, D, RWKV_GATE_LORA), D ** -0.5)
    inp['rwkv_g2'] = _normal(ks[26], (N_ODD, RWKV_GATE_LORA, D), RWKV_GATE_LORA ** -0.5)
    inp['rwkv_k_k'] = 0.85 + _normal(ks[27], (N_ODD, D), 0.05)
    inp['rwkv_k_a'] = 1.0 + _normal(ks[28], (N_ODD, D), 0.05)
    inp['rwkv_r_k'] = _normal(ks[29], (N_ODD, RWKV_HEADS, RWKV_HEAD), 0.1)
    inp['rwkv_ln_g'] = 1.0 + _normal(ks[30], (N_ODD, D), 0.01)
    inp['rwkv_ln_b'] = _normal(ks[31], (N_ODD, D), 0.01)
    inp['moe_router'] = _normal(ks[32], (N_ODD, D, N_EXPERTS), D ** -0.5)
    inp['moe_router_b'] = _normal(ks[33], (N_ODD, N_EXPERTS), 0.01)
    inp['moe_w_gate'] = _normal(ks[34], (N_ODD, N_EXPERTS, D, D_FF_EXPERT), D ** -0.5)
    inp['moe_w_up'] = _normal(ks[35], (N_ODD, N_EXPERTS, D, D_FF_EXPERT), D ** -0.5)
    inp['moe_w_down'] = _normal(ks[36], (N_ODD, N_EXPERTS, D_FF_EXPERT, D), D_FF_EXPERT ** -0.5)
    return inp


def reference(x, norm_mix_g, norm_ffn_g, norm_out_g,
              mix_w_in, mix_w_out, gmlp_norm_g, gmlp_w_s, gmlp_b_s, hgrn_lb_logits, hgrn_onorm_g,
              ffn_w_gate, ffn_w_up, ffn_w_down,
              rwkv_mix, rwkv_w_r, rwkv_w_k, rwkv_w_v, rwkv_w_o, rwkv_w0, rwkv_w1, rwkv_w2,
              rwkv_a0, rwkv_a1, rwkv_a2, rwkv_g1, rwkv_g2, rwkv_k_k, rwkv_k_a, rwkv_r_k,
              rwkv_ln_g, rwkv_ln_b,
              moe_router, moe_router_b, moe_w_gate, moe_w_up, moe_w_down):
    lower_bounds = jnp.cumsum(jax.nn.softmax(hgrn_lb_logits.astype(F32), axis=0), axis=0)
    h = x
    for layer in range(DEPTH):
        j = layer // 2
        if layer % 2 == 0:
            h = h + even_mixer(rms_norm(h, norm_mix_g[layer]), mix_w_in[j], mix_w_out[j],
                               gmlp_norm_g[j], gmlp_w_s[j], gmlp_b_s[j], lower_bounds[j], hgrn_onorm_g[j])
            h = h + swiglu(rms_norm(h, norm_ffn_g[layer]), ffn_w_gate[j], ffn_w_up[j], ffn_w_down[j])
        else:
            h = h + rwkv7_time_mix(rms_norm(h, norm_mix_g[layer]), rwkv_mix[j], rwkv_w_r[j], rwkv_w_k[j],
                                   rwkv_w_v[j], rwkv_w_o[j], rwkv_w0[j], rwkv_w1[j], rwkv_w2[j],
                                   rwkv_a0[j], rwkv_a1[j], rwkv_a2[j], rwkv_g1[j], rwkv_g2[j],
                                   rwkv_k_k[j], rwkv_k_a[j], rwkv_r_k[j], rwkv_ln_g[j], rwkv_ln_b[j])
            h = h + moe_swiglu(rms_norm(h, norm_ffn_g[layer]), moe_router[j], moe_router_b[j],
                               moe_w_gate[j], moe_w_up[j], moe_w_down[j])
    return rms_norm(h, norm_out_g)
```

```python
import functools

import jax
import jax.numpy as jnp
from jax import lax
from jax.experimental import pallas as pl
from jax.experimental.pallas import tpu as pltpu

F32 = jnp.float32
BF16 = jnp.bfloat16

D_MODEL = 1024
GMLP_GROUPS = 4
GMLP_DIM = 128
GMLP_CHUNK = 128
HGRN_HEADS = 4
HGRN_DK = 128
HGRN_CHUNK = 32
MIX_HALF = 512
MIX_IN = 3072
RWKV_HEAD = 64
N_EXPERTS = 8
D_FF_EXPERT = 1408
RMS_EPS = 1e-6
LN_EPS = 1e-5
GN_EPS = 64e-5

LANES = 128
VMEM_LIMIT_BYTES = 56 * 1024 * 1024

MIX_TM = 256
MLP_TM = 512
PROJ_TM = 256
REC_TB = 256
REC_C = 64
OUT_TM = 512


def _cparams(sem):
    return pltpu.CompilerParams(dimension_semantics=sem, vmem_limit_bytes=VMEM_LIMIT_BYTES)


def _dot(a, b):
    return jnp.dot(a.astype(BF16), b.astype(BF16), preferred_element_type=F32)


def _dot_nt(a, b):
    return lax.dot_general(a.astype(BF16), b.astype(BF16), (((1,), (1,)), ((), ())),
                           preferred_element_type=F32)


def _dot_tn(a, b):
    return lax.dot_general(a.astype(BF16), b.astype(BF16), (((0,), (0,)), ((), ())),
                           preferred_element_type=F32)


def _split(a):
    hi = a.astype(BF16)
    lo = (a - hi.astype(F32)).astype(BF16)
    return hi, lo


def _dot_exact_lhs(m_bf16, a):
    hi, lo = _split(a)
    return (jnp.dot(m_bf16, hi, preferred_element_type=F32)
            + jnp.dot(m_bf16, lo, preferred_element_type=F32))


def _dot_exact_rhs(a, m_bf16):
    hi, lo = _split(a)
    return (jnp.dot(hi, m_bf16, preferred_element_type=F32)
            + jnp.dot(lo, m_bf16, preferred_element_type=F32))


def _dot3(a, b):
    a_hi, a_lo = _split(a)
    b_hi, b_lo = _split(b)
    return (jnp.dot(a_hi, b_hi, preferred_element_type=F32)
            + jnp.dot(a_hi, b_lo, preferred_element_type=F32)
            + jnp.dot(a_lo, b_hi, preferred_element_type=F32))


def _rms(x, g):
    return x * lax.rsqrt(jnp.mean(x * x, axis=-1, keepdims=True) + RMS_EPS) * g


def _sigmoid(x):
    return jax.nn.sigmoid(x)


def _silu(x):
    return x * jax.nn.sigmoid(x)


def _iota2(shape):
    return (lax.broadcasted_iota(jnp.int32, shape, 0), lax.broadcasted_iota(jnp.int32, shape, 1))


def _same_block(a, b, size):
    shift = size.bit_length() - 1
    assert 1 << shift == size
    return (a >> shift) == (b >> shift)


def _mixer0_kernel(x_ref, g_ref, win_ref, wout_ref, gng_ref, ws_ref, bias_ref, lb_ref, og_ref,
                   o_ref, st_ref):
    tm = MIX_TM

    @pl.when(pl.program_id(1) == 0)
    def _():
        st_ref[...] = jnp.zeros_like(st_ref)

    x = x_ref[0]
    z = _dot(_rms(x, g_ref[...]), win_ref[...])

    u = jax.nn.gelu(z[:, 0:MIX_HALF])
    v = jax.nn.gelu(z[:, MIX_HALF:2 * MIX_HALF])
    row, col = _iota2((GMLP_CHUNK, GMLP_CHUNK))
    tril = col <= row
    mixed_groups = []
    for g in range(GMLP_GROUPS):
        gs = slice(g * GMLP_DIM, (g + 1) * GMLP_DIM)
        vg = v[:, gs]
        mu = jnp.mean(vg, axis=-1, keepdims=True)
        d = vg - mu
        var = jnp.mean(d * d, axis=-1, keepdims=True)
        vn = (d * lax.rsqrt(var + LN_EPS) * gng_ref[:, gs]).astype(BF16)
        wg = jnp.where(tril, ws_ref[g], 0.0).astype(BF16)
        parts = [jnp.dot(wg, vn[c * GMLP_CHUNK:(c + 1) * GMLP_CHUNK], preferred_element_type=F32)
                 for c in range(tm // GMLP_CHUNK)]
        mixed_groups.append(jnp.concatenate(parts, axis=0))
    bias = jnp.concatenate([bias_ref[...]] * (tm // GMLP_CHUNK), axis=0)
    y_a = u * (jnp.concatenate(mixed_groups, axis=1) + bias)

    o0 = 2 * MIX_HALF
    zq = z[:, o0:o0 + MIX_HALF]
    zf = z[:, o0 + MIX_HALF:o0 + 2 * MIX_HALF]
    zi = z[:, o0 + 2 * MIX_HALF:o0 + 3 * MIX_HALF]
    zg = z[:, o0 + 3 * MIX_HALF:o0 + 4 * MIX_HALF]
    lb = lb_ref[...]
    q = _silu(zq)
    f = lb + (1.0 - lb) * _sigmoid(zf)
    k = 1.0 - f
    lf = jnp.log(f)
    rr, cc = _iota2((tm, tm))
    same = _same_block(rr, cc, HGRN_CHUNK)
    cum_m = jnp.where(same & (cc <= rr), 1.0, 0.0).astype(BF16)
    end_m = jnp.where(same, 1.0, 0.0).astype(BF16)
    lf_hi, lf_lo = _split(lf)
    b = (jnp.dot(cum_m, lf_hi, preferred_element_type=F32)
         + jnp.dot(cum_m, lf_lo, preferred_element_type=F32))
    b_end = (jnp.dot(end_m, lf_hi, preferred_element_type=F32)
             + jnp.dot(end_m, lf_lo, preferred_element_type=F32))
    q_dec = (q * jnp.exp(b)).astype(BF16)
    k_inv = (k * jnp.exp(-b)).astype(BF16)
    k_end = (k * jnp.exp(b_end - b)).astype(BF16)
    dec = jnp.exp(b_end)
    vb = zi.astype(BF16)

    r128, c128 = _iota2((128, 128))
    intra_mask = _same_block(r128, c128, HGRN_CHUNK) & (c128 <= r128)
    og = og_ref[...]
    o_heads = []
    for h in range(HGRN_HEADS):
        hs = slice(h * HGRN_DK, (h + 1) * HGRN_DK)
        intra = []
        for rb in range(tm // 128):
            rs = slice(rb * 128, (rb + 1) * 128)
            sc = _dot_nt(q_dec[rs, hs], k_inv[rs, hs])
            sc = jnp.where(intra_mask, sc, 0.0)
            intra.append(_dot(sc, vb[rs, hs]))
        o_intra = jnp.concatenate(intra, axis=0)
        st = st_ref[h]
        inter = []
        for c in range(tm // HGRN_CHUNK):
            cs = slice(c * HGRN_CHUNK, (c + 1) * HGRN_CHUNK)
            inter.append(_dot_nt(q_dec[cs, hs], st))
            inc_t = _dot_tn(vb[cs, hs], k_end[cs, hs])
            st = st * dec[c * HGRN_CHUNK:c * HGRN_CHUNK + 1, hs] + inc_t
        st_ref[h] = st
        o_h = o_intra + jnp.concatenate(inter, axis=0)
        o_h = o_h * lax.rsqrt(jnp.mean(o_h * o_h, axis=-1, keepdims=True) + RMS_EPS) * og
        o_heads.append(o_h * _silu(zg[:, hs]))

    y = jnp.concatenate([y_a] + o_heads, axis=1)
    o_ref[0] = x + _dot(y, wout_ref[...])


def _mixer0(x, g, w_in, w_out, gn_g, w_s, bias_b, lb, og):
    bsz, t, d = x.shape
    grid = (bsz, t // MIX_TM)
    const = lambda shape: pl.BlockSpec(shape, lambda b, i: (0,) * len(shape))
    return pl.pallas_call(
        _mixer0_kernel,
        grid=grid,
        in_specs=[
            pl.BlockSpec((1, MIX_TM, d), lambda b, i: (b, i, 0)),
            const((1, d)),
            const((d, MIX_IN)),
            const((2 * MIX_HALF, d)),
            const((1, MIX_HALF)),
            const((GMLP_GROUPS, GMLP_CHUNK, GMLP_CHUNK)),
            const((GMLP_CHUNK, MIX_HALF)),
            const((1, MIX_HALF)),
            const((1, HGRN_DK)),
        ],
        out_specs=pl.BlockSpec((1, MIX_TM, d), lambda b, i: (b, i, 0)),
        out_shape=jax.ShapeDtypeStruct((bsz, t, d), F32),
        scratch_shapes=[pltpu.VMEM((HGRN_HEADS, HGRN_DK, HGRN_DK), F32)],
        compiler_params=_cparams(("arbitrary", "arbitrary")),
        name="mixer0",
    )(x, g, w_in, w_out, gn_g, w_s, bias_b, lb, og)


def _mlp_kernel(*refs, moe, final_norm, n_j):
    refs = list(refs)
    x_ref, g_ref = refs[0], refs[1]
    pos = 2
    if moe:
        router_ref, rb_ref = refs[pos], refs[pos + 1]
        pos += 2
    wg_ref, wu_ref, wd_ref = refs[pos:pos + 3]
    pos += 3
    if final_norm:
        gout_ref = refs[pos]
        pos += 1
    o_ref = refs[pos]
    hn_ref, acc_ref = refs[pos + 1], refs[pos + 2]
    if moe:
        gates_ref = refs[pos + 3]
    j = pl.program_id(1)

    @pl.when(j == 0)
    def _():
        hn = _rms(x_ref[...], g_ref[...])
        hn_ref[...] = hn.astype(BF16)
        acc_ref[...] = jnp.zeros_like(acc_ref)
        if moe:
            logits = _dot3(hn, router_ref[...]) + rb_ref[...]
            lane = lax.broadcasted_iota(jnp.int32, logits.shape, 1)
            m1 = jnp.max(logits, axis=-1, keepdims=True)
            i1 = jnp.min(jnp.where(logits == m1, lane, LANES), axis=-1, keepdims=True)
            rest = jnp.where(lane == i1, -jnp.inf, logits)
            m2 = jnp.max(rest, axis=-1, keepdims=True)
            i2 = jnp.min(jnp.where(rest == m2, lane, LANES), axis=-1, keepdims=True)
            e2 = jnp.exp(m2 - m1)
            den = 1.0 + e2
            gates_ref[...] = (jnp.where(lane == i1, 1.0 / den, 0.0)
                              + jnp.where(lane == i2, e2 / den, 0.0))

    hn = hn_ref[...]
    a = jnp.dot(hn, wg_ref[0], preferred_element_type=F32)
    b = jnp.dot(hn, wu_ref[0], preferred_element_type=F32)
    y = _dot(_silu(a) * b, wd_ref[0])
    if moe:
        gates = gates_ref[...]
        lane = lax.broadcasted_iota(jnp.int32, gates.shape, 1)
        y = y * jnp.sum(jnp.where(lane == j, gates, 0.0), axis=-1, keepdims=True)
    acc_ref[...] += y

    @pl.when(j == n_j - 1)
    def _():
        out = x_ref[...] + acc_ref[...]
        if final_norm:
            out = _rms(out, gout_ref[...])
        o_ref[...] = out


def _gated_mlp(x, g, wg, wu, wd, router=None, router_b=None, g_out=None):
    n, d = x.shape
    n_j, _, dff = wg.shape
    moe = router is not None
    final_norm = g_out is not None
    tm = MLP_TM
    vec = pl.BlockSpec((1, d), lambda i, j: (0, 0))
    in_specs = [pl.BlockSpec((tm, d), lambda i, j: (i, 0)), vec]
    args = [x, g]
    if moe:
        in_specs += [pl.BlockSpec((d, LANES), lambda i, j: (0, 0)),
                     pl.BlockSpec((1, LANES), lambda i, j: (0, 0))]
        args += [router, router_b]
    in_specs += [pl.BlockSpec((1, d, dff), lambda i, j: (j, 0, 0)),
                 pl.BlockSpec((1, d, dff), lambda i, j: (j, 0, 0)),
                 pl.BlockSpec((1, dff, d), lambda i, j: (j, 0, 0))]
    args += [wg, wu, wd]
    if final_norm:
        in_specs.append(vec)
        args.append(g_out)
    scratch = [pltpu.VMEM((tm, d), BF16), pltpu.VMEM((tm, d), F32)]
    if moe:
        scratch.append(pltpu.VMEM((tm, LANES), F32))
    return pl.pallas_call(
        functools.partial(_mlp_kernel, moe=moe, final_norm=final_norm, n_j=n_j),
        grid=(n // tm, n_j),
        in_specs=in_specs,
        out_specs=pl.BlockSpec((tm, d), lambda i, j: (i, 0)),
        out_shape=jax.ShapeDtypeStruct((n, d), F32),
        scratch_shapes=scratch,
        compiler_params=_cparams(("arbitrary", "arbitrary")),
        name="moe_mlp" if moe else "ffn_mlp",
    )(*args)


def _rwkv_proj_kernel(h_ref, hp_ref, g_ref, mix_ref, wr_ref, wk_ref, wv_ref, w1_ref, w2_ref,
                      a1_ref, a2_ref, g1_ref, g2_ref, w0_ref, a0_ref, kk_ref, ka_ref, rk_ref,
                      hsum_ref,
                      r_out, k_out, v_out, kk_out, bb_out, lw_out, g_out, bonus_out,
                      *, tiles_per_seq):
    i = pl.program_id(0)
    g = g_ref[...]
    hn = _rms(h_ref[...], g)
    prev = _rms(hp_ref[7:8, :], g)
    prev = jnp.where(i % tiles_per_seq == 0, 0.0, prev)
    row = lax.broadcasted_iota(jnp.int32, hn.shape, 0)
    shifted = jnp.where(row == 0, prev, pltpu.roll(hn, 1, axis=0))
    dx = shifted - hn
    xr, xw, xk, xv, xa, xg = (hn + dx * mix_ref[m:m + 1, :] for m in range(6))
    r = _dot(xr, wr_ref[...])
    k = _dot(xk, wk_ref[...])
    v = _dot(xv, wv_ref[...])
    wl = w0_ref[...] + _dot(jnp.tanh(_dot(xw, w1_ref[...])), w2_ref[...])
    nwl = -wl
    w_log = -(jnp.maximum(nwl, 0.0) + jnp.log(1.0 + jnp.exp(-jnp.abs(nwl)))) - 0.5
    a = _sigmoid(a0_ref[...] + _dot(_dot(xa, a1_ref[...]), a2_ref[...]))
    gate = _dot(_sigmoid(_dot(xg, g1_ref[...])), g2_ref[...])
    hsum = hsum_ref[...]
    kk = k * kk_ref[...]
    kk = kk / jnp.maximum(jnp.sqrt(_dot_exact_rhs(kk * kk, hsum)), 1e-12)
    k2 = k * (1.0 + (a - 1.0) * ka_ref[...])
    r_out[...] = r
    k_out[...] = k2
    v_out[...] = v
    kk_out[...] = kk
    bb_out[...] = kk * a
    lw_out[...] = -jnp.exp(w_log)
    g_out[...] = gate
    bonus_out[...] = _dot_exact_rhs(r * k2 * rk_ref[...], hsum) * v


def _rwkv_proj(h, t, g, mix8, wr, wk, wv, w1, w2, a1, a2, g1, g2, w0, a0, k_k, k_a, r_k, hsum):
    n, d = h.shape
    tm = PROJ_TM
    full = lambda arr: pl.BlockSpec(arr.shape, lambda i: (0,) * arr.ndim)
    tile = pl.BlockSpec((tm, d), lambda i: (i, 0))
    prev = pl.BlockSpec((8, d), lambda i: (jnp.maximum(i * (tm // 8) - 1, 0), 0))
    consts = [g, mix8, wr, wk, wv, w1, w2, a1, a2, g1, g2, w0, a0, k_k, k_a, r_k, hsum]
    return pl.pallas_call(
        functools.partial(_rwkv_proj_kernel, tiles_per_seq=t // tm),
        grid=(n // tm,),
        in_specs=[tile, prev] + [full(c) for c in consts],
        out_specs=[tile] * 8,
        out_shape=[jax.ShapeDtypeStruct((n, d), F32)] * 8,
        compiler_params=_cparams(("arbitrary",)),
        name="rwkv_proj",
    )(h, h, *consts)


def _rwkv_rec_kernel(r_ref, k_ref, v_ref, kk_ref, bb_ref, lw_ref, bonus_ref, gate_ref,
                     lng_ref, lnb_ref, o_ref, st_ref):
    tb, c = REC_TB, REC_C
    hd = RWKV_HEAD

    @pl.when(pl.program_id(2) == 0)
    def _():
        st_ref[...] = jnp.zeros_like(st_ref)

    lw = lw_ref[0]
    rr, cc = _iota2((tb, tb))
    same = _same_block(rr, cc, c)
    cum_m = jnp.where(same & (cc <= rr), 1.0, 0.0).astype(BF16)
    end_m = jnp.where(same, 1.0, 0.0).astype(BF16)
    lp = _dot_exact_lhs(cum_m, lw)
    pend = _dot_exact_lhs(end_m, lw)
    e_neg = jnp.exp(-lp)
    e_end = jnp.exp(pend - lp)
    rt_all = r_ref[0] * jnp.exp(lp)
    kt_all = kk_ref[0] * jnp.exp(lp - lw)
    kh_all = k_ref[0] * e_neg
    bh_all = bb_ref[0] * e_neg
    khp_all = k_ref[0] * e_end
    bhp_all = bb_ref[0] * e_end
    p_end = jnp.exp(pend)
    v_all = v_ref[0]

    lane_c = lax.broadcasted_iota(jnp.int32, (c, LANES), 1)
    row_c = lax.broadcasted_iota(jnp.int32, (c, LANES), 0)
    head0 = lane_c < hd
    colm = lane_c & (hd - 1)
    strict = colm < row_c
    incl = colm <= row_c
    eye_cat = jnp.where(colm == row_c, 1.0, 0.0)
    r2, c2 = _iota2((LANES, LANES))
    bd = (r2 < hd) == (c2 < hd)
    diag = r2 == c2

    def stack(xm):
        return jnp.concatenate([jnp.where(head0, xm, 0.0), jnp.where(head0, 0.0, xm)], axis=0)

    a_bd = st_ref[...]
    ys = []
    for ci in range(tb // c):
        cs = slice(ci * c, (ci + 1) * c)
        rt, kt, kh, bh = rt_all[cs], kt_all[cs], kh_all[cs], bh_all[cs]
        khp, bhp, vv = khp_all[cs], bhp_all[cs], v_all[cs]
        lhs = jnp.concatenate([kt, rt], axis=0)
        rhs = jnp.concatenate([stack(kh), stack(bh)], axis=0)
        a_hi, a_lo = _split(lhs)
        b_hi, b_lo = _split(rhs)
        nt = (((1,), (1,)), ((), ()))
        gram = (lax.dot_general(a_hi, b_hi, nt, preferred_element_type=F32)
                + lax.dot_general(a_hi, b_lo, nt, preferred_element_type=F32)
                + lax.dot_general(a_lo, b_hi, nt, preferred_element_type=F32))
        m_kk = jnp.where(strict, gram[0:c, 0:2 * c], 0.0)
        m_kb = jnp.where(strict, gram[0:c, 2 * c:4 * c], 0.0)
        a_rk = jnp.where(incl, gram[c:2 * c, 0:2 * c], 0.0)
        a_rb = jnp.where(incl, gram[c:2 * c, 2 * c:4 * c], 0.0)
        tinv = eye_cat - m_kb
        pw = m_kb
        n_lvl = (c - 1).bit_length() - 1
        for _ in range(n_lvl):
            pw = _dot3(pw, stack(pw))
            tinv = tinv + _dot3(tinv, stack(pw))
        mv = _dot(jnp.concatenate([m_kk, a_rk], axis=0), stack(vv))
        mkk_v, ark_v = mv[0:c], mv[c:2 * c]
        kv = _dot(tinv, jnp.concatenate([stack(kt), stack(mkk_v)], axis=1))
        kp, vp = kv[:, 0:LANES], kv[:, LANES:2 * LANES]
        ab = _dot(a_rb, jnp.concatenate([stack(kp), stack(vp)], axis=1))
        rp = rt - ab[:, 0:LANES]
        y0 = ark_v - ab[:, LANES:2 * LANES]
        xs = jnp.concatenate([khp, -bhp], axis=0)
        rhs2 = jnp.concatenate([jnp.concatenate([vv, jnp.zeros_like(vv)], axis=1),
                                jnp.concatenate([vp, kp], axis=1)], axis=0)
        pp = _dot_tn(xs, rhs2)
        psi = jnp.where(bd, pp[:, 0:LANES], 0.0)
        phi = jnp.where(bd, pp[:, LANES:2 * LANES], 0.0) + jnp.where(diag, p_end[ci * c:ci * c + 1], 0.0)
        ys.append(_dot(rp, a_bd) + y0)
        a_bd = _dot(phi, a_bd) + psi
    st_ref[...] = a_bd

    y = jnp.concatenate(ys, axis=0)
    hmean = jnp.where(bd, 1.0 / hd, 0.0).astype(BF16)
    mu = _dot_exact_rhs(y, hmean)
    dlt = y - mu
    var = _dot_exact_rhs(dlt * dlt, hmean)
    yn = dlt * lax.rsqrt(var + GN_EPS) * lng_ref[...] + lnb_ref[...]
    o_ref[0] = ((yn + bonus_ref[0]) * gate_ref[0]).astype(o_ref.dtype)


def _rwkv_rec(r, k, v, kk, bb, lw, bonus, gate, ln_g, ln_b):
    bsz, t, d = r.shape
    blk = pl.BlockSpec((1, REC_TB, LANES), lambda b, h, i: (b, i, h))
    vec = pl.BlockSpec((1, LANES), lambda b, h, i: (0, h))
    return pl.pallas_call(
        _rwkv_rec_kernel,
        grid=(bsz, d // LANES, t // REC_TB),
        in_specs=[blk] * 8 + [vec, vec],
        out_specs=blk,
        out_shape=jax.ShapeDtypeStruct((bsz, t, d), BF16),
        scratch_shapes=[pltpu.VMEM((LANES, LANES), F32)],
        compiler_params=_cparams(("arbitrary", "arbitrary", "arbitrary")),
        name="rwkv_rec",
    )(r, k, v, kk, bb, lw, bonus, gate, ln_g, ln_b)


def _proj_resid_kernel(a_ref, w_ref, res_ref, o_ref):
    o_ref[...] = res_ref[...] + jnp.dot(a_ref[...], w_ref[...], preferred_element_type=F32)


def _proj_resid(a, w, res):
    n, d = res.shape
    tm = OUT_TM
    return pl.pallas_call(
        _proj_resid_kernel,
        grid=(n // tm,),
        in_specs=[pl.BlockSpec((tm, a.shape[1]), lambda i: (i, 0)),
                  pl.BlockSpec(w.shape, lambda i: (0, 0)),
                  pl.BlockSpec((tm, d), lambda i: (i, 0))],
        out_specs=pl.BlockSpec((tm, d), lambda i: (i, 0)),
        out_shape=jax.ShapeDtypeStruct((n, d), F32),
        compiler_params=_cparams(("arbitrary",)),
        name="proj_resid",
    )(a, w, res)


def kernel(x, norm_mix_g, norm_ffn_g, norm_out_g, mix_w_in, mix_w_out, gmlp_norm_g, gmlp_w_s, gmlp_b_s, hgrn_lb_logits, hgrn_onorm_g, ffn_w_gate, ffn_w_up, ffn_w_down, rwkv_mix, rwkv_w_r, rwkv_w_k, rwkv_w_v, rwkv_w_o, rwkv_w0, rwkv_w1, rwkv_w2, rwkv_a0, rwkv_a1, rwkv_a2, rwkv_g1, rwkv_g2, rwkv_k_k, rwkv_k_a, rwkv_r_k, rwkv_ln_g, rwkv_ln_b, moe_router, moe_router_b, moe_w_gate, moe_w_up, moe_w_down):
    bsz, t, d = x.shape
    n = bsz * t
    row = lambda vec: vec.reshape(1, -1).astype(F32)
    bf = lambda w: w.astype(BF16)

    lower_bounds = jnp.cumsum(jax.nn.softmax(hgrn_lb_logits.astype(F32), axis=0), axis=0)
    bias_b = jnp.repeat(gmlp_b_s[0].astype(F32).T, GMLP_DIM, axis=1)
    h = _mixer0(x, row(norm_mix_g[0]), bf(mix_w_in[0]), bf(mix_w_out[0]), row(gmlp_norm_g[0]),
                gmlp_w_s[0].astype(F32), bias_b, row(lower_bounds[0]), row(hgrn_onorm_g[0]))
    h = h.reshape(n, d)
    n_half = ffn_w_gate.shape[2] // D_FF_EXPERT
    halves = lambda w: bf(w).reshape(d, n_half, D_FF_EXPERT).transpose(1, 0, 2)
    h = _gated_mlp(h, row(norm_ffn_g[0]), halves(ffn_w_gate[0]), halves(ffn_w_up[0]),
                   bf(ffn_w_down[0]).reshape(n_half, D_FF_EXPERT, d))

    mix8 = jnp.concatenate([rwkv_mix[0].astype(F32), jnp.zeros((2, d), F32)], axis=0)
    head_id = jnp.arange(d) // RWKV_HEAD
    hsum = (head_id[:, None] == head_id[None, :]).astype(BF16)
    r, k2, v, kk, bb, lw, gate, bonus = _rwkv_proj(
        h, t, row(norm_mix_g[1]), mix8, bf(rwkv_w_r[0]), bf(rwkv_w_k[0]), bf(rwkv_w_v[0]),
        bf(rwkv_w1[0]), bf(rwkv_w2[0]), bf(rwkv_a1[0]), bf(rwkv_a2[0]), bf(rwkv_g1[0]), bf(rwkv_g2[0]),
        row(rwkv_w0[0]), row(rwkv_a0[0]), row(rwkv_k_k[0]), row(rwkv_k_a[0]), row(rwkv_r_k[0]), hsum)
    sh = lambda z: z.reshape(bsz, t, d)
    yg = _rwkv_rec(sh(r), sh(k2), sh(v), sh(kk), sh(bb), sh(lw), sh(bonus), sh(gate),
                   row(rwkv_ln_g[0]), row(rwkv_ln_b[0]))
    h = _proj_resid(yg.reshape(n, d), bf(rwkv_w_o[0]), h)
    router = jnp.zeros((d, LANES), F32).at[:, :N_EXPERTS].set(moe_router[0].astype(F32))
    router_b = jnp.full((1, LANES), -1e30, F32).at[0, :N_EXPERTS].set(moe_router_b[0].astype(F32))
    out = _gated_mlp(h, row(norm_ffn_g[1]), bf(moe_w_gate[0]), bf(moe_w_up[0]), bf(moe_w_down[0]),
                     router=router, router_b=router_b, g_out=row(norm_out_g))
    return out.reshape(bsz, t, d)
```

```python
import functools

import jax
import jax.numpy as jnp
from jax import lax
from jax.experimental import pallas as pl
from jax.experimental.pallas import tpu as pltpu

F32 = jnp.float32
BF16 = jnp.bfloat16

D_MODEL = 1024
GMLP_GROUPS = 4
GMLP_DIM = 128
GMLP_CHUNK = 128
HGRN_HEADS = 4
HGRN_DK = 128
HGRN_CHUNK = 32
MIX_HALF = 512
MIX_IN = 3072
RWKV_HEAD = 64
N_EXPERTS = 8
D_FF_EXPERT = 1408
RMS_EPS = 1e-6
LN_EPS = 1e-5
GN_EPS = 64e-5

LANES = 128
VMEM_LIMIT_BYTES = 56 * 1024 * 1024

MIX_TM = 256
MLP_TM = 512
PROJ_TM = 256
REC_C = 64
REC_HPU = 4
REC_W = REC_HPU * RWKV_HEAD
OUT_TM = 512


def _cparams(sem):
    return pltpu.CompilerParams(dimension_semantics=sem, vmem_limit_bytes=VMEM_LIMIT_BYTES)


def _dot(a, b):
    return jnp.dot(a.astype(BF16), b.astype(BF16), preferred_element_type=F32)


def _dot_nt(a, b):
    return lax.dot_general(a.astype(BF16), b.astype(BF16), (((1,), (1,)), ((), ())),
                           preferred_element_type=F32)


def _dot_tn(a, b):
    return lax.dot_general(a.astype(BF16), b.astype(BF16), (((0,), (0,)), ((), ())),
                           preferred_element_type=F32)


def _split(a):
    hi = a.astype(BF16)
    lo = (a - hi.astype(F32)).astype(BF16)
    return hi, lo


def _dot_exact_lhs(m_bf16, a):
    hi, lo = _split(a)
    return (jnp.dot(m_bf16, hi, preferred_element_type=F32)
            + jnp.dot(m_bf16, lo, preferred_element_type=F32))


def _dot_exact_rhs(a, m_bf16):
    hi, lo = _split(a)
    return (jnp.dot(hi, m_bf16, preferred_element_type=F32)
            + jnp.dot(lo, m_bf16, preferred_element_type=F32))


def _dot3(a, b):
    a_hi, a_lo = _split(a)
    b_hi, b_lo = _split(b)
    return (jnp.dot(a_hi, b_hi, preferred_element_type=F32)
            + jnp.dot(a_hi, b_lo, preferred_element_type=F32)
            + jnp.dot(a_lo, b_hi, preferred_element_type=F32))


def _rms(x, g):
    return x * lax.rsqrt(jnp.mean(x * x, axis=-1, keepdims=True) + RMS_EPS) * g


def _sigmoid(x):
    return jax.nn.sigmoid(x)


def _silu(x):
    return x * jax.nn.sigmoid(x)


def _iota2(shape):
    return (lax.broadcasted_iota(jnp.int32, shape, 0), lax.broadcasted_iota(jnp.int32, shape, 1))


def _same_block(a, b, size):
    shift = size.bit_length() - 1
    assert 1 << shift == size
    return (a >> shift) == (b >> shift)


def _mixer0_kernel(x_ref, g_ref, win_ref, wout_ref, gng_ref, ws_ref, bias_ref, lb_ref, og_ref,
                   o_ref, st_ref):
    tm = MIX_TM

    @pl.when(pl.program_id(1) == 0)
    def _():
        st_ref[...] = jnp.zeros_like(st_ref)

    x = x_ref[0]
    z = _dot(_rms(x, g_ref[...]), win_ref[...])

    u = jax.nn.gelu(z[:, 0:MIX_HALF])
    v = jax.nn.gelu(z[:, MIX_HALF:2 * MIX_HALF])
    row, col = _iota2((GMLP_CHUNK, GMLP_CHUNK))
    tril = col <= row
    mixed_groups = []
    for g in range(GMLP_GROUPS):
        gs = slice(g * GMLP_DIM, (g + 1) * GMLP_DIM)
        vg = v[:, gs]
        mu = jnp.mean(vg, axis=-1, keepdims=True)
        d = vg - mu
        var = jnp.mean(d * d, axis=-1, keepdims=True)
        vn = (d * lax.rsqrt(var + LN_EPS) * gng_ref[:, gs]).astype(BF16)
        wg = jnp.where(tril, ws_ref[g], 0.0).astype(BF16)
        parts = [jnp.dot(wg, vn[c * GMLP_CHUNK:(c + 1) * GMLP_CHUNK], preferred_element_type=F32)
                 for c in range(tm // GMLP_CHUNK)]
        mixed_groups.append(jnp.concatenate(parts, axis=0))
    bias = jnp.concatenate([bias_ref[...]] * (tm // GMLP_CHUNK), axis=0)
    y_a = u * (jnp.concatenate(mixed_groups, axis=1) + bias)

    o0 = 2 * MIX_HALF
    zq = z[:, o0:o0 + MIX_HALF]
    zf = z[:, o0 + MIX_HALF:o0 + 2 * MIX_HALF]
    zi = z[:, o0 + 2 * MIX_HALF:o0 + 3 * MIX_HALF]
    zg = z[:, o0 + 3 * MIX_HALF:o0 + 4 * MIX_HALF]
    lb = lb_ref[...]
    q = _silu(zq)
    f = lb + (1.0 - lb) * _sigmoid(zf)
    k = 1.0 - f
    lf = jnp.log(f)
    rr, cc = _iota2((tm, tm))
    same = _same_block(rr, cc, HGRN_CHUNK)
    cum_m = jnp.where(same & (cc <= rr), 1.0, 0.0).astype(BF16)
    end_m = jnp.where(same, 1.0, 0.0).astype(BF16)
    lf_hi, lf_lo = _split(lf)
    b = (jnp.dot(cum_m, lf_hi, preferred_element_type=F32)
         + jnp.dot(cum_m, lf_lo, preferred_element_type=F32))
    b_end = (jnp.dot(end_m, lf_hi, preferred_element_type=F32)
             + jnp.dot(end_m, lf_lo, preferred_element_type=F32))
    q_dec = (q * jnp.exp(b)).astype(BF16)
    k_inv = (k * jnp.exp(-b)).astype(BF16)
    k_end = (k * jnp.exp(b_end - b)).astype(BF16)
    dec = jnp.exp(b_end)
    vb = zi.astype(BF16)

    r128, c128 = _iota2((128, 128))
    intra_mask = _same_block(r128, c128, HGRN_CHUNK) & (c128 <= r128)
    og = og_ref[...]
    o_heads = []
    for h in range(HGRN_HEADS):
        hs = slice(h * HGRN_DK, (h + 1) * HGRN_DK)
        intra = []
        for rb in range(tm // 128):
            rs = slice(rb * 128, (rb + 1) * 128)
            sc = _dot_nt(q_dec[rs, hs], k_inv[rs, hs])
            sc = jnp.where(intra_mask, sc, 0.0)
            intra.append(_dot(sc, vb[rs, hs]))
        o_intra = jnp.concatenate(intra, axis=0)
        st = st_ref[h]
        inter = []
        for c in range(tm // HGRN_CHUNK):
            cs = slice(c * HGRN_CHUNK, (c + 1) * HGRN_CHUNK)
            inter.append(_dot_nt(q_dec[cs, hs], st))
            inc_t = _dot_tn(vb[cs, hs], k_end[cs, hs])
            st = st * dec[c * HGRN_CHUNK:c * HGRN_CHUNK + 1, hs] + inc_t
        st_ref[h] = st
        o_h = o_intra + jnp.concatenate(inter, axis=0)
        o_h = o_h * lax.rsqrt(jnp.mean(o_h * o_h, axis=-1, keepdims=True) + RMS_EPS) * og
        o_heads.append(o_h * _silu(zg[:, hs]))

    y = jnp.concatenate([y_a] + o_heads, axis=1)
    o_ref[0] = x + _dot(y, wout_ref[...])


def _mixer0(x, g, w_in, w_out, gn_g, w_s, bias_b, lb, og):
    bsz, t, d = x.shape
    grid = (bsz, t // MIX_TM)
    const = lambda shape: pl.BlockSpec(shape, lambda b, i: (0,) * len(shape))
    return pl.pallas_call(
        _mixer0_kernel,
        grid=grid,
        in_specs=[
            pl.BlockSpec((1, MIX_TM, d), lambda b, i: (b, i, 0)),
            const((1, d)),
            const((d, MIX_IN)),
            const((2 * MIX_HALF, d)),
            const((1, MIX_HALF)),
            const((GMLP_GROUPS, GMLP_CHUNK, GMLP_CHUNK)),
            const((GMLP_CHUNK, MIX_HALF)),
            const((1, MIX_HALF)),
            const((1, HGRN_DK)),
        ],
        out_specs=pl.BlockSpec((1, MIX_TM, d), lambda b, i: (b, i, 0)),
        out_shape=jax.ShapeDtypeStruct((bsz, t, d), F32),
        scratch_shapes=[pltpu.VMEM((HGRN_HEADS, HGRN_DK, HGRN_DK), F32)],
        compiler_params=_cparams(("arbitrary", "arbitrary")),
        name="mixer0",
    )(x, g, w_in, w_out, gn_g, w_s, bias_b, lb, og)


def _mlp_kernel(*refs, moe, final_norm, n_j):
    refs = list(refs)
    x_ref, g_ref = refs[0], refs[1]
    pos = 2
    if moe:
        router_ref, rb_ref = refs[pos], refs[pos + 1]
        pos += 2
    wg_ref, wu_ref, wd_ref = refs[pos:pos + 3]
    pos += 3
    if final_norm:
        gout_ref = refs[pos]
        pos += 1
    o_ref = refs[pos]
    hn_ref, acc_ref = refs[pos + 1], refs[pos + 2]
    if moe:
        gates_ref = refs[pos + 3]
    j = pl.program_id(1)

    @pl.when(j == 0)
    def _():
        hn = _rms(x_ref[...], g_ref[...])
        hn_ref[...] = hn.astype(BF16)
        acc_ref[...] = jnp.zeros_like(acc_ref)
        if moe:
            logits = _dot3(hn, router_ref[...]) + rb_ref[...]
            lane = lax.broadcasted_iota(jnp.int32, logits.shape, 1)
            m1 = jnp.max(logits, axis=-1, keepdims=True)
            i1 = jnp.min(jnp.where(logits == m1, lane, LANES), axis=-1, keepdims=True)
            rest = jnp.where(lane == i1, -jnp.inf, logits)
            m2 = jnp.max(rest, axis=-1, keepdims=True)
            i2 = jnp.min(jnp.where(rest == m2, lane, LANES), axis=-1, keepdims=True)
            e2 = jnp.exp(m2 - m1)
            den = 1.0 + e2
            gates_ref[...] = (jnp.where(lane == i1, 1.0 / den, 0.0)
                              + jnp.where(lane == i2, e2 / den, 0.0))

    hn = hn_ref[...]
    a = jnp.dot(hn, wg_ref[0], preferred_element_type=F32)
    b = jnp.dot(hn, wu_ref[0], preferred_element_type=F32)
    y = _dot(_silu(a) * b, wd_ref[0])
    if moe:
        gates = gates_ref[...]
        lane = lax.broadcasted_iota(jnp.int32, gates.shape, 1)
        y = y * jnp.sum(jnp.where(lane == j, gates, 0.0), axis=-1, keepdims=True)
    acc_ref[...] += y

    @pl.when(j == n_j - 1)
    def _():
        out = x_ref[...] + acc_ref[...]
        if final_norm:
            out = _rms(out, gout_ref[...])
        o_ref[...] = out


def _gated_mlp(x, g, wg, wu, wd, router=None, router_b=None, g_out=None):
    n, d = x.shape
    n_j, _, dff = wg.shape
    moe = router is not None
    final_norm = g_out is not None
    tm = MLP_TM
    vec = pl.BlockSpec((1, d), lambda i, j: (0, 0))
    in_specs = [pl.BlockSpec((tm, d), lambda i, j: (i, 0)), vec]
    args = [x, g]
    if moe:
        in_specs += [pl.BlockSpec((d, LANES), lambda i, j: (0, 0)),
                     pl.BlockSpec((1, LANES), lambda i, j: (0, 0))]
        args += [router, router_b]
    in_specs += [pl.BlockSpec((1, d, dff), lambda i, j: (j, 0, 0)),
                 pl.BlockSpec((1, d, dff), lambda i, j: (j, 0, 0)),
                 pl.BlockSpec((1, dff, d), lambda i, j: (j, 0, 0))]
    args += [wg, wu, wd]
    if final_norm:
        in_specs.append(vec)
        args.append(g_out)
    scratch = [pltpu.VMEM((tm, d), BF16), pltpu.VMEM((tm, d), F32)]
    if moe:
        scratch.append(pltpu.VMEM((tm, LANES), F32))
    return pl.pallas_call(
        functools.partial(_mlp_kernel, moe=moe, final_norm=final_norm, n_j=n_j),
        grid=(n // tm, n_j),
        in_specs=in_specs,
        out_specs=pl.BlockSpec((tm, d), lambda i, j: (i, 0)),
        out_shape=jax.ShapeDtypeStruct((n, d), F32),
        scratch_shapes=scratch,
        compiler_params=_cparams(("arbitrary", "arbitrary")),
        name="moe_mlp" if moe else "ffn_mlp",
    )(*args)


def _rwkv_proj_kernel(h_ref, hp_ref, g_ref, mix_ref, wr_ref, wk_ref, wv_ref, w1_ref, w2_ref,
                      a1_ref, a2_ref, g1_ref, g2_ref, w0_ref, a0_ref, kk_ref, ka_ref, rk_ref,
                      hsum_ref,
                      r_out, k_out, v_out, kk_out, bb_out, lw_out, g_out, bonus_out,
                      *, tiles_per_seq):
    i = pl.program_id(0)
    g = g_ref[...]
    hn = _rms(h_ref[...], g)
    prev = _rms(hp_ref[7:8, :], g)
    prev = jnp.where(i % tiles_per_seq == 0, 0.0, prev)
    row = lax.broadcasted_iota(jnp.int32, hn.shape, 0)
    shifted = jnp.where(row == 0, prev, pltpu.roll(hn, 1, axis=0))
    dx = shifted - hn
    xr, xw, xk, xv, xa, xg = (hn + dx * mix_ref[m:m + 1, :] for m in range(6))
    r = _dot(xr, wr_ref[...])
    k = _dot(xk, wk_ref[...])
    v = _dot(xv, wv_ref[...])
    wl = w0_ref[...] + _dot(jnp.tanh(_dot(xw, w1_ref[...])), w2_ref[...])
    nwl = -wl
    w_log = -(jnp.maximum(nwl, 0.0) + jnp.log(1.0 + jnp.exp(-jnp.abs(nwl)))) - 0.5
    a = _sigmoid(a0_ref[...] + _dot(_dot(xa, a1_ref[...]), a2_ref[...]))
    gate = _dot(_sigmoid(_dot(xg, g1_ref[...])), g2_ref[...])
    hsum = hsum_ref[...]
    kk = k * kk_ref[...]
    kk = kk / jnp.maximum(jnp.sqrt(_dot_exact_rhs(kk * kk, hsum)), 1e-12)
    k2 = k * (1.0 + (a - 1.0) * ka_ref[...])
    r_out[...] = r
    k_out[...] = k2
    v_out[...] = v
    kk_out[...] = kk
    bb_out[...] = kk * a
    lw_out[...] = -jnp.exp(w_log)
    g_out[...] = gate
    bonus_out[...] = _dot_exact_rhs(r * k2 * rk_ref[...], hsum) * v


def _rwkv_proj(h, t, g, mix8, wr, wk, wv, w1, w2, a1, a2, g1, g2, w0, a0, k_k, k_a, r_k, hsum):
    n, d = h.shape
    tm = PROJ_TM
    full = lambda arr: pl.BlockSpec(arr.shape, lambda i: (0,) * arr.ndim)
    tile = pl.BlockSpec((tm, d), lambda i: (i, 0))
    prev = pl.BlockSpec((8, d), lambda i: (jnp.maximum(i * (tm // 8) - 1, 0), 0))
    consts = [g, mix8, wr, wk, wv, w1, w2, a1, a2, g1, g2, w0, a0, k_k, k_a, r_k, hsum]
    return pl.pallas_call(
        functools.partial(_rwkv_proj_kernel, tiles_per_seq=t // tm),
        grid=(n // tm,),
        in_specs=[tile, prev] + [full(c) for c in consts],
        out_specs=[tile] * 8,
        out_shape=[jax.ShapeDtypeStruct((n, d), F32)] * 8,
        compiler_params=_cparams(("arbitrary",)),
        name="rwkv_proj",
    )(h, h, *consts)


def _rwkv_rec_kernel(r_ref, k_ref, v_ref, kk_ref, bb_ref, lw_ref, bonus_ref, gate_ref,
                     lng_ref, lnb_ref, o_ref, st_ref):
    c, hd, w = REC_C, RWKV_HEAD, REC_W
    bsz, _, d = r_ref.shape
    units = [(b, u) for b in range(bsz) for u in range(d // w)]

    @pl.when(pl.program_id(0) == 0)
    def _():
        st_ref[...] = jnp.zeros_like(st_ref)

    rr, cc = _iota2((c, c))
    cum_m = jnp.where(cc <= rr, 1.0, 0.0).astype(BF16)
    lane_c = lax.broadcasted_iota(jnp.int32, (c, w), 1)
    row_c = lax.broadcasted_iota(jnp.int32, (c, w), 0)
    head_c = lane_c >> (hd.bit_length() - 1)
    colm = lane_c & (hd - 1)
    strict = colm < row_c
    incl = colm <= row_c
    eye_cat = jnp.where(colm == row_c, 1.0, 0.0)
    r2, c2 = _iota2((w, w))
    bd = _same_block(r2, c2, hd)
    diag = r2 == c2
    hmean = jnp.where(bd, 1.0 / hd, 0.0).astype(BF16)

    def stack(xm):
        return jnp.concatenate([jnp.where(head_c == h, xm, 0.0) for h in range(w // hd)], axis=0)

    prep = []
    for b in range(bsz):
        lw = lw_ref[b]
        lp = _dot_exact_lhs(cum_m, lw)
        pend = lp[c - 1:c, :]
        e_neg = jnp.exp(-lp)
        e_end = jnp.exp(pend - lp)
        kb, bbv = k_ref[b], bb_ref[b]
        prep.append(dict(rt=r_ref[b] * jnp.exp(lp), kt=kk_ref[b] * jnp.exp(lp - lw),
                         kh=kb * e_neg, bh=bbv * e_neg, khp=kb * e_end, bhp=bbv * e_end,
                         p_end=jnp.exp(pend), v=v_ref[b]))

    def usl(name, b, u):
        return prep[b][name][:, u * w:(u + 1) * w]

    m_kk, m_kb, a_rk, a_rb = [], [], [], []
    for (b, u) in units:
        lhs = jnp.concatenate([usl('kt', b, u), usl('rt', b, u)], axis=0)
        rhs = jnp.concatenate([stack(usl('kh', b, u)), stack(usl('bh', b, u))], axis=0)
        gram = _dot_nt(lhs, rhs)
        m_kk.append(jnp.where(strict, gram[0:c, 0:w], 0.0))
        m_kb.append(jnp.where(strict, gram[0:c, w:2 * w], 0.0))
        a_rk.append(jnp.where(incl, gram[c:2 * c, 0:w], 0.0))
        a_rb.append(jnp.where(incl, gram[c:2 * c, w:2 * w], 0.0))
    tinv = [eye_cat - m for m in m_kb]
    pw = [_dot(m, stack(m)) for m in m_kb]
    mv = [_dot(jnp.concatenate([m_kk[i], a_rk[i]], axis=0), stack(usl('v', b, u)))
          for i, (b, u) in enumerate(units)]
    n_lvl = (c - 1).bit_length() - 1
    for lvl in range(n_lvl):
        last = lvl == n_lvl - 1
        for i in range(len(units)):
            spw = stack(pw[i])
            if last:
                tinv[i] = tinv[i] + _dot(tinv[i], spw)
            else:
                res = _dot(jnp.concatenate([tinv[i], pw[i]], axis=0), spw)
                tinv[i] = tinv[i] + res[0:c]
                pw[i] = res[c:2 * c]
    kv = [_dot(tinv[i], jnp.concatenate([stack(usl('kt', b, u)), stack(mv[i][0:c])], axis=1))
          for i, (b, u) in enumerate(units)]
    rp, y0, phi, psi = [], [], [], []
    for i, (b, u) in enumerate(units):
        kp, vp = kv[i][:, 0:w], kv[i][:, w:2 * w]
        ab = _dot(a_rb[i], jnp.concatenate([stack(kp), stack(vp)], axis=1))
        rp.append(usl('rt', b, u) - ab[:, 0:w])
        y0.append(mv[i][c:2 * c] - ab[:, w:2 * w])
        vv = usl('v', b, u)
        xs = jnp.concatenate([usl('khp', b, u), -usl('bhp', b, u)], axis=0)
        rhs2 = jnp.concatenate([jnp.concatenate([vv, jnp.zeros_like(vv)], axis=1),
                                jnp.concatenate([vp, kp], axis=1)], axis=0)
        pp = _dot_tn(xs, rhs2)
        psi.append(jnp.where(bd, pp[:, 0:w], 0.0))
        phi.append(jnp.where(bd, pp[:, w:2 * w], 0.0) + jnp.where(diag, usl('p_end', b, u), 0.0))
    ys = []
    for i, (b, u) in enumerate(units):
        a_bd = st_ref[b, u]
        ys.append(_dot(rp[i], a_bd) + y0[i])
        st_ref[b, u] = _dot(phi[i], a_bd) + psi[i]
    for i, (b, u) in enumerate(units):
        ls = slice(u * w, (u + 1) * w)
        mu = _dot(ys[i], hmean)
        dlt = ys[i] - mu
        var = _dot(dlt * dlt, hmean)
        yn = dlt * lax.rsqrt(var + GN_EPS) * lng_ref[:, ls] + lnb_ref[:, ls]
        o_ref[b, :, ls] = ((yn + bonus_ref[b, :, ls]) * gate_ref[b, :, ls]).astype(o_ref.dtype)


def _rwkv_rec(r, k, v, kk, bb, lw, bonus, gate, ln_g, ln_b):
    bsz, t, d = r.shape
    blk = pl.BlockSpec((bsz, REC_C, d), lambda i: (0, i, 0))
    vec = pl.BlockSpec((1, d), lambda i: (0, 0))
    return pl.pallas_call(
        _rwkv_rec_kernel,
        grid=(t // REC_C,),
        in_specs=[blk] * 8 + [vec, vec],
        out_specs=blk,
        out_shape=jax.ShapeDtypeStruct((bsz, t, d), BF16),
        scratch_shapes=[pltpu.VMEM((bsz, d // REC_W, REC_W, REC_W), F32)],
        compiler_params=_cparams(("arbitrary",)),
        name="rwkv_rec",
    )(r, k, v, kk, bb, lw, bonus, gate, ln_g, ln_b)


def _proj_resid_kernel(a_ref, w_ref, res_ref, o_ref):
    o_ref[...] = res_ref[...] + jnp.dot(a_ref[...], w_ref[...], preferred_element_type=F32)


def _proj_resid(a, w, res):
    n, d = res.shape
    tm = OUT_TM
    return pl.pallas_call(
        _proj_resid_kernel,
        grid=(n // tm,),
        in_specs=[pl.BlockSpec((tm, a.shape[1]), lambda i: (i, 0)),
                  pl.BlockSpec(w.shape, lambda i: (0, 0)),
                  pl.BlockSpec((tm, d), lambda i: (i, 0))],
        out_specs=pl.BlockSpec((tm, d), lambda i: (i, 0)),
        out_shape=jax.ShapeDtypeStruct((n, d), F32),
        compiler_params=_cparams(("arbitrary",)),
        name="proj_resid",
    )(a, w, res)


def kernel(x, norm_mix_g, norm_ffn_g, norm_out_g, mix_w_in, mix_w_out, gmlp_norm_g, gmlp_w_s, gmlp_b_s, hgrn_lb_logits, hgrn_onorm_g, ffn_w_gate, ffn_w_up, ffn_w_down, rwkv_mix, rwkv_w_r, rwkv_w_k, rwkv_w_v, rwkv_w_o, rwkv_w0, rwkv_w1, rwkv_w2, rwkv_a0, rwkv_a1, rwkv_a2, rwkv_g1, rwkv_g2, rwkv_k_k, rwkv_k_a, rwkv_r_k, rwkv_ln_g, rwkv_ln_b, moe_router, moe_router_b, moe_w_gate, moe_w_up, moe_w_down):
    bsz, t, d = x.shape
    n = bsz * t
    row = lambda vec: vec.reshape(1, -1).astype(F32)
    bf = lambda w: w.astype(BF16)

    lower_bounds = jnp.cumsum(jax.nn.softmax(hgrn_lb_logits.astype(F32), axis=0), axis=0)
    bias_b = jnp.repeat(gmlp_b_s[0].astype(F32).T, GMLP_DIM, axis=1)
    h = _mixer0(x, row(norm_mix_g[0]), bf(mix_w_in[0]), bf(mix_w_out[0]), row(gmlp_norm_g[0]),
                gmlp_w_s[0].astype(F32), bias_b, row(lower_bounds[0]), row(hgrn_onorm_g[0]))
    h = h.reshape(n, d)
    n_half = ffn_w_gate.shape[2] // D_FF_EXPERT
    halves = lambda w: bf(w).reshape(d, n_half, D_FF_EXPERT).transpose(1, 0, 2)
    h = _gated_mlp(h, row(norm_ffn_g[0]), halves(ffn_w_gate[0]), halves(ffn_w_up[0]),
                   bf(ffn_w_down[0]).reshape(n_half, D_FF_EXPERT, d))

    mix8 = jnp.concatenate([rwkv_mix[0].astype(F32), jnp.zeros((2, d), F32)], axis=0)
    head_id = jnp.arange(d) // RWKV_HEAD
    hsum = (head_id[:, None] == head_id[None, :]).astype(BF16)
    r, k2, v, kk, bb, lw, gate, bonus = _rwkv_proj(
        h, t, row(norm_mix_g[1]), mix8, bf(rwkv_w_r[0]), bf(rwkv_w_k[0]), bf(rwkv_w_v[0]),
        bf(rwkv_w1[0]), bf(rwkv_w2[0]), bf(rwkv_a1[0]), bf(rwkv_a2[0]), bf(rwkv_g1[0]), bf(rwkv_g2[0]),
        row(rwkv_w0[0]), row(rwkv_a0[0]), row(rwkv_k_k[0]), row(rwkv_k_a[0]), row(rwkv_r_k[0]), hsum)
    sh = lambda z: z.reshape(bsz, t, d)
    yg = _rwkv_rec(sh(r), sh(k2), sh(v), sh(kk), sh(bb), sh(lw), sh(bonus), sh(gate),
                   row(rwkv_ln_g[0]), row(rwkv_ln_b[0]))
    h = _proj_resid(yg.reshape(n, d), bf(rwkv_w_o[0]), h)
    router = jnp.zeros((d, LANES), F32).at[:, :N_EXPERTS].set(moe_router[0].astype(F32))
    router_b = jnp.full((1, LANES), -1e30, F32).at[0, :N_EXPERTS].set(moe_router_b[0].astype(F32))
    out = _gated_mlp(h, row(norm_ffn_g[1]), bf(moe_w_gate[0]), bf(moe_w_up[0]), bf(moe_w_down[0]),
                     router=router, router_b=router_b, g_out=row(norm_out_g))
    return out.reshape(bsz, t, d)
```

```python
import functools

import jax
import jax.numpy as jnp
from jax import lax
from jax.experimental import pallas as pl
from jax.experimental.pallas import tpu as pltpu

F32 = jnp.float32
BF16 = jnp.bfloat16

D_MODEL = 1024
GMLP_GROUPS = 4
GMLP_DIM = 128
GMLP_CHUNK = 128
HGRN_HEADS = 4
HGRN_DK = 128
HGRN_CHUNK = 32
MIX_HALF = 512
MIX_IN = 3072
RWKV_HEAD = 64
N_EXPERTS = 8
D_FF_EXPERT = 1408
RMS_EPS = 1e-6
LN_EPS = 1e-5
GN_EPS = 64e-5

LANES = 128
VMEM_LIMIT_BYTES = 56 * 1024 * 1024

MIX_TM = 256
MLP_TM = 512
MOE_TT = 1024
MOE_CH = 128
PROJ_TM = 256
REC_C = 64
REC_HPU = 4
REC_W = REC_HPU * RWKV_HEAD
OUT_TM = 512


def _cparams(sem):
    return pltpu.CompilerParams(dimension_semantics=sem, vmem_limit_bytes=VMEM_LIMIT_BYTES)


def _dot(a, b):
    return jnp.dot(a.astype(BF16), b.astype(BF16), preferred_element_type=F32)


def _dot_nt(a, b):
    return lax.dot_general(a.astype(BF16), b.astype(BF16), (((1,), (1,)), ((), ())),
                           preferred_element_type=F32)


def _dot_tn(a, b):
    return lax.dot_general(a.astype(BF16), b.astype(BF16), (((0,), (0,)), ((), ())),
                           preferred_element_type=F32)


def _split(a):
    hi = a.astype(BF16)
    lo = (a - hi.astype(F32)).astype(BF16)
    return hi, lo


def _dot_exact_lhs(m_bf16, a):
    hi, lo = _split(a)
    return (jnp.dot(m_bf16, hi, preferred_element_type=F32)
            + jnp.dot(m_bf16, lo, preferred_element_type=F32))


def _dot_exact_rhs(a, m_bf16):
    hi, lo = _split(a)
    return (jnp.dot(hi, m_bf16, preferred_element_type=F32)
            + jnp.dot(lo, m_bf16, preferred_element_type=F32))


def _dot3(a, b):
    a_hi, a_lo = _split(a)
    b_hi, b_lo = _split(b)
    return (jnp.dot(a_hi, b_hi, preferred_element_type=F32)
            + jnp.dot(a_hi, b_lo, preferred_element_type=F32)
            + jnp.dot(a_lo, b_hi, preferred_element_type=F32))


def _rms(x, g):
    return x * lax.rsqrt(jnp.mean(x * x, axis=-1, keepdims=True) + RMS_EPS) * g


def _sigmoid(x):
    return jax.nn.sigmoid(x)


def _silu(x):
    return x * jax.nn.sigmoid(x)


def _iota2(shape):
    return (lax.broadcasted_iota(jnp.int32, shape, 0), lax.broadcasted_iota(jnp.int32, shape, 1))


def _same_block(a, b, size):
    shift = size.bit_length() - 1
    assert 1 << shift == size
    return (a >> shift) == (b >> shift)


def _mixer0_kernel(x_ref, g_ref, win_ref, wout_ref, gng_ref, ws_ref, bias_ref, lb_ref, og_ref,
                   o_ref, st_ref):
    tm = MIX_TM

    @pl.when(pl.program_id(1) == 0)
    def _():
        st_ref[...] = jnp.zeros_like(st_ref)

    x = x_ref[0]
    z = _dot(_rms(x, g_ref[...]), win_ref[...])

    u = jax.nn.gelu(z[:, 0:MIX_HALF])
    v = jax.nn.gelu(z[:, MIX_HALF:2 * MIX_HALF])
    row, col = _iota2((GMLP_CHUNK, GMLP_CHUNK))
    tril = col <= row
    mixed_groups = []
    for g in range(GMLP_GROUPS):
        gs = slice(g * GMLP_DIM, (g + 1) * GMLP_DIM)
        vg = v[:, gs]
        mu = jnp.mean(vg, axis=-1, keepdims=True)
        d = vg - mu
        var = jnp.mean(d * d, axis=-1, keepdims=True)
        vn = (d * lax.rsqrt(var + LN_EPS) * gng_ref[:, gs]).astype(BF16)
        wg = jnp.where(tril, ws_ref[g], 0.0).astype(BF16)
        parts = [jnp.dot(wg, vn[c * GMLP_CHUNK:(c + 1) * GMLP_CHUNK], preferred_element_type=F32)
                 for c in range(tm // GMLP_CHUNK)]
        mixed_groups.append(jnp.concatenate(parts, axis=0))
    bias = jnp.concatenate([bias_ref[...]] * (tm // GMLP_CHUNK), axis=0)
    y_a = u * (jnp.concatenate(mixed_groups, axis=1) + bias)

    o0 = 2 * MIX_HALF
    zq = z[:, o0:o0 + MIX_HALF]
    zf = z[:, o0 + MIX_HALF:o0 + 2 * MIX_HALF]
    zi = z[:, o0 + 2 * MIX_HALF:o0 + 3 * MIX_HALF]
    zg = z[:, o0 + 3 * MIX_HALF:o0 + 4 * MIX_HALF]
    lb = lb_ref[...]
    q = _silu(zq)
    f = lb + (1.0 - lb) * _sigmoid(zf)
    k = 1.0 - f
    lf = jnp.log(f)
    rr, cc = _iota2((tm, tm))
    same = _same_block(rr, cc, HGRN_CHUNK)
    cum_m = jnp.where(same & (cc <= rr), 1.0, 0.0).astype(BF16)
    end_m = jnp.where(same, 1.0, 0.0).astype(BF16)
    lf_hi, lf_lo = _split(lf)
    b = (jnp.dot(cum_m, lf_hi, preferred_element_type=F32)
         + jnp.dot(cum_m, lf_lo, preferred_element_type=F32))
    b_end = (jnp.dot(end_m, lf_hi, preferred_element_type=F32)
             + jnp.dot(end_m, lf_lo, preferred_element_type=F32))
    q_dec = (q * jnp.exp(b)).astype(BF16)
    k_inv = (k * jnp.exp(-b)).astype(BF16)
    k_end = (k * jnp.exp(b_end - b)).astype(BF16)
    dec = jnp.exp(b_end)
    vb = zi.astype(BF16)

    r128, c128 = _iota2((128, 128))
    intra_mask = _same_block(r128, c128, HGRN_CHUNK) & (c128 <= r128)
    og = og_ref[...]
    o_heads = []
    for h in range(HGRN_HEADS):
        hs = slice(h * HGRN_DK, (h + 1) * HGRN_DK)
        intra = []
        for rb in range(tm // 128):
            rs = slice(rb * 128, (rb + 1) * 128)
            sc = _dot_nt(q_dec[rs, hs], k_inv[rs, hs])
            sc = jnp.where(intra_mask, sc, 0.0)
            intra.append(_dot(sc, vb[rs, hs]))
        o_intra = jnp.concatenate(intra, axis=0)
        st = st_ref[h]
        inter = []
        for c in range(tm // HGRN_CHUNK):
            cs = slice(c * HGRN_CHUNK, (c + 1) * HGRN_CHUNK)
            inter.append(_dot_nt(q_dec[cs, hs], st))
            inc_t = _dot_tn(vb[cs, hs], k_end[cs, hs])
            st = st * dec[c * HGRN_CHUNK:c * HGRN_CHUNK + 1, hs] + inc_t
        st_ref[h] = st
        o_h = o_intra + jnp.concatenate(inter, axis=0)
        o_h = o_h * lax.rsqrt(jnp.mean(o_h * o_h, axis=-1, keepdims=True) + RMS_EPS) * og
        o_heads.append(o_h * _silu(zg[:, hs]))

    y = jnp.concatenate([y_a] + o_heads, axis=1)
    o_ref[0] = x + _dot(y, wout_ref[...])


def _mixer0(x, g, w_in, w_out, gn_g, w_s, bias_b, lb, og):
    bsz, t, d = x.shape
    grid = (bsz, t // MIX_TM)
    const = lambda shape: pl.BlockSpec(shape, lambda b, i: (0,) * len(shape))
    return pl.pallas_call(
        _mixer0_kernel,
        grid=grid,
        in_specs=[
            pl.BlockSpec((1, MIX_TM, d), lambda b, i: (b, i, 0)),
            const((1, d)),
            const((d, MIX_IN)),
            const((2 * MIX_HALF, d)),
            const((1, MIX_HALF)),
            const((GMLP_GROUPS, GMLP_CHUNK, GMLP_CHUNK)),
            const((GMLP_CHUNK, MIX_HALF)),
            const((1, MIX_HALF)),
            const((1, HGRN_DK)),
        ],
        out_specs=pl.BlockSpec((1, MIX_TM, d), lambda b, i: (b, i, 0)),
        out_shape=jax.ShapeDtypeStruct((bsz, t, d), F32),
        scratch_shapes=[pltpu.VMEM((HGRN_HEADS, HGRN_DK, HGRN_DK), F32)],
        compiler_params=_cparams(("arbitrary", "arbitrary")),
        name="mixer0",
    )(x, g, w_in, w_out, gn_g, w_s, bias_b, lb, og)


def _ffn_kernel(x_ref, g_ref, wg_ref, wu_ref, wd_ref, o_ref, hn_ref, acc_ref, *, n_j):
    j = pl.program_id(1)

    @pl.when(j == 0)
    def _():
        hn_ref[...] = _rms(x_ref[...], g_ref[...]).astype(BF16)
        acc_ref[...] = jnp.zeros_like(acc_ref)

    hn = hn_ref[...]
    a = jnp.dot(hn, wg_ref[0], preferred_element_type=F32)
    b = jnp.dot(hn, wu_ref[0], preferred_element_type=F32)
    acc_ref[...] += _dot(_silu(a) * b, wd_ref[0])

    @pl.when(j == n_j - 1)
    def _():
        o_ref[...] = x_ref[...] + acc_ref[...]


def _ffn(x, g, wg, wu, wd):
    n, d = x.shape
    n_j, _, dff = wg.shape
    tm = MLP_TM
    return pl.pallas_call(
        functools.partial(_ffn_kernel, n_j=n_j),
        grid=(n // tm, n_j),
        in_specs=[pl.BlockSpec((tm, d), lambda i, j: (i, 0)),
                  pl.BlockSpec((1, d), lambda i, j: (0, 0)),
                  pl.BlockSpec((1, d, dff), lambda i, j: (j, 0, 0)),
                  pl.BlockSpec((1, d, dff), lambda i, j: (j, 0, 0)),
                  pl.BlockSpec((1, dff, d), lambda i, j: (j, 0, 0))],
        out_specs=pl.BlockSpec((tm, d), lambda i, j: (i, 0)),
        out_shape=jax.ShapeDtypeStruct((n, d), F32),
        scratch_shapes=[pltpu.VMEM((tm, d), BF16), pltpu.VMEM((tm, d), F32)],
        compiler_params=_cparams(("arbitrary", "arbitrary")),
        name="ffn_mlp",
    )(x, g, wg, wu, wd)


def _moe_router_kernel(x_ref, g_ref, router_ref, rb_ref, tri_ref, hn_ref, gates_ref, rank_ref, cnt_ref):
    hn = _rms(x_ref[...], g_ref[...])
    hn_ref[...] = hn.astype(BF16)
    logits = _dot3(hn, router_ref[...]) + rb_ref[...]
    lane = lax.broadcasted_iota(jnp.int32, logits.shape, 1)
    m1 = jnp.max(logits, axis=-1, keepdims=True)
    i1 = jnp.min(jnp.where(logits == m1, lane, LANES), axis=-1, keepdims=True)
    rest = jnp.where(lane == i1, -jnp.inf, logits)
    m2 = jnp.max(rest, axis=-1, keepdims=True)
    i2 = jnp.min(jnp.where(rest == m2, lane, LANES), axis=-1, keepdims=True)
    e2 = jnp.exp(m2 - m1)
    den = 1.0 + e2
    gates_ref[...] = jnp.where(lane == i1, 1.0 / den, 0.0) + jnp.where(lane == i2, e2 / den, 0.0)
    sel = (lane == i1) | (lane == i2)
    sel_f = jnp.where(sel, 1.0, 0.0)
    rank = jnp.dot(tri_ref[...], sel_f.astype(BF16), preferred_element_type=F32)
    rank_ref[...] = jnp.where(sel, rank, -1.0)
    cnt_ref[0] = jnp.broadcast_to(jnp.sum(sel_f, axis=0, keepdims=True), cnt_ref.shape[1:])


def _moe_router(x, g, router, router_b, tri):
    n, d = x.shape
    tm = MOE_TT
    tile = lambda w: pl.BlockSpec((tm, w), lambda i: (i, 0))
    full = lambda arr: pl.BlockSpec(arr.shape, lambda i: (0,) * arr.ndim)
    return pl.pallas_call(
        _moe_router_kernel,
        grid=(n // tm,),
        in_specs=[tile(d), full(g), full(router), full(router_b), full(tri)],
        out_specs=[tile(d), tile(LANES), tile(LANES), pl.BlockSpec((1, 8, LANES), lambda i: (i, 0, 0))],
        out_shape=[jax.ShapeDtypeStruct((n, d), BF16), jax.ShapeDtypeStruct((n, LANES), F32),
                   jax.ShapeDtypeStruct((n, LANES), F32), jax.ShapeDtypeStruct((n // tm, 8, LANES), F32)],
        compiler_params=_cparams(("arbitrary",)),
        name="moe_router",
    )(x, g, router, router_b, tri)


def _moe_expert_kernel(nch_ref, x_ref, hn_ref, gates_ref, rank_ref, wg_ref, wu_ref, wd_ref, gout_ref,
                       o_ref, acc_ref, *, n_e):
    i, e = pl.program_id(0), pl.program_id(1)
    tt, ch = MOE_TT, MOE_CH

    @pl.when(e == 0)
    def _():
        acc_ref[...] = jnp.zeros_like(acc_ref)

    lane = lax.broadcasted_iota(jnp.int32, (tt, LANES), 1)
    rank_col = jnp.sum(jnp.where(lane == e, rank_ref[...], 0.0), axis=-1, keepdims=True)
    gates = jnp.where(lane == e, gates_ref[...], 0.0)
    g_hi, g_lo = _split(gates)
    slot = lax.broadcasted_iota(jnp.int32, (tt, ch), 1).astype(F32)
    lane_ch = lax.broadcasted_iota(jnp.int32, (ch, LANES), 1)
    tn = (((0,), (0,)), ((), ()))

    def chunk(k, carry):
        pt = jnp.where(rank_col - (k * ch).astype(F32) == slot, 1.0, 0.0).astype(BF16)
        xs = lax.dot_general(pt, hn_ref[...], tn, preferred_element_type=F32).astype(BF16)
        a = jnp.dot(xs, wg_ref[0], preferred_element_type=F32)
        b = jnp.dot(xs, wu_ref[0], preferred_element_type=F32)
        y = _dot(_silu(a) * b, wd_ref[0])
        gch = (lax.dot_general(pt, g_hi, tn, preferred_element_type=F32)
               + lax.dot_general(pt, g_lo, tn, preferred_element_type=F32))
        gcol = jnp.sum(jnp.where(lane_ch == e, gch, 0.0), axis=-1, keepdims=True)
        acc_ref[...] += jnp.dot(pt, (y * gcol).astype(BF16), preferred_element_type=F32)
        return carry

    lax.fori_loop(0, nch_ref[i, e], chunk, 0)

    @pl.when(e == n_e - 1)
    def _():
        o_ref[...] = _rms(x_ref[...] + acc_ref[...], gout_ref[...])


def _moe_experts(nch, x, hn, gates, rank, wg, wu, wd, g_out):
    n, d = x.shape
    n_e, _, dff = wg.shape
    tt = MOE_TT
    tile = lambda w: pl.BlockSpec((tt, w), lambda i, e, nch: (i, 0))
    grid_spec = pltpu.PrefetchScalarGridSpec(
        num_scalar_prefetch=1,
        grid=(n // tt, n_e),
        in_specs=[tile(d), tile(d), tile(LANES), tile(LANES),
                  pl.BlockSpec((1, d, dff), lambda i, e, nch: (e, 0, 0)),
                  pl.BlockSpec((1, d, dff), lambda i, e, nch: (e, 0, 0)),
                  pl.BlockSpec((1, dff, d), lambda i, e, nch: (e, 0, 0)),
                  pl.BlockSpec((1, d), lambda i, e, nch: (0, 0))],
        out_specs=tile(d),
        scratch_shapes=[pltpu.VMEM((tt, d), F32)],
    )
    return pl.pallas_call(
        functools.partial(_moe_expert_kernel, n_e=n_e),
        grid_spec=grid_spec,
        out_shape=jax.ShapeDtypeStruct((n, d), F32),
        compiler_params=_cparams(("arbitrary", "arbitrary")),
        name="moe_experts",
    )(nch, x, hn, gates, rank, wg, wu, wd, g_out)


def _rwkv_proj_kernel(h_ref, hp_ref, g_ref, mix_ref, wr_ref, wk_ref, wv_ref, w1_ref, w2_ref,
                      a1_ref, a2_ref, g1_ref, g2_ref, w0_ref, a0_ref, kk_ref, ka_ref, rk_ref,
                      hsum_ref,
                      r_out, k_out, v_out, kk_out, bb_out, lw_out, g_out, bonus_out,
                      *, tiles_per_seq):
    i = pl.program_id(0)
    g = g_ref[...]
    hn = _rms(h_ref[...], g)
    prev = _rms(hp_ref[7:8, :], g)
    prev = jnp.where(i % tiles_per_seq == 0, 0.0, prev)
    row = lax.broadcasted_iota(jnp.int32, hn.shape, 0)
    shifted = jnp.where(row == 0, prev, pltpu.roll(hn, 1, axis=0))
    dx = shifted - hn
    xr, xw, xk, xv, xa, xg = (hn + dx * mix_ref[m:m + 1, :] for m in range(6))
    r = _dot(xr, wr_ref[...])
    k = _dot(xk, wk_ref[...])
    v = _dot(xv, wv_ref[...])
    wl = w0_ref[...] + _dot(jnp.tanh(_dot(xw, w1_ref[...])), w2_ref[...])
    nwl = -wl
    w_log = -(jnp.maximum(nwl, 0.0) + jnp.log(1.0 + jnp.exp(-jnp.abs(nwl)))) - 0.5
    a = _sigmoid(a0_ref[...] + _dot(_dot(xa, a1_ref[...]), a2_ref[...]))
    gate = _dot(_sigmoid(_dot(xg, g1_ref[...])), g2_ref[...])
    hsum = hsum_ref[...]
    kk = k * kk_ref[...]
    kk = kk / jnp.maximum(jnp.sqrt(_dot_exact_rhs(kk * kk, hsum)), 1e-12)
    k2 = k * (1.0 + (a - 1.0) * ka_ref[...])
    r_out[...] = r
    k_out[...] = k2
    v_out[...] = v
    kk_out[...] = kk
    bb_out[...] = kk * a
    lw_out[...] = -jnp.exp(w_log)
    g_out[...] = gate
    bonus_out[...] = _dot_exact_rhs(r * k2 * rk_ref[...], hsum) * v


def _rwkv_proj(h, t, g, mix8, wr, wk, wv, w1, w2, a1, a2, g1, g2, w0, a0, k_k, k_a, r_k, hsum):
    n, d = h.shape
    tm = PROJ_TM
    full = lambda arr: pl.BlockSpec(arr.shape, lambda i: (0,) * arr.ndim)
    tile = pl.BlockSpec((tm, d), lambda i: (i, 0))
    prev = pl.BlockSpec((8, d), lambda i: (jnp.maximum(i * (tm // 8) - 1, 0), 0))
    consts = [g, mix8, wr, wk, wv, w1, w2, a1, a2, g1, g2, w0, a0, k_k, k_a, r_k, hsum]
    return pl.pallas_call(
        functools.partial(_rwkv_proj_kernel, tiles_per_seq=t // tm),
        grid=(n // tm,),
        in_specs=[tile, prev] + [full(c) for c in consts],
        out_specs=[tile] * 8,
        out_shape=[jax.ShapeDtypeStruct((n, d), F32)] * 8,
        compiler_params=_cparams(("arbitrary",)),
        name="rwkv_proj",
    )(h, h, *consts)


def _rwkv_rec_kernel(r_ref, k_ref, v_ref, kk_ref, bb_ref, lw_ref, bonus_ref, gate_ref,
                     lng_ref, lnb_ref, o_ref, st_ref):
    c, hd, w = REC_C, RWKV_HEAD, REC_W
    bsz, _, d = r_ref.shape
    units = [(b, u) for b in range(bsz) for u in range(d // w)]

    @pl.when(pl.program_id(0) == 0)
    def _():
        st_ref[...] = jnp.zeros_like(st_ref)

    rr, cc = _iota2((c, c))
    cum_m = jnp.where(cc <= rr, 1.0, 0.0).astype(BF16)
    lane_c = lax.broadcasted_iota(jnp.int32, (c, w), 1)
    row_c = lax.broadcasted_iota(jnp.int32, (c, w), 0)
    head_c = lane_c >> (hd.bit_length() - 1)
    colm = lane_c & (hd - 1)
    strict = colm < row_c
    incl = colm <= row_c
    eye_cat = jnp.where(colm == row_c, 1.0, 0.0)
    r2, c2 = _iota2((w, w))
    bd = _same_block(r2, c2, hd)
    diag = r2 == c2
    hmean = jnp.where(bd, 1.0 / hd, 0.0).astype(BF16)

    def stack(xm):
        return jnp.concatenate([jnp.where(head_c == h, xm, 0.0) for h in range(w // hd)], axis=0)

    prep = []
    for b in range(bsz):
        lw = lw_ref[b]
        lp = _dot_exact_lhs(cum_m, lw)
        pend = lp[c - 1:c, :]
        e_neg = jnp.exp(-lp)
        e_end = jnp.exp(pend - lp)
        kb, bbv = k_ref[b], bb_ref[b]
        prep.append(dict(rt=r_ref[b] * jnp.exp(lp), kt=kk_ref[b] * jnp.exp(lp - lw),
                         kh=kb * e_neg, bh=bbv * e_neg, khp=kb * e_end, bhp=bbv * e_end,
                         p_end=jnp.exp(pend), v=v_ref[b]))

    def usl(name, b, u):
        return prep[b][name][:, u * w:(u + 1) * w]

    m_kk, m_kb, a_rk, a_rb = [], [], [], []
    for (b, u) in units:
        lhs = jnp.concatenate([usl('kt', b, u), usl('rt', b, u)], axis=0)
        rhs = jnp.concatenate([stack(usl('kh', b, u)), stack(usl('bh', b, u))], axis=0)
        gram = _dot_nt(lhs, rhs)
        m_kk.append(jnp.where(strict, gram[0:c, 0:w], 0.0))
        m_kb.append(jnp.where(strict, gram[0:c, w:2 * w], 0.0))
        a_rk.append(jnp.where(incl, gram[c:2 * c, 0:w], 0.0))
        a_rb.append(jnp.where(incl, gram[c:2 * c, w:2 * w], 0.0))
    tinv = [eye_cat - m for m in m_kb]
    pw = [_dot(m, stack(m)) for m in m_kb]
    mv = [_dot(jnp.concatenate([m_kk[i], a_rk[i]], axis=0), stack(usl('v', b, u)))
          for i, (b, u) in enumerate(units)]
    n_lvl = (c - 1).bit_length() - 1
    for lvl in range(n_lvl):
        last = lvl == n_lvl - 1
        for i in range(len(units)):
            spw = stack(pw[i])
            if last:
                tinv[i] = tinv[i] + _dot(tinv[i], spw)
            else:
                res = _dot(jnp.concatenate([tinv[i], pw[i]], axis=0), spw)
                tinv[i] = tinv[i] + res[0:c]
                pw[i] = res[c:2 * c]
    kv = [_dot(tinv[i], jnp.concatenate([stack(usl('kt', b, u)), stack(mv[i][0:c])], axis=1))
          for i, (b, u) in enumerate(units)]
    rp, y0, phi, psi = [], [], [], []
    for i, (b, u) in enumerate(units):
        kp, vp = kv[i][:, 0:w], kv[i][:, w:2 * w]
        ab = _dot(a_rb[i], jnp.concatenate([stack(kp), stack(vp)], axis=1))
        rp.append(usl('rt', b, u) - ab[:, 0:w])
        y0.append(mv[i][c:2 * c] - ab[:, w:2 * w])
        vv = usl('v', b, u)
        xs = jnp.concatenate([usl('khp', b, u), -usl('bhp', b, u)], axis=0)
        rhs2 = jnp.concatenate([jnp.concatenate([vv, jnp.zeros_like(vv)], axis=1),
                                jnp.concatenate([vp, kp], axis=1)], axis=0)
        pp = _dot_tn(xs, rhs2)
        psi.append(jnp.where(bd, pp[:, 0:w], 0.0))
        phi.append(jnp.where(bd, pp[:, w:2 * w], 0.0) + jnp.where(diag, usl('p_end', b, u), 0.0))
    ys = []
    for i, (b, u) in enumerate(units):
        a_bd = st_ref[b, u]
        ys.append(_dot(rp[i], a_bd) + y0[i])
        st_ref[b, u] = _dot(phi[i], a_bd) + psi[i]
    for i, (b, u) in enumerate(units):
        ls = slice(u * w, (u + 1) * w)
        mu = _dot(ys[i], hmean)
        dlt = ys[i] - mu
        var = _dot(dlt * dlt, hmean)
        yn = dlt * lax.rsqrt(var + GN_EPS) * lng_ref[:, ls] + lnb_ref[:, ls]
        o_ref[b, :, ls] = ((yn + bonus_ref[b, :, ls]) * gate_ref[b, :, ls]).astype(o_ref.dtype)


def _rwkv_rec(r, k, v, kk, bb, lw, bonus, gate, ln_g, ln_b):
    bsz, t, d = r.shape
    blk = pl.BlockSpec((bsz, REC_C, d), lambda i: (0, i, 0))
    vec = pl.BlockSpec((1, d), lambda i: (0, 0))
    return pl.pallas_call(
        _rwkv_rec_kernel,
        grid=(t // REC_C,),
        in_specs=[blk] * 8 + [vec, vec],
        out_specs=blk,
        out_shape=jax.ShapeDtypeStruct((bsz, t, d), BF16),
        scratch_shapes=[pltpu.VMEM((bsz, d // REC_W, REC_W, REC_W), F32)],
        compiler_params=_cparams(("arbitrary",)),
        name="rwkv_rec",
    )(r, k, v, kk, bb, lw, bonus, gate, ln_g, ln_b)


def _proj_resid_kernel(a_ref, w_ref, res_ref, o_ref):
    o_ref[...] = res_ref[...] + jnp.dot(a_ref[...], w_ref[...], preferred_element_type=F32)


def _proj_resid(a, w, res):
    n, d = res.shape
    tm = OUT_TM
    return pl.pallas_call(
        _proj_resid_kernel,
        grid=(n // tm,),
        in_specs=[pl.BlockSpec((tm, a.shape[1]), lambda i: (i, 0)),
                  pl.BlockSpec(w.shape, lambda i: (0, 0)),
                  pl.BlockSpec((tm, d), lambda i: (i, 0))],
        out_specs=pl.BlockSpec((tm, d), lambda i: (i, 0)),
        out_shape=jax.ShapeDtypeStruct((n, d), F32),
        compiler_params=_cparams(("arbitrary",)),
        name="proj_resid",
    )(a, w, res)


def kernel(x, norm_mix_g, norm_ffn_g, norm_out_g, mix_w_in, mix_w_out, gmlp_norm_g, gmlp_w_s, gmlp_b_s, hgrn_lb_logits, hgrn_onorm_g, ffn_w_gate, ffn_w_up, ffn_w_down, rwkv_mix, rwkv_w_r, rwkv_w_k, rwkv_w_v, rwkv_w_o, rwkv_w0, rwkv_w1, rwkv_w2, rwkv_a0, rwkv_a1, rwkv_a2, rwkv_g1, rwkv_g2, rwkv_k_k, rwkv_k_a, rwkv_r_k, rwkv_ln_g, rwkv_ln_b, moe_router, moe_router_b, moe_w_gate, moe_w_up, moe_w_down):
    bsz, t, d = x.shape
    n = bsz * t
    row = lambda vec: vec.reshape(1, -1).astype(F32)
    bf = lambda w: w.astype(BF16)

    lower_bounds = jnp.cumsum(jax.nn.softmax(hgrn_lb_logits.astype(F32), axis=0), axis=0)
    bias_b = jnp.repeat(gmlp_b_s[0].astype(F32).T, GMLP_DIM, axis=1)
    h = _mixer0(x, row(norm_mix_g[0]), bf(mix_w_in[0]), bf(mix_w_out[0]), row(gmlp_norm_g[0]),
                gmlp_w_s[0].astype(F32), bias_b, row(lower_bounds[0]), row(hgrn_onorm_g[0]))
    h = h.reshape(n, d)
    n_half = ffn_w_gate.shape[2] // D_FF_EXPERT
    halves = lambda w: bf(w).reshape(d, n_half, D_FF_EXPERT).transpose(1, 0, 2)
    h = _ffn(h, row(norm_ffn_g[0]), halves(ffn_w_gate[0]), halves(ffn_w_up[0]),
             bf(ffn_w_down[0]).reshape(n_half, D_FF_EXPERT, d))

    mix8 = jnp.concatenate([rwkv_mix[0].astype(F32), jnp.zeros((2, d), F32)], axis=0)
    head_id = jnp.arange(d) // RWKV_HEAD
    hsum = (head_id[:, None] == head_id[None, :]).astype(BF16)
    r, k2, v, kk, bb, lw, gate, bonus = _rwkv_proj(
        h, t, row(norm_mix_g[1]), mix8, bf(rwkv_w_r[0]), bf(rwkv_w_k[0]), bf(rwkv_w_v[0]),
        bf(rwkv_w1[0]), bf(rwkv_w2[0]), bf(rwkv_a1[0]), bf(rwkv_a2[0]), bf(rwkv_g1[0]), bf(rwkv_g2[0]),
        row(rwkv_w0[0]), row(rwkv_a0[0]), row(rwkv_k_k[0]), row(rwkv_k_a[0]), row(rwkv_r_k[0]), hsum)
    sh = lambda z: z.reshape(bsz, t, d)
    yg = _rwkv_rec(sh(r), sh(k2), sh(v), sh(kk), sh(bb), sh(lw), sh(bonus), sh(gate),
                   row(rwkv_ln_g[0]), row(rwkv_ln_b[0]))
    h = _proj_resid(yg.reshape(n, d), bf(rwkv_w_o[0]), h)
    router = jnp.zeros((d, LANES), F32).at[:, :N_EXPERTS].set(moe_router[0].astype(F32))
    router_b = jnp.full((1, LANES), -1e30, F32).at[0, :N_EXPERTS].set(moe_router_b[0].astype(F32))
    tok = jnp.arange(MOE_TT)
    tri = (tok[None, :] < tok[:, None]).astype(BF16)
    hn, gates, rank, counts = _moe_router(h, row(norm_ffn_g[1]), router, router_b, tri)
    n_chunks = ((counts[:, 0, :N_EXPERTS].astype(jnp.int32) + (MOE_CH - 1)) // MOE_CH)
    out = _moe_experts(n_chunks, h, hn, gates, rank, bf(moe_w_gate[0]), bf(moe_w_up[0]), bf(moe_w_down[0]),
                       row(norm_out_g))
    return out.reshape(bsz, t, d)
```

```python
import functools

import jax
import jax.numpy as jnp
from jax import lax
from jax.experimental import pallas as pl
from jax.experimental.pallas import tpu as pltpu

F32 = jnp.float32
BF16 = jnp.bfloat16

D_MODEL = 1024
GMLP_GROUPS = 4
GMLP_DIM = 128
GMLP_CHUNK = 128
HGRN_HEADS = 4
HGRN_DK = 128
HGRN_CHUNK = 32
MIX_HALF = 512
MIX_IN = 3072
RWKV_HEAD = 64
N_EXPERTS = 8
D_FF_EXPERT = 1408
RMS_EPS = 1e-6
LN_EPS = 1e-5
GN_EPS = 64e-5

LANES = 128
VMEM_LIMIT_BYTES = 56 * 1024 * 1024

MIX_TM = 256
MLP_TM = 512
MOE_TT = 1024
MOE_CH = 128
PROJ_TM = 256
REC_C = 64
REC_HPU = 4
REC_W = REC_HPU * RWKV_HEAD


def _cparams(sem):
    return pltpu.CompilerParams(dimension_semantics=sem, vmem_limit_bytes=VMEM_LIMIT_BYTES)


def _dot(a, b):
    return jnp.dot(a.astype(BF16), b.astype(BF16), preferred_element_type=F32)


def _dot_nt(a, b):
    return lax.dot_general(a.astype(BF16), b.astype(BF16), (((1,), (1,)), ((), ())),
                           preferred_element_type=F32)


def _dot_tn(a, b):
    return lax.dot_general(a.astype(BF16), b.astype(BF16), (((0,), (0,)), ((), ())),
                           preferred_element_type=F32)


def _split(a):
    hi = a.astype(BF16)
    lo = (a - hi.astype(F32)).astype(BF16)
    return hi, lo


def _dot_exact_lhs(m_bf16, a):
    hi, lo = _split(a)
    return (jnp.dot(m_bf16, hi, preferred_element_type=F32)
            + jnp.dot(m_bf16, lo, preferred_element_type=F32))


def _dot3(a, b):
    a_hi, a_lo = _split(a)
    b_hi, b_lo = _split(b)
    return (jnp.dot(a_hi, b_hi, preferred_element_type=F32)
            + jnp.dot(a_hi, b_lo, preferred_element_type=F32)
            + jnp.dot(a_lo, b_hi, preferred_element_type=F32))


def _rms(x, g):
    return x * lax.rsqrt(jnp.mean(x * x, axis=-1, keepdims=True) + RMS_EPS) * g


def _sigmoid(x):
    return jax.nn.sigmoid(x)


def _silu(x):
    return x * jax.nn.sigmoid(x)


def _iota2(shape):
    return (lax.broadcasted_iota(jnp.int32, shape, 0), lax.broadcasted_iota(jnp.int32, shape, 1))


def _same_block(a, b, size):
    shift = size.bit_length() - 1
    assert 1 << shift == size
    return (a >> shift) == (b >> shift)


def _mixer0_kernel(x_ref, g_ref, win_ref, wout_ref, gng_ref, ws_ref, bias_ref, lb_ref, og_ref,
                   o_ref, st_ref):
    tm = MIX_TM

    @pl.when(pl.program_id(1) == 0)
    def _():
        st_ref[...] = jnp.zeros_like(st_ref)

    x = x_ref[0]
    z = _dot(_rms(x, g_ref[...]), win_ref[...])

    u = jax.nn.gelu(z[:, 0:MIX_HALF])
    v = jax.nn.gelu(z[:, MIX_HALF:2 * MIX_HALF])
    row, col = _iota2((GMLP_CHUNK, GMLP_CHUNK))
    tril = col <= row
    mixed_groups = []
    for g in range(GMLP_GROUPS):
        gs = slice(g * GMLP_DIM, (g + 1) * GMLP_DIM)
        vg = v[:, gs]
        mu = jnp.mean(vg, axis=-1, keepdims=True)
        d = vg - mu
        var = jnp.mean(d * d, axis=-1, keepdims=True)
        vn = (d * lax.rsqrt(var + LN_EPS) * gng_ref[:, gs]).astype(BF16)
        wg = jnp.where(tril, ws_ref[g], 0.0).astype(BF16)
        parts = [jnp.dot(wg, vn[c * GMLP_CHUNK:(c + 1) * GMLP_CHUNK], preferred_element_type=F32)
                 for c in range(tm // GMLP_CHUNK)]
        mixed_groups.append(jnp.concatenate(parts, axis=0))
    bias = jnp.concatenate([bias_ref[...]] * (tm // GMLP_CHUNK), axis=0)
    y_a = u * (jnp.concatenate(mixed_groups, axis=1) + bias)

    o0 = 2 * MIX_HALF
    zq = z[:, o0:o0 + MIX_HALF]
    zf = z[:, o0 + MIX_HALF:o0 + 2 * MIX_HALF]
    zi = z[:, o0 + 2 * MIX_HALF:o0 + 3 * MIX_HALF]
    zg = z[:, o0 + 3 * MIX_HALF:o0 + 4 * MIX_HALF]
    lb = lb_ref[...]
    q = _silu(zq)
    f = lb + (1.0 - lb) * _sigmoid(zf)
    k = 1.0 - f
    lf = jnp.log(f)
    rr, cc = _iota2((tm, tm))
    same = _same_block(rr, cc, HGRN_CHUNK)
    cum_m = jnp.where(same & (cc <= rr), 1.0, 0.0).astype(BF16)
    end_m = jnp.where(same, 1.0, 0.0).astype(BF16)
    lf_hi, lf_lo = _split(lf)
    b = (jnp.dot(cum_m, lf_hi, preferred_element_type=F32)
         + jnp.dot(cum_m, lf_lo, preferred_element_type=F32))
    b_end = (jnp.dot(end_m, lf_hi, preferred_element_type=F32)
             + jnp.dot(end_m, lf_lo, preferred_element_type=F32))
    q_dec = (q * jnp.exp(b)).astype(BF16)
    k_inv = (k * jnp.exp(-b)).astype(BF16)
    k_end = (k * jnp.exp(b_end - b)).astype(BF16)
    dec = jnp.exp(b_end)
    vb = zi.astype(BF16)

    r128, c128 = _iota2((128, 128))
    intra_mask = _same_block(r128, c128, HGRN_CHUNK) & (c128 <= r128)
    og = og_ref[...]
    o_heads = []
    for h in range(HGRN_HEADS):
        hs = slice(h * HGRN_DK, (h + 1) * HGRN_DK)
        intra = []
        for rb in range(tm // 128):
            rs = slice(rb * 128, (rb + 1) * 128)
            sc = _dot_nt(q_dec[rs, hs], k_inv[rs, hs])
            sc = jnp.where(intra_mask, sc, 0.0)
            intra.append(_dot(sc, vb[rs, hs]))
        o_intra = jnp.concatenate(intra, axis=0)
        st = st_ref[h]
        inter = []
        for c in range(tm // HGRN_CHUNK):
            cs = slice(c * HGRN_CHUNK, (c + 1) * HGRN_CHUNK)
            inter.append(_dot_nt(q_dec[cs, hs], st))
            inc_t = _dot_tn(vb[cs, hs], k_end[cs, hs])
            st = st * dec[c * HGRN_CHUNK:c * HGRN_CHUNK + 1, hs] + inc_t
        st_ref[h] = st
        o_h = o_intra + jnp.concatenate(inter, axis=0)
        o_h = o_h * lax.rsqrt(jnp.mean(o_h * o_h, axis=-1, keepdims=True) + RMS_EPS) * og
        o_heads.append(o_h * _silu(zg[:, hs]))

    y = jnp.concatenate([y_a] + o_heads, axis=1)
    o_ref[0] = x + _dot(y, wout_ref[...])


def _mixer0(x, g, w_in, w_out, gn_g, w_s, bias_b, lb, og):
    bsz, t, d = x.shape
    grid = (bsz, t // MIX_TM)
    const = lambda shape: pl.BlockSpec(shape, lambda b, i: (0,) * len(shape))
    return pl.pallas_call(
        _mixer0_kernel,
        grid=grid,
        in_specs=[
            pl.BlockSpec((1, MIX_TM, d), lambda b, i: (b, i, 0)),
            const((1, d)),
            const((d, MIX_IN)),
            const((2 * MIX_HALF, d)),
            const((1, MIX_HALF)),
            const((GMLP_GROUPS, GMLP_CHUNK, GMLP_CHUNK)),
            const((GMLP_CHUNK, MIX_HALF)),
            const((1, MIX_HALF)),
            const((1, HGRN_DK)),
        ],
        out_specs=pl.BlockSpec((1, MIX_TM, d), lambda b, i: (b, i, 0)),
        out_shape=jax.ShapeDtypeStruct((bsz, t, d), F32),
        scratch_shapes=[pltpu.VMEM((HGRN_HEADS, HGRN_DK, HGRN_DK), F32)],
        compiler_params=_cparams(("arbitrary", "arbitrary")),
        name="mixer0",
    )(x, g, w_in, w_out, gn_g, w_s, bias_b, lb, og)


def _ffn_kernel(x_ref, g_ref, wg_ref, wu_ref, wd_ref, o_ref, hn_ref, acc_ref, *, n_j):
    j = pl.program_id(1)

    @pl.when(j == 0)
    def _():
        hn_ref[...] = _rms(x_ref[...], g_ref[...]).astype(BF16)
        acc_ref[...] = jnp.zeros_like(acc_ref)

    hn = hn_ref[...]
    a = jnp.dot(hn, wg_ref[0], preferred_element_type=F32)
    b = jnp.dot(hn, wu_ref[0], preferred_element_type=F32)
    acc_ref[...] += _dot(_silu(a) * b, wd_ref[0])

    @pl.when(j == n_j - 1)
    def _():
        o_ref[...] = x_ref[...] + acc_ref[...]


def _ffn(x, g, wg, wu, wd):
    n, d = x.shape
    n_j, _, dff = wg.shape
    tm = MLP_TM
    return pl.pallas_call(
        functools.partial(_ffn_kernel, n_j=n_j),
        grid=(n // tm, n_j),
        in_specs=[pl.BlockSpec((tm, d), lambda i, j: (i, 0)),
                  pl.BlockSpec((1, d), lambda i, j: (0, 0)),
                  pl.BlockSpec((1, d, dff), lambda i, j: (j, 0, 0)),
                  pl.BlockSpec((1, d, dff), lambda i, j: (j, 0, 0)),
                  pl.BlockSpec((1, dff, d), lambda i, j: (j, 0, 0))],
        out_specs=pl.BlockSpec((tm, d), lambda i, j: (i, 0)),
        out_shape=jax.ShapeDtypeStruct((n, d), F32),
        scratch_shapes=[pltpu.VMEM((tm, d), BF16), pltpu.VMEM((tm, d), F32)],
        compiler_params=_cparams(("arbitrary", "arbitrary")),
        name="ffn_mlp",
    )(x, g, wg, wu, wd)


def _moe_router_kernel(x_ref, y_ref, wo_ref, g_ref, router_ref, rb_ref, tri_ref,
                       h_ref, hn_ref, gates_ref, rank_ref, cnt_ref):
    h = x_ref[...] + jnp.dot(y_ref[...], wo_ref[...], preferred_element_type=F32)
    h_ref[...] = h
    hn = _rms(h, g_ref[...])
    hn_ref[...] = hn.astype(BF16)
    logits = _dot3(hn, router_ref[...]) + rb_ref[...]
    lane = lax.broadcasted_iota(jnp.int32, logits.shape, 1)
    m1 = jnp.max(logits, axis=-1, keepdims=True)
    i1 = jnp.min(jnp.where(logits == m1, lane, LANES), axis=-1, keepdims=True)
    rest = jnp.where(lane == i1, -jnp.inf, logits)
    m2 = jnp.max(rest, axis=-1, keepdims=True)
    i2 = jnp.min(jnp.where(rest == m2, lane, LANES), axis=-1, keepdims=True)
    e2 = jnp.exp(m2 - m1)
    den = 1.0 + e2
    gates_ref[...] = jnp.where(lane == i1, 1.0 / den, 0.0) + jnp.where(lane == i2, e2 / den, 0.0)
    sel = (lane == i1) | (lane == i2)
    sel_f = jnp.where(sel, 1.0, 0.0)
    rank = jnp.dot(tri_ref[...], sel_f.astype(BF16), preferred_element_type=F32)
    rank_ref[...] = jnp.where(sel, rank, -1.0)
    cnt_ref[0] = jnp.broadcast_to(jnp.sum(sel_f, axis=0, keepdims=True), cnt_ref.shape[1:])


def _moe_router(x, y, w_o, g, router, router_b, tri):
    n, d = x.shape
    tm = MOE_TT
    tile = lambda w: pl.BlockSpec((tm, w), lambda i: (i, 0))
    full = lambda arr: pl.BlockSpec(arr.shape, lambda i: (0,) * arr.ndim)
    return pl.pallas_call(
        _moe_router_kernel,
        grid=(n // tm,),
        in_specs=[tile(d), tile(d), full(w_o), full(g), full(router), full(router_b), full(tri)],
        out_specs=[tile(d), tile(d), tile(LANES), tile(LANES), pl.BlockSpec((1, 8, LANES), lambda i: (i, 0, 0))],
        out_shape=[jax.ShapeDtypeStruct((n, d), F32), jax.ShapeDtypeStruct((n, d), BF16),
                   jax.ShapeDtypeStruct((n, LANES), F32), jax.ShapeDtypeStruct((n, LANES), F32),
                   jax.ShapeDtypeStruct((n // tm, 8, LANES), F32)],
        compiler_params=_cparams(("arbitrary",)),
        name="moe_router",
    )(x, y, w_o, g, router, router_b, tri)


def _moe_expert_kernel(nch_ref, x_ref, hn_ref, gates_ref, rank_ref, wg_ref, wu_ref, wd_ref, gout_ref,
                       o_ref, acc_ref, *, n_e):
    i, e = pl.program_id(0), pl.program_id(1)
    tt, ch = MOE_TT, MOE_CH

    @pl.when(e == 0)
    def _():
        acc_ref[...] = jnp.zeros_like(acc_ref)

    lane = lax.broadcasted_iota(jnp.int32, (tt, LANES), 1)
    rank_col = jnp.sum(jnp.where(lane == e, rank_ref[...], 0.0), axis=-1, keepdims=True)
    gates = jnp.where(lane == e, gates_ref[...], 0.0)
    g_hi, g_lo = _split(gates)
    slot = lax.broadcasted_iota(jnp.int32, (tt, ch), 1).astype(F32)
    lane_ch = lax.broadcasted_iota(jnp.int32, (ch, LANES), 1)
    tn = (((0,), (0,)), ((), ()))

    def chunk(k, carry):
        pt = jnp.where(rank_col - (k * ch).astype(F32) == slot, 1.0, 0.0).astype(BF16)
        xs = lax.dot_general(pt, hn_ref[...], tn, preferred_element_type=F32).astype(BF16)
        a = jnp.dot(xs, wg_ref[0], preferred_element_type=F32)
        b = jnp.dot(xs, wu_ref[0], preferred_element_type=F32)
        y = _dot(_silu(a) * b, wd_ref[0])
        gch = (lax.dot_general(pt, g_hi, tn, preferred_element_type=F32)
               + lax.dot_general(pt, g_lo, tn, preferred_element_type=F32))
        gcol = jnp.sum(jnp.where(lane_ch == e, gch, 0.0), axis=-1, keepdims=True)
        acc_ref[...] += jnp.dot(pt, (y * gcol).astype(BF16), preferred_element_type=F32)
        return carry

    lax.fori_loop(0, nch_ref[i, e], chunk, 0)

    @pl.when(e == n_e - 1)
    def _():
        o_ref[...] = _rms(x_ref[...] + acc_ref[...], gout_ref[...])


def _moe_experts(nch, x, hn, gates, rank, wg, wu, wd, g_out):
    n, d = x.shape
    n_e, _, dff = wg.shape
    tt = MOE_TT
    tile = lambda w: pl.BlockSpec((tt, w), lambda i, e, nch: (i, 0))
    grid_spec = pltpu.PrefetchScalarGridSpec(
        num_scalar_prefetch=1,
        grid=(n // tt, n_e),
        in_specs=[tile(d), tile(d), tile(LANES), tile(LANES),
                  pl.BlockSpec((1, d, dff), lambda i, e, nch: (e, 0, 0)),
                  pl.BlockSpec((1, d, dff), lambda i, e, nch: (e, 0, 0)),
                  pl.BlockSpec((1, dff, d), lambda i, e, nch: (e, 0, 0)),
                  pl.BlockSpec((1, d), lambda i, e, nch: (0, 0))],
        out_specs=tile(d),
        scratch_shapes=[pltpu.VMEM((tt, d), F32)],
    )
    return pl.pallas_call(
        functools.partial(_moe_expert_kernel, n_e=n_e),
        grid_spec=grid_spec,
        out_shape=jax.ShapeDtypeStruct((n, d), F32),
        compiler_params=_cparams(("arbitrary", "arbitrary")),
        name="moe_experts",
    )(nch, x, hn, gates, rank, wg, wu, wd, g_out)


def _rwkv_proj_kernel(h_ref, hp_ref, g_ref, mix_ref, wr_ref, wk_ref, wv_ref, w1_ref, w2_ref,
                      a1_ref, a2_ref, g1_ref, g2_ref, w0_ref, a0_ref, kk_ref, ka_ref, rk_ref,
                      hsum_ref,
                      r_out, k_out, v_out, kk_out, bb_out, lw_out, g_out, bonus_out,
                      *, tiles_per_seq):
    i = pl.program_id(0)
    g = g_ref[...]
    hn = _rms(h_ref[...], g)
    prev = _rms(hp_ref[7:8, :], g)
    prev = jnp.where(i % tiles_per_seq == 0, 0.0, prev)
    row = lax.broadcasted_iota(jnp.int32, hn.shape, 0)
    shifted = jnp.where(row == 0, prev, pltpu.roll(hn, 1, axis=0))
    dx = shifted - hn
    xr, xw, xk, xv, xa, xg = (hn + dx * mix_ref[m:m + 1, :] for m in range(6))
    r = _dot(xr, wr_ref[...])
    k = _dot(xk, wk_ref[...])
    v = _dot(xv, wv_ref[...])
    wl = w0_ref[...] + _dot(jnp.tanh(_dot(xw, w1_ref[...])), w2_ref[...])
    nwl = -wl
    w_log = -(jnp.maximum(nwl, 0.0) + jnp.log(1.0 + jnp.exp(-jnp.abs(nwl)))) - 0.5
    a = _sigmoid(a0_ref[...] + _dot(_dot(xa, a1_ref[...]), a2_ref[...]))
    gate = _dot(_sigmoid(_dot(xg, g1_ref[...])), g2_ref[...])
    hsum = hsum_ref[...]

    def head_sum(z):
        w = hsum.shape[0]
        return jnp.concatenate([_dot(z[:, s:s + w], hsum) for s in range(0, z.shape[1], w)], axis=1)

    kk = k * kk_ref[...]
    kk = kk / jnp.maximum(jnp.sqrt(head_sum(kk * kk)), 1e-12)
    k2 = k * (1.0 + (a - 1.0) * ka_ref[...])
    r_out[...] = r.astype(r_out.dtype)
    k_out[...] = k2.astype(k_out.dtype)
    v_out[...] = v.astype(v_out.dtype)
    kk_out[...] = kk.astype(kk_out.dtype)
    bb_out[...] = (kk * a).astype(bb_out.dtype)
    lw_out[...] = -jnp.exp(w_log)
    g_out[...] = gate.astype(g_out.dtype)
    bonus_out[...] = (head_sum(r * k2 * rk_ref[...]) * v).astype(bonus_out.dtype)


def _rwkv_proj(h, t, g, mix8, wr, wk, wv, w1, w2, a1, a2, g1, g2, w0, a0, k_k, k_a, r_k, hsum):
    n, d = h.shape
    tm = PROJ_TM
    full = lambda arr: pl.BlockSpec(arr.shape, lambda i: (0,) * arr.ndim)
    tile = pl.BlockSpec((tm, d), lambda i: (i, 0))
    prev = pl.BlockSpec((8, d), lambda i: (jnp.maximum(i * (tm // 8) - 1, 0), 0))
    consts = [g, mix8, wr, wk, wv, w1, w2, a1, a2, g1, g2, w0, a0, k_k, k_a, r_k, hsum]
    return pl.pallas_call(
        functools.partial(_rwkv_proj_kernel, tiles_per_seq=t // tm),
        grid=(n // tm,),
        in_specs=[tile, prev] + [full(c) for c in consts],
        out_specs=[tile] * 8,
        out_shape=[jax.ShapeDtypeStruct((n, d), F32 if idx == 5 else BF16) for idx in range(8)],
        compiler_params=_cparams(("arbitrary",)),
        name="rwkv_proj",
    )(h, h, *consts)


def _rwkv_rec_kernel(r_ref, k_ref, v_ref, kk_ref, bb_ref, lw_ref, bonus_ref, gate_ref,
                     lng_ref, lnb_ref, o_ref, st_ref):
    c, hd, w = REC_C, RWKV_HEAD, REC_W
    bsz, _, d = r_ref.shape
    units = [(b, u) for b in range(bsz) for u in range(d // w)]

    @pl.when(pl.program_id(0) == 0)
    def _():
        st_ref[...] = jnp.zeros_like(st_ref)

    rr, cc = _iota2((c, c))
    cum_m = jnp.where(cc <= rr, 1.0, 0.0).astype(BF16)
    lane_c = lax.broadcasted_iota(jnp.int32, (c, w), 1)
    row_c = lax.broadcasted_iota(jnp.int32, (c, w), 0)
    head_c = lane_c >> (hd.bit_length() - 1)
    colm = lane_c & (hd - 1)
    strict = colm < row_c
    incl = colm <= row_c
    eye_cat = jnp.where(colm == row_c, 1.0, 0.0)
    r2, c2 = _iota2((w, w))
    bd = _same_block(r2, c2, hd)
    diag = r2 == c2
    hmean = jnp.where(bd, 1.0 / hd, 0.0).astype(BF16)

    def stack(xm):
        return jnp.concatenate([jnp.where(head_c == h, xm, 0.0) for h in range(w // hd)], axis=0)

    prep = []
    for b in range(bsz):
        lw = lw_ref[b]
        lp = _dot_exact_lhs(cum_m, lw)
        pend = lp[c - 1:c, :]
        e_neg = jnp.exp(-lp)
        e_end = jnp.exp(pend - lp)
        kb, bbv = k_ref[b].astype(F32), bb_ref[b].astype(F32)
        prep.append(dict(rt=r_ref[b].astype(F32) * jnp.exp(lp), kt=kk_ref[b].astype(F32) * jnp.exp(lp - lw),
                         kh=kb * e_neg, bh=bbv * e_neg, khp=kb * e_end, bhp=bbv * e_end,
                         p_end=jnp.exp(pend), v=v_ref[b]))

    def usl(name, b, u):
        return prep[b][name][:, u * w:(u + 1) * w]

    m_kk, m_kb, a_rk, a_rb = [], [], [], []
    for (b, u) in units:
        lhs = jnp.concatenate([usl('kt', b, u), usl('rt', b, u)], axis=0)
        rhs = jnp.concatenate([stack(usl('kh', b, u)), stack(usl('bh', b, u))], axis=0)
        gram = _dot_nt(lhs, rhs)
        m_kk.append(jnp.where(strict, gram[0:c, 0:w], 0.0))
        m_kb.append(jnp.where(strict, gram[0:c, w:2 * w], 0.0))
        a_rk.append(jnp.where(incl, gram[c:2 * c, 0:w], 0.0))
        a_rb.append(jnp.where(incl, gram[c:2 * c, w:2 * w], 0.0))
    tinv = [eye_cat - m for m in m_kb]
    pw = [_dot(m, stack(m)) for m in m_kb]
    mv = [_dot(jnp.concatenate([m_kk[i], a_rk[i]], axis=0), stack(usl('v', b, u)))
          for i, (b, u) in enumerate(units)]
    n_lvl = (c - 1).bit_length() - 1
    for lvl in range(n_lvl):
        last = lvl == n_lvl - 1
        for i in range(len(units)):
            spw = stack(pw[i])
            if last:
                tinv[i] = tinv[i] + _dot(tinv[i], spw)
            else:
                res = _dot(jnp.concatenate([tinv[i], pw[i]], axis=0), spw)
                tinv[i] = tinv[i] + res[0:c]
                pw[i] = res[c:2 * c]
    kv = [_dot(tinv[i], jnp.concatenate([stack(usl('kt', b, u)), stack(mv[i][0:c])], axis=1))
          for i, (b, u) in enumerate(units)]
    rp, y0, phi, psi = [], [], [], []
    for i, (b, u) in enumerate(units):
        kp, vp = kv[i][:, 0:w], kv[i][:, w:2 * w]
        ab = _dot(a_rb[i], jnp.concatenate([stack(kp), stack(vp)], axis=1))
        rp.append(usl('rt', b, u) - ab[:, 0:w])
        y0.append(mv[i][c:2 * c] - ab[:, w:2 * w])
        vv = usl('v', b, u)
        xs = jnp.concatenate([usl('khp', b, u), -usl('bhp', b, u)], axis=0)
        rhs2 = jnp.concatenate([jnp.concatenate([vv, jnp.zeros_like(vv)], axis=1),
                                jnp.concatenate([vp, kp], axis=1)], axis=0)
        pp = _dot_tn(xs, rhs2)
        psi.append(jnp.where(bd, pp[:, 0:w], 0.0))
        phi.append(jnp.where(bd, pp[:, w:2 * w], 0.0) + jnp.where(diag, usl('p_end', b, u), 0.0))
    ys = []
    for i, (b, u) in enumerate(units):
        a_bd = st_ref[b, u]
        ys.append(_dot(rp[i], a_bd) + y0[i])
        st_ref[b, u] = _dot(phi[i], a_bd) + psi[i]
    y_all = jnp.concatenate(ys, axis=0)
    dlt = y_all - _dot(y_all, hmean)
    yn_all = dlt * lax.rsqrt(_dot(dlt * dlt, hmean) + GN_EPS)
    for i, (b, u) in enumerate(units):
        ls = slice(u * w, (u + 1) * w)
        yn = yn_all[i * c:(i + 1) * c] * lng_ref[:, ls] + lnb_ref[:, ls]
        o_ref[b, :, ls] = ((yn + bonus_ref[b, :, ls].astype(F32))
                           * gate_ref[b, :, ls].astype(F32)).astype(o_ref.dtype)


def _rwkv_rec(r, k, v, kk, bb, lw, bonus, gate, ln_g, ln_b):
    bsz, t, d = r.shape
    blk = pl.BlockSpec((bsz, REC_C, d), lambda i: (0, i, 0))
    vec = pl.BlockSpec((1, d), lambda i: (0, 0))
    return pl.pallas_call(
        _rwkv_rec_kernel,
        grid=(t // REC_C,),
        in_specs=[blk] * 8 + [vec, vec],
        out_specs=blk,
        out_shape=jax.ShapeDtypeStruct((bsz, t, d), BF16),
        scratch_shapes=[pltpu.VMEM((bsz, d // REC_W, REC_W, REC_W), F32)],
        compiler_params=_cparams(("arbitrary",)),
        name="rwkv_rec",
    )(r, k, v, kk, bb, lw, bonus, gate, ln_g, ln_b)


def kernel(x, norm_mix_g, norm_ffn_g, norm_out_g, mix_w_in, mix_w_out, gmlp_norm_g, gmlp_w_s, gmlp_b_s, hgrn_lb_logits, hgrn_onorm_g, ffn_w_gate, ffn_w_up, ffn_w_down, rwkv_mix, rwkv_w_r, rwkv_w_k, rwkv_w_v, rwkv_w_o, rwkv_w0, rwkv_w1, rwkv_w2, rwkv_a0, rwkv_a1, rwkv_a2, rwkv_g1, rwkv_g2, rwkv_k_k, rwkv_k_a, rwkv_r_k, rwkv_ln_g, rwkv_ln_b, moe_router, moe_router_b, moe_w_gate, moe_w_up, moe_w_down):
    bsz, t, d = x.shape
    n = bsz * t
    row = lambda vec: vec.reshape(1, -1).astype(F32)
    bf = lambda w: w.astype(BF16)

    lower_bounds = jnp.cumsum(jax.nn.softmax(hgrn_lb_logits.astype(F32), axis=0), axis=0)
    bias_b = jnp.repeat(gmlp_b_s[0].astype(F32).T, GMLP_DIM, axis=1)
    h = _mixer0(x, row(norm_mix_g[0]), bf(mix_w_in[0]), bf(mix_w_out[0]), row(gmlp_norm_g[0]),
                gmlp_w_s[0].astype(F32), bias_b, row(lower_bounds[0]), row(hgrn_onorm_g[0]))
    h = h.reshape(n, d)
    n_half = ffn_w_gate.shape[2] // D_FF_EXPERT
    halves = lambda w: bf(w).reshape(d, n_half, D_FF_EXPERT).transpose(1, 0, 2)
    h = _ffn(h, row(norm_ffn_g[0]), halves(ffn_w_gate[0]), halves(ffn_w_up[0]),
             bf(ffn_w_down[0]).reshape(n_half, D_FF_EXPERT, d))

    mix8 = jnp.concatenate([rwkv_mix[0].astype(F32), jnp.zeros((2, d), F32)], axis=0)
    head_id = jnp.arange(REC_W) // RWKV_HEAD
    hsum = (head_id[:, None] == head_id[None, :]).astype(BF16)
    r, k2, v, kk, bb, lw, gate, bonus = _rwkv_proj(
        h, t, row(norm_mix_g[1]), mix8, bf(rwkv_w_r[0]), bf(rwkv_w_k[0]), bf(rwkv_w_v[0]),
        bf(rwkv_w1[0]), bf(rwkv_w2[0]), bf(rwkv_a1[0]), bf(rwkv_a2[0]), bf(rwkv_g1[0]), bf(rwkv_g2[0]),
        row(rwkv_w0[0]), row(rwkv_a0[0]), row(rwkv_k_k[0]), row(rwkv_k_a[0]), row(rwkv_r_k[0]), hsum)
    sh = lambda z: z.reshape(bsz, t, d)
    yg = _rwkv_rec(sh(r), sh(k2), sh(v), sh(kk), sh(bb), sh(lw), sh(bonus), sh(gate),
                   row(rwkv_ln_g[0]), row(rwkv_ln_b[0]))
    router = jnp.zeros((d, LANES), F32).at[:, :N_EXPERTS].set(moe_router[0].astype(F32))
    router_b = jnp.full((1, LANES), -1e30, F32).at[0, :N_EXPERTS].set(moe_router_b[0].astype(F32))
    tok = jnp.arange(MOE_TT)
    tri = (tok[None, :] < tok[:, None]).astype(BF16)
    h, hn, gates, rank, counts = _moe_router(h, yg.reshape(n, d), bf(rwkv_w_o[0]), row(norm_ffn_g[1]),
                                             router, router_b, tri)
    n_chunks = ((counts[:, 0, :N_EXPERTS].astype(jnp.int32) + (MOE_CH - 1)) // MOE_CH)
    out = _moe_experts(n_chunks, h, hn, gates, rank, bf(moe_w_gate[0]), bf(moe_w_up[0]), bf(moe_w_down[0]),
                       row(norm_out_g))
    return out.reshape(bsz, t, d)
```

```python
import functools

import jax
import jax.numpy as jnp
from jax import lax
from jax.experimental import pallas as pl
from jax.experimental.pallas import tpu as pltpu

F32 = jnp.float32
BF16 = jnp.bfloat16

D_MODEL = 1024
GMLP_GROUPS = 4
GMLP_DIM = 128
GMLP_CHUNK = 128
HGRN_HEADS = 4
HGRN_DK = 128
HGRN_CHUNK = 32
MIX_HALF = 512
MIX_IN = 3072
RWKV_HEAD = 64
N_EXPERTS = 8
D_FF_EXPERT = 1408
RMS_EPS = 1e-6
LN_EPS = 1e-5
GN_EPS = 64e-5

LANES = 128
VMEM_LIMIT_BYTES = 56 * 1024 * 1024

MIX_TM = 256
MLP_TM = 512
MOE_TT = 1024
MOE_CH = 144
MOE_CH_PAD = 256
PROJ_TM = 256
REC_C = 64
REC_HPU = 4
REC_W = REC_HPU * RWKV_HEAD


def _cparams(sem):
    return pltpu.CompilerParams(dimension_semantics=sem, vmem_limit_bytes=VMEM_LIMIT_BYTES)


def _dot(a, b):
    return jnp.dot(a.astype(BF16), b.astype(BF16), preferred_element_type=F32)


def _dot_nt(a, b):
    return lax.dot_general(a.astype(BF16), b.astype(BF16), (((1,), (1,)), ((), ())),
                           preferred_element_type=F32)


def _dot_tn(a, b):
    return lax.dot_general(a.astype(BF16), b.astype(BF16), (((0,), (0,)), ((), ())),
                           preferred_element_type=F32)


def _split(a):
    hi = a.astype(BF16)
    lo = (a - hi.astype(F32)).astype(BF16)
    return hi, lo


def _dot_exact_lhs(m_bf16, a):
    hi, lo = _split(a)
    return (jnp.dot(m_bf16, hi, preferred_element_type=F32)
            + jnp.dot(m_bf16, lo, preferred_element_type=F32))


def _dot3(a, b):
    a_hi, a_lo = _split(a)
    b_hi, b_lo = _split(b)
    return (jnp.dot(a_hi, b_hi, preferred_element_type=F32)
            + jnp.dot(a_hi, b_lo, preferred_element_type=F32)
            + jnp.dot(a_lo, b_hi, preferred_element_type=F32))


def _rms(x, g):
    return x * lax.rsqrt(jnp.mean(x * x, axis=-1, keepdims=True) + RMS_EPS) * g


def _sigmoid(x):
    return jax.nn.sigmoid(x)


def _silu(x):
    return x * jax.nn.sigmoid(x)


def _iota2(shape):
    return (lax.broadcasted_iota(jnp.int32, shape, 0), lax.broadcasted_iota(jnp.int32, shape, 1))


def _same_block(a, b, size):
    shift = size.bit_length() - 1
    assert 1 << shift == size
    return (a >> shift) == (b >> shift)


def _mixer0_kernel(x_ref, g_ref, win_ref, wout_ref, gng_ref, ws_ref, bias_ref, lb_ref, og_ref,
                   o_ref, st_ref):
    tm = MIX_TM

    @pl.when(pl.program_id(1) == 0)
    def _():
        st_ref[...] = jnp.zeros_like(st_ref)

    x = x_ref[0]
    z = _dot(_rms(x, g_ref[...]), win_ref[...])

    u = jax.nn.gelu(z[:, 0:MIX_HALF])
    v = jax.nn.gelu(z[:, MIX_HALF:2 * MIX_HALF])
    row, col = _iota2((GMLP_CHUNK, GMLP_CHUNK))
    tril = col <= row
    mixed_groups = []
    for g in range(GMLP_GROUPS):
        gs = slice(g * GMLP_DIM, (g + 1) * GMLP_DIM)
        vg = v[:, gs]
        mu = jnp.mean(vg, axis=-1, keepdims=True)
        d = vg - mu
        var = jnp.mean(d * d, axis=-1, keepdims=True)
        vn = (d * lax.rsqrt(var + LN_EPS) * gng_ref[:, gs]).astype(BF16)
        wg = jnp.where(tril, ws_ref[g], 0.0).astype(BF16)
        parts = [jnp.dot(wg, vn[c * GMLP_CHUNK:(c + 1) * GMLP_CHUNK], preferred_element_type=F32)
                 for c in range(tm // GMLP_CHUNK)]
        mixed_groups.append(jnp.concatenate(parts, axis=0))
    bias = jnp.concatenate([bias_ref[...]] * (tm // GMLP_CHUNK), axis=0)
    y_a = u * (jnp.concatenate(mixed_groups, axis=1) + bias)

    o0 = 2 * MIX_HALF
    zq = z[:, o0:o0 + MIX_HALF]
    zf = z[:, o0 + MIX_HALF:o0 + 2 * MIX_HALF]
    zi = z[:, o0 + 2 * MIX_HALF:o0 + 3 * MIX_HALF]
    zg = z[:, o0 + 3 * MIX_HALF:o0 + 4 * MIX_HALF]
    lb = lb_ref[...]
    q = _silu(zq)
    f = lb + (1.0 - lb) * _sigmoid(zf)
    k = 1.0 - f
    lf = jnp.log(f)
    rr, cc = _iota2((tm, tm))
    same = _same_block(rr, cc, HGRN_CHUNK)
    cum_m = jnp.where(same & (cc <= rr), 1.0, 0.0).astype(BF16)
    end_m = jnp.where(same, 1.0, 0.0).astype(BF16)
    lf_hi, lf_lo = _split(lf)
    b = (jnp.dot(cum_m, lf_hi, preferred_element_type=F32)
         + jnp.dot(cum_m, lf_lo, preferred_element_type=F32))
    b_end = (jnp.dot(end_m, lf_hi, preferred_element_type=F32)
             + jnp.dot(end_m, lf_lo, preferred_element_type=F32))
    q_dec = (q * jnp.exp(b)).astype(BF16)
    k_inv = (k * jnp.exp(-b)).astype(BF16)
    k_end = (k * jnp.exp(b_end - b)).astype(BF16)
    dec = jnp.exp(b_end)
    vb = zi.astype(BF16)

    r128, c128 = _iota2((128, 128))
    intra_mask = _same_block(r128, c128, HGRN_CHUNK) & (c128 <= r128)
    og = og_ref[...]
    o_heads = []
    for h in range(HGRN_HEADS):
        hs = slice(h * HGRN_DK, (h + 1) * HGRN_DK)
        intra = []
        for rb in range(tm // 128):
            rs = slice(rb * 128, (rb + 1) * 128)
            sc = _dot_nt(q_dec[rs, hs], k_inv[rs, hs])
            sc = jnp.where(intra_mask, sc, 0.0)
            intra.append(_dot(sc, vb[rs, hs]))
        o_intra = jnp.concatenate(intra, axis=0)
        st = st_ref[h]
        inter = []
        for c in range(tm // HGRN_CHUNK):
            cs = slice(c * HGRN_CHUNK, (c + 1) * HGRN_CHUNK)
            inter.append(_dot_nt(q_dec[cs, hs], st))
            inc_t = _dot_tn(vb[cs, hs], k_end[cs, hs])
            st = st * dec[c * HGRN_CHUNK:c * HGRN_CHUNK + 1, hs] + inc_t
        st_ref[h] = st
        o_h = o_intra + jnp.concatenate(inter, axis=0)
        o_h = o_h * lax.rsqrt(jnp.mean(o_h * o_h, axis=-1, keepdims=True) + RMS_EPS) * og
        o_heads.append(o_h * _silu(zg[:, hs]))

    y = jnp.concatenate([y_a] + o_heads, axis=1)
    o_ref[0] = x + _dot(y, wout_ref[...])


def _mixer0(x, g, w_in, w_out, gn_g, w_s, bias_b, lb, og):
    bsz, t, d = x.shape
    grid = (bsz, t // MIX_TM)
    const = lambda shape: pl.BlockSpec(shape, lambda b, i: (0,) * len(shape))
    return pl.pallas_call(
        _mixer0_kernel,
        grid=grid,
        in_specs=[
            pl.BlockSpec((1, MIX_TM, d), lambda b, i: (b, i, 0)),
            const((1, d)),
            const((d, MIX_IN)),
            const((2 * MIX_HALF, d)),
            const((1, MIX_HALF)),
            const((GMLP_GROUPS, GMLP_CHUNK, GMLP_CHUNK)),
            const((GMLP_CHUNK, MIX_HALF)),
            const((1, MIX_HALF)),
            const((1, HGRN_DK)),
        ],
        out_specs=pl.BlockSpec((1, MIX_TM, d), lambda b, i: (b, i, 0)),
        out_shape=jax.ShapeDtypeStruct((bsz, t, d), F32),
        scratch_shapes=[pltpu.VMEM((HGRN_HEADS, HGRN_DK, HGRN_DK), F32)],
        compiler_params=_cparams(("arbitrary", "arbitrary")),
        name="mixer0",
    )(x, g, w_in, w_out, gn_g, w_s, bias_b, lb, og)


def _ffn_kernel(x_ref, g_ref, wg_ref, wu_ref, wd_ref, o_ref, hn_ref, acc_ref, *, n_j):
    j = pl.program_id(1)

    @pl.when(j == 0)
    def _():
        hn_ref[...] = _rms(x_ref[...], g_ref[...]).astype(BF16)
        acc_ref[...] = jnp.zeros_like(acc_ref)

    hn = hn_ref[...]
    a = jnp.dot(hn, wg_ref[0], preferred_element_type=F32)
    b = jnp.dot(hn, wu_ref[0], preferred_element_type=F32)
    acc_ref[...] += _dot(_silu(a) * b, wd_ref[0])

    @pl.when(j == n_j - 1)
    def _():
        o_ref[...] = x_ref[...] + acc_ref[...]


def _ffn(x, g, wg, wu, wd):
    n, d = x.shape
    n_j, _, dff = wg.shape
    tm = MLP_TM
    return pl.pallas_call(
        functools.partial(_ffn_kernel, n_j=n_j),
        grid=(n // tm, n_j),
        in_specs=[pl.BlockSpec((tm, d), lambda i, j: (i, 0)),
                  pl.BlockSpec((1, d), lambda i, j: (0, 0)),
                  pl.BlockSpec((1, d, dff), lambda i, j: (j, 0, 0)),
                  pl.BlockSpec((1, d, dff), lambda i, j: (j, 0, 0)),
                  pl.BlockSpec((1, dff, d), lambda i, j: (j, 0, 0))],
        out_specs=pl.BlockSpec((tm, d), lambda i, j: (i, 0)),
        out_shape=jax.ShapeDtypeStruct((n, d), F32),
        scratch_shapes=[pltpu.VMEM((tm, d), BF16), pltpu.VMEM((tm, d), F32)],
        compiler_params=_cparams(("arbitrary", "arbitrary")),
        name="ffn_mlp",
    )(x, g, wg, wu, wd)


def _moe_router_kernel(x_ref, y_ref, wo_ref, g_ref, router_ref, rb_ref, tri_ref,
                       h_ref, hn_ref, gates_ref, rank_ref, cnt_ref):
    h = x_ref[...] + jnp.dot(y_ref[...], wo_ref[...], preferred_element_type=F32)
    h_ref[...] = h
    hn = _rms(h, g_ref[...])
    hn_ref[...] = hn.astype(BF16)
    logits = _dot3(hn, router_ref[...]) + rb_ref[...]
    lane = lax.broadcasted_iota(jnp.int32, logits.shape, 1)
    m1 = jnp.max(logits, axis=-1, keepdims=True)
    i1 = jnp.min(jnp.where(logits == m1, lane, LANES), axis=-1, keepdims=True)
    rest = jnp.where(lane == i1, -jnp.inf, logits)
    m2 = jnp.max(rest, axis=-1, keepdims=True)
    i2 = jnp.min(jnp.where(rest == m2, lane, LANES), axis=-1, keepdims=True)
    e2 = jnp.exp(m2 - m1)
    den = 1.0 + e2
    gates_ref[...] = jnp.where(lane == i1, 1.0 / den, 0.0) + jnp.where(lane == i2, e2 / den, 0.0)
    sel = (lane == i1) | (lane == i2)
    sel_f = jnp.where(sel, 1.0, 0.0)
    rank = jnp.dot(tri_ref[...], sel_f.astype(BF16), preferred_element_type=F32)
    rank_ref[0] = jnp.where(sel, rank, -1.0).T
    cnt_ref[0] = jnp.broadcast_to(jnp.sum(sel_f, axis=0, keepdims=True), cnt_ref.shape[1:])


def _moe_router(x, y, w_o, g, router, router_b, tri):
    n, d = x.shape
    tm = MOE_TT
    tile = lambda w: pl.BlockSpec((tm, w), lambda i: (i, 0))
    full = lambda arr: pl.BlockSpec(arr.shape, lambda i: (0,) * arr.ndim)
    return pl.pallas_call(
        _moe_router_kernel,
        grid=(n // tm,),
        in_specs=[tile(d), tile(d), full(w_o), full(g), full(router), full(router_b), full(tri)],
        out_specs=[tile(d), tile(d), tile(LANES), pl.BlockSpec((1, LANES, tm), lambda i: (i, 0, 0)),
                   pl.BlockSpec((1, 8, LANES), lambda i: (i, 0, 0))],
        out_shape=[jax.ShapeDtypeStruct((n, d), F32), jax.ShapeDtypeStruct((n, d), BF16),
                   jax.ShapeDtypeStruct((n, LANES), F32), jax.ShapeDtypeStruct((n // tm, LANES, tm), F32),
                   jax.ShapeDtypeStruct((n // tm, 8, LANES), F32)],
        compiler_params=_cparams(("arbitrary",)),
        name="moe_router",
    )(x, y, w_o, g, router, router_b, tri)


def _moe_expert_kernel(nch_ref, x_ref, hn_ref, gates_ref, rank_ref, wg_ref, wu_ref, wd_ref, gout_ref,
                       o_ref, acc_ref, *, n_e):
    i, e = pl.program_id(0), pl.program_id(1)
    tt, ch = MOE_TT, MOE_CH

    @pl.when(e == 0)
    def _():
        acc_ref[...] = jnp.zeros_like(acc_ref)

    lane = lax.broadcasted_iota(jnp.int32, (tt, LANES), 1)
    rank_row = rank_ref[0, pl.ds(e, 1), :]
    g_hi, g_lo = _split(jnp.where(lane == e, gates_ref[...], 0.0))
    slot = lax.broadcasted_iota(jnp.int32, (ch, tt), 0).astype(F32)
    pad = jnp.zeros((MOE_CH_PAD - ch, tt), BF16)
    ypad = jnp.zeros((MOE_CH_PAD - ch, x_ref.shape[1]), BF16)

    def chunk(k, carry):
        p = jnp.where(rank_row - (k * ch).astype(F32) == slot, 1.0, 0.0).astype(BF16)
        xs = jnp.dot(p, hn_ref[...], preferred_element_type=F32).astype(BF16)
        a = jnp.dot(xs, wg_ref[0], preferred_element_type=F32)
        b = jnp.dot(xs, wu_ref[0], preferred_element_type=F32)
        y = _dot(_silu(a) * b, wd_ref[0])
        gcol = jnp.sum(jnp.dot(p, g_hi, preferred_element_type=F32)
                       + jnp.dot(p, g_lo, preferred_element_type=F32), axis=-1, keepdims=True)
        yw = jnp.concatenate([(y * gcol).astype(BF16), ypad], axis=0)
        acc_ref[...] += _dot_tn(jnp.concatenate([p, pad], axis=0), yw)
        return carry

    lax.fori_loop(0, nch_ref[i, e], chunk, 0)

    @pl.when(e == n_e - 1)
    def _():
        o_ref[...] = _rms(x_ref[...] + acc_ref[...], gout_ref[...])


def _moe_experts(nch, x, hn, gates, rank, wg, wu, wd, g_out):
    n, d = x.shape
    n_e, _, dff = wg.shape
    tt = MOE_TT
    tile = lambda w: pl.BlockSpec((tt, w), lambda i, e, nch: (i, 0))
    grid_spec = pltpu.PrefetchScalarGridSpec(
        num_scalar_prefetch=1,
        grid=(n // tt, n_e),
        in_specs=[tile(d), tile(d), tile(LANES), pl.BlockSpec((1, LANES, tt), lambda i, e, nch: (i, 0, 0)),
                  pl.BlockSpec((1, d, dff), lambda i, e, nch: (e, 0, 0)),
                  pl.BlockSpec((1, d, dff), lambda i, e, nch: (e, 0, 0)),
                  pl.BlockSpec((1, dff, d), lambda i, e, nch: (e, 0, 0)),
                  pl.BlockSpec((1, d), lambda i, e, nch: (0, 0))],
        out_specs=tile(d),
        scratch_shapes=[pltpu.VMEM((tt, d), F32)],
    )
    return pl.pallas_call(
        functools.partial(_moe_expert_kernel, n_e=n_e),
        grid_spec=grid_spec,
        out_shape=jax.ShapeDtypeStruct((n, d), F32),
        compiler_params=_cparams(("arbitrary", "arbitrary")),
        name="moe_experts",
    )(nch, x, hn, gates, rank, wg, wu, wd, g_out)


def _rwkv_proj_kernel(h_ref, hp_ref, g_ref, mix_ref, wr_ref, wk_ref, wv_ref, w1_ref, w2_ref,
                      a1_ref, a2_ref, g1_ref, g2_ref, w0_ref, a0_ref, kk_ref, ka_ref, rk_ref,
                      hsum_ref,
                      r_out, k_out, v_out, kk_out, bb_out, lw_out, g_out, bonus_out,
                      *, tiles_per_seq):
    i = pl.program_id(0)
    g = g_ref[...]
    hn = _rms(h_ref[...], g)
    prev = _rms(hp_ref[7:8, :], g)
    prev = jnp.where(i % tiles_per_seq == 0, 0.0, prev)
    row = lax.broadcasted_iota(jnp.int32, hn.shape, 0)
    shifted = jnp.where(row == 0, prev, pltpu.roll(hn, 1, axis=0))
    dx = shifted - hn
    xr, xw, xk, xv, xa, xg = (hn + dx * mix_ref[m:m + 1, :] for m in range(6))
    r = _dot(xr, wr_ref[...])
    k = _dot(xk, wk_ref[...])
    v = _dot(xv, wv_ref[...])
    wl = w0_ref[...] + _dot(jnp.tanh(_dot(xw, w1_ref[...])), w2_ref[...])
    nwl = -wl
    w_log = -(jnp.maximum(nwl, 0.0) + jnp.log(1.0 + jnp.exp(-jnp.abs(nwl)))) - 0.5
    a = _sigmoid(a0_ref[...] + _dot(_dot(xa, a1_ref[...]), a2_ref[...]))
    gate = _dot(_sigmoid(_dot(xg, g1_ref[...])), g2_ref[...])
    hsum = hsum_ref[...]

    def head_sum(z):
        w = hsum.shape[0]
        return jnp.concatenate([_dot(z[:, s:s + w], hsum) for s in range(0, z.shape[1], w)], axis=1)

    kk = k * kk_ref[...]
    kk = kk / jnp.maximum(jnp.sqrt(head_sum(kk * kk)), 1e-12)
    k2 = k * (1.0 + (a - 1.0) * ka_ref[...])
    r_out[...] = r.astype(r_out.dtype)
    k_out[...] = k2.astype(k_out.dtype)
    v_out[...] = v.astype(v_out.dtype)
    kk_out[...] = kk.astype(kk_out.dtype)
    bb_out[...] = (kk * a).astype(bb_out.dtype)
    lw_out[...] = -jnp.exp(w_log)
    g_out[...] = gate.astype(g_out.dtype)
    bonus_out[...] = (head_sum(r * k2 * rk_ref[...]) * v).astype(bonus_out.dtype)


def _rwkv_proj(h, t, g, mix8, wr, wk, wv, w1, w2, a1, a2, g1, g2, w0, a0, k_k, k_a, r_k, hsum):
    n, d = h.shape
    tm = PROJ_TM
    full = lambda arr: pl.BlockSpec(arr.shape, lambda i: (0,) * arr.ndim)
    tile = pl.BlockSpec((tm, d), lambda i: (i, 0))
    prev = pl.BlockSpec((8, d), lambda i: (jnp.maximum(i * (tm // 8) - 1, 0), 0))
    consts = [g, mix8, wr, wk, wv, w1, w2, a1, a2, g1, g2, w0, a0, k_k, k_a, r_k, hsum]
    return pl.pallas_call(
        functools.partial(_rwkv_proj_kernel, tiles_per_seq=t // tm),
        grid=(n // tm,),
        in_specs=[tile, prev] + [full(c) for c in consts],
        out_specs=[tile] * 8,
        out_shape=[jax.ShapeDtypeStruct((n, d), F32 if idx == 5 else BF16) for idx in range(8)],
        compiler_params=_cparams(("arbitrary",)),
        name="rwkv_proj",
    )(h, h, *consts)


def _rwkv_rec_kernel(r_ref, k_ref, v_ref, kk_ref, bb_ref, lw_ref, bonus_ref, gate_ref,
                     lng_ref, lnb_ref, o_ref, st_ref):
    c, hd, w = REC_C, RWKV_HEAD, REC_W
    bsz, _, d = r_ref.shape
    units = [(b, u) for b in range(bsz) for u in range(d // w)]

    @pl.when(pl.program_id(0) == 0)
    def _():
        st_ref[...] = jnp.zeros_like(st_ref)

    rr, cc = _iota2((c, c))
    cum_m = jnp.where(cc <= rr, 1.0, 0.0).astype(BF16)
    lane_c = lax.broadcasted_iota(jnp.int32, (c, w), 1)
    row_c = lax.broadcasted_iota(jnp.int32, (c, w), 0)
    head_c = lane_c >> (hd.bit_length() - 1)
    colm = lane_c & (hd - 1)
    strict = colm < row_c
    incl = colm <= row_c
    eye_cat = jnp.where(colm == row_c, 1.0, 0.0)
    r2, c2 = _iota2((w, w))
    bd = _same_block(r2, c2, hd)
    diag = r2 == c2
    hmean = jnp.where(bd, 1.0 / hd, 0.0).astype(BF16)

    def stack(xm):
        return jnp.concatenate([jnp.where(head_c == h, xm, 0.0) for h in range(w // hd)], axis=0)

    prep = []
    for b in range(bsz):
        lw = lw_ref[b]
        lp = _dot_exact_lhs(cum_m, lw)
        pend = lp[c - 1:c, :]
        e_neg = jnp.exp(-lp)
        e_end = jnp.exp(pend - lp)
        kb, bbv = k_ref[b].astype(F32), bb_ref[b].astype(F32)
        prep.append(dict(rt=r_ref[b].astype(F32) * jnp.exp(lp), kt=kk_ref[b].astype(F32) * jnp.exp(lp - lw),
                         kh=kb * e_neg, bh=bbv * e_neg, khp=kb * e_end, bhp=bbv * e_end,
                         p_end=jnp.exp(pend), v=v_ref[b]))

    def usl(name, b, u):
        return prep[b][name][:, u * w:(u + 1) * w]

    m_kk, m_kb, a_rk, a_rb = [], [], [], []
    for (b, u) in units:
        lhs = jnp.concatenate([usl('kt', b, u), usl('rt', b, u)], axis=0)
        rhs = jnp.concatenate([stack(usl('kh', b, u)), stack(usl('bh', b, u))], axis=0)
        gram = _dot_nt(lhs, rhs)
        m_kk.append(jnp.where(strict, gram[0:c, 0:w], 0.0))
        m_kb.append(jnp.where(strict, gram[0:c, w:2 * w], 0.0))
        a_rk.append(jnp.where(incl, gram[c:2 * c, 0:w], 0.0))
        a_rb.append(jnp.where(incl, gram[c:2 * c, w:2 * w], 0.0))
    tinv = [eye_cat - m for m in m_kb]
    pw = [_dot(m, stack(m)) for m in m_kb]
    mv = [_dot(jnp.concatenate([m_kk[i], a_rk[i]], axis=0), stack(usl('v', b, u)))
          for i, (b, u) in enumerate(units)]
    n_lvl = (c - 1).bit_length() - 1
    for lvl in range(n_lvl):
        last = lvl == n_lvl - 1
        for i in range(len(units)):
            spw = stack(pw[i])
            if last:
                tinv[i] = tinv[i] + _dot(tinv[i], spw)
            else:
                res = _dot(jnp.concatenate([tinv[i], pw[i]], axis=0), spw)
                tinv[i] = tinv[i] + res[0:c]
                pw[i] = res[c:2 * c]
    kv = [_dot(tinv[i], jnp.concatenate([stack(usl('kt', b, u)), stack(mv[i][0:c])], axis=1))
          for i, (b, u) in enumerate(units)]
    rp, y0, phi, psi = [], [], [], []
    for i, (b, u) in enumerate(units):
        kp, vp = kv[i][:, 0:w], kv[i][:, w:2 * w]
        ab = _dot(a_rb[i], jnp.concatenate([stack(kp), stack(vp)], axis=1))
        rp.append(usl('rt', b, u) - ab[:, 0:w])
        y0.append(mv[i][c:2 * c] - ab[:, w:2 * w])
        vv = usl('v', b, u)
        xs = jnp.concatenate([usl('khp', b, u), -usl('bhp', b, u)], axis=0)
        rhs2 = jnp.concatenate([jnp.concatenate([vv, jnp.zeros_like(vv)], axis=1),
                                jnp.concatenate([vp, kp], axis=1)], axis=0)
        pp = _dot_tn(xs, rhs2)
        psi.append(jnp.where(bd, pp[:, 0:w], 0.0))
        phi.append(jnp.where(bd, pp[:, w:2 * w], 0.0) + jnp.where(diag, usl('p_end', b, u), 0.0))
    ys = []
    for i, (b, u) in enumerate(units):
        a_bd = st_ref[b, u]
        ys.append(_dot(rp[i], a_bd) + y0[i])
        st_ref[b, u] = _dot(phi[i], a_bd) + psi[i]
    y_all = jnp.concatenate(ys, axis=0)
    dlt = y_all - _dot(y_all, hmean)
    yn_all = dlt * lax.rsqrt(_dot(dlt * dlt, hmean) + GN_EPS)
    for i, (b, u) in enumerate(units):
        ls = slice(u * w, (u + 1) * w)
        yn = yn_all[i * c:(i + 1) * c] * lng_ref[:, ls] + lnb_ref[:, ls]
        o_ref[b, :, ls] = ((yn + bonus_ref[b, :, ls].astype(F32))
                           * gate_ref[b, :, ls].astype(F32)).astype(o_ref.dtype)


def _rwkv_rec(r, k, v, kk, bb, lw, bonus, gate, ln_g, ln_b):
    bsz, t, d = r.shape
    blk = pl.BlockSpec((bsz, REC_C, d), lambda i: (0, i, 0))
    vec = pl.BlockSpec((1, d), lambda i: (0, 0))
    return pl.pallas_call(
        _rwkv_rec_kernel,
        grid=(t // REC_C,),
        in_specs=[blk] * 8 + [vec, vec],
        out_specs=blk,
        out_shape=jax.ShapeDtypeStruct((bsz, t, d), BF16),
        scratch_shapes=[pltpu.VMEM((bsz, d // REC_W, REC_W, REC_W), F32)],
        compiler_params=_cparams(("arbitrary",)),
        name="rwkv_rec",
    )(r, k, v, kk, bb, lw, bonus, gate, ln_g, ln_b)


def kernel(x, norm_mix_g, norm_ffn_g, norm_out_g, mix_w_in, mix_w_out, gmlp_norm_g, gmlp_w_s, gmlp_b_s, hgrn_lb_logits, hgrn_onorm_g, ffn_w_gate, ffn_w_up, ffn_w_down, rwkv_mix, rwkv_w_r, rwkv_w_k, rwkv_w_v, rwkv_w_o, rwkv_w0, rwkv_w1, rwkv_w2, rwkv_a0, rwkv_a1, rwkv_a2, rwkv_g1, rwkv_g2, rwkv_k_k, rwkv_k_a, rwkv_r_k, rwkv_ln_g, rwkv_ln_b, moe_router, moe_router_b, moe_w_gate, moe_w_up, moe_w_down):
    bsz, t, d = x.shape
    n = bsz * t
    row = lambda vec: vec.reshape(1, -1).astype(F32)
    bf = lambda w: w.astype(BF16)

    lower_bounds = jnp.cumsum(jax.nn.softmax(hgrn_lb_logits.astype(F32), axis=0), axis=0)
    bias_b = jnp.repeat(gmlp_b_s[0].astype(F32).T, GMLP_DIM, axis=1)
    h = _mixer0(x, row(norm_mix_g[0]), bf(mix_w_in[0]), bf(mix_w_out[0]), row(gmlp_norm_g[0]),
                gmlp_w_s[0].astype(F32), bias_b, row(lower_bounds[0]), row(hgrn_onorm_g[0]))
    h = h.reshape(n, d)
    n_half = ffn_w_gate.shape[2] // D_FF_EXPERT
    halves = lambda w: bf(w).reshape(d, n_half, D_FF_EXPERT).transpose(1, 0, 2)
    h = _ffn(h, row(norm_ffn_g[0]), halves(ffn_w_gate[0]), halves(ffn_w_up[0]),
             bf(ffn_w_down[0]).reshape(n_half, D_FF_EXPERT, d))

    mix8 = jnp.concatenate([rwkv_mix[0].astype(F32), jnp.zeros((2, d), F32)], axis=0)
    head_id = jnp.arange(REC_W) // RWKV_HEAD
    hsum = (head_id[:, None] == head_id[None, :]).astype(BF16)
    r, k2, v, kk, bb, lw, gate, bonus = _rwkv_proj(
        h, t, row(norm_mix_g[1]), mix8, bf(rwkv_w_r[0]), bf(rwkv_w_k[0]), bf(rwkv_w_v[0]),
        bf(rwkv_w1[0]), bf(rwkv_w2[0]), bf(rwkv_a1[0]), bf(rwkv_a2[0]), bf(rwkv_g1[0]), bf(rwkv_g2[0]),
        row(rwkv_w0[0]), row(rwkv_a0[0]), row(rwkv_k_k[0]), row(rwkv_k_a[0]), row(rwkv_r_k[0]), hsum)
    sh = lambda z: z.reshape(bsz, t, d)
    yg = _rwkv_rec(sh(r), sh(k2), sh(v), sh(kk), sh(bb), sh(lw), sh(bonus), sh(gate),
                   row(rwkv_ln_g[0]), row(rwkv_ln_b[0]))
    router = jnp.zeros((d, LANES), F32).at[:, :N_EXPERTS].set(moe_router[0].astype(F32))
    router_b = jnp.full((1, LANES), -1e30, F32).at[0, :N_EXPERTS].set(moe_router_b[0].astype(F32))
    tok = jnp.arange(MOE_TT)
    tri = (tok[None, :] < tok[:, None]).astype(BF16)
    h, hn, gates, rank, counts = _moe_router(h, yg.reshape(n, d), bf(rwkv_w_o[0]), row(norm_ffn_g[1]),
                                             router, router_b, tri)
    n_chunks = ((counts[:, 0, :N_EXPERTS].astype(jnp.int32) + (MOE_CH - 1)) // MOE_CH)
    out = _moe_experts(n_chunks, h, hn, gates, rank, bf(moe_w_gate[0]), bf(moe_w_up[0]), bf(moe_w_down[0]),
                       row(norm_out_g))
    return out.reshape(bsz, t, d)
```

```python
import functools

import jax
import jax.numpy as jnp
from jax import lax
from jax.experimental import pallas as pl
from jax.experimental.pallas import tpu as pltpu

F32 = jnp.float32
BF16 = jnp.bfloat16

D_MODEL = 1024
GMLP_GROUPS = 4
GMLP_DIM = 128
GMLP_CHUNK = 128
HGRN_HEADS = 4
HGRN_DK = 128
HGRN_CHUNK = 32
MIX_HALF = 512
MIX_IN = 3072
RWKV_HEAD = 64
N_EXPERTS = 8
D_FF_EXPERT = 1408
RMS_EPS = 1e-6
LN_EPS = 1e-5
GN_EPS = 64e-5

LANES = 128
VMEM_LIMIT_BYTES = 56 * 1024 * 1024

MIX_TM = 256
MLP_TM = 512
MOE_TT = 1024
MOE_CH = 144
MOE_CH_PAD = 256
PROJ_TM = 256
REC_C = 64
REC_HPU = 4
REC_W = REC_HPU * RWKV_HEAD


def _cparams(sem):
    return pltpu.CompilerParams(dimension_semantics=sem, vmem_limit_bytes=VMEM_LIMIT_BYTES)


def _dot(a, b):
    return jnp.dot(a.astype(BF16), b.astype(BF16), preferred_element_type=F32)


def _dot_nt(a, b):
    return lax.dot_general(a.astype(BF16), b.astype(BF16), (((1,), (1,)), ((), ())),
                           preferred_element_type=F32)


def _dot_tn(a, b):
    return lax.dot_general(a.astype(BF16), b.astype(BF16), (((0,), (0,)), ((), ())),
                           preferred_element_type=F32)


def _split(a):
    hi = a.astype(BF16)
    lo = (a - hi.astype(F32)).astype(BF16)
    return hi, lo


def _dot_exact_lhs(m_bf16, a):
    hi, lo = _split(a)
    return (jnp.dot(m_bf16, hi, preferred_element_type=F32)
            + jnp.dot(m_bf16, lo, preferred_element_type=F32))


def _dot3(a, b):
    a_hi, a_lo = _split(a)
    b_hi, b_lo = _split(b)
    return (jnp.dot(a_hi, b_hi, preferred_element_type=F32)
            + jnp.dot(a_hi, b_lo, preferred_element_type=F32)
            + jnp.dot(a_lo, b_hi, preferred_element_type=F32))


def _rms(x, g):
    return x * lax.rsqrt(jnp.mean(x * x, axis=-1, keepdims=True) + RMS_EPS) * g


def _sigmoid(x):
    return jax.nn.sigmoid(x)


def _silu(x):
    return x * jax.nn.sigmoid(x)


def _iota2(shape):
    return (lax.broadcasted_iota(jnp.int32, shape, 0), lax.broadcasted_iota(jnp.int32, shape, 1))


def _same_block(a, b, size):
    shift = size.bit_length() - 1
    assert 1 << shift == size
    return (a >> shift) == (b >> shift)


def _mixer0_kernel(x_ref, g_ref, win_ref, wout_ref, gng_ref, ws_ref, bias_ref, lb_ref, og_ref,
                   o_ref, st_ref):
    tm = MIX_TM

    @pl.when(pl.program_id(1) == 0)
    def _():
        st_ref[...] = jnp.zeros_like(st_ref)

    x = x_ref[0]
    z = _dot(_rms(x, g_ref[...]), win_ref[...])

    u = jax.nn.gelu(z[:, 0:MIX_HALF])
    v = jax.nn.gelu(z[:, MIX_HALF:2 * MIX_HALF])
    row, col = _iota2((GMLP_CHUNK, GMLP_CHUNK))
    tril = col <= row
    mixed_groups = []
    for g in range(GMLP_GROUPS):
        gs = slice(g * GMLP_DIM, (g + 1) * GMLP_DIM)
        vg = v[:, gs]
        mu = jnp.mean(vg, axis=-1, keepdims=True)
        d = vg - mu
        var = jnp.mean(d * d, axis=-1, keepdims=True)
        vn = (d * lax.rsqrt(var + LN_EPS) * gng_ref[:, gs]).astype(BF16)
        wg = jnp.where(tril, ws_ref[g], 0.0).astype(BF16)
        parts = [jnp.dot(wg, vn[c * GMLP_CHUNK:(c + 1) * GMLP_CHUNK], preferred_element_type=F32)
                 for c in range(tm // GMLP_CHUNK)]
        mixed_groups.append(jnp.concatenate(parts, axis=0))
    bias = jnp.concatenate([bias_ref[...]] * (tm // GMLP_CHUNK), axis=0)
    y_a = u * (jnp.concatenate(mixed_groups, axis=1) + bias)

    o0 = 2 * MIX_HALF
    zq = z[:, o0:o0 + MIX_HALF]
    zf = z[:, o0 + MIX_HALF:o0 + 2 * MIX_HALF]
    zi = z[:, o0 + 2 * MIX_HALF:o0 + 3 * MIX_HALF]
    zg = z[:, o0 + 3 * MIX_HALF:o0 + 4 * MIX_HALF]
    lb = lb_ref[...]
    q = _silu(zq)
    f = lb + (1.0 - lb) * _sigmoid(zf)
    k = 1.0 - f
    lf = jnp.log(f)
    rr, cc = _iota2((tm, tm))
    same = _same_block(rr, cc, HGRN_CHUNK)
    cum_m = jnp.where(same & (cc <= rr), 1.0, 0.0).astype(BF16)
    end_m = jnp.where(same, 1.0, 0.0).astype(BF16)
    lf_hi, lf_lo = _split(lf)
    b = (jnp.dot(cum_m, lf_hi, preferred_element_type=F32)
         + jnp.dot(cum_m, lf_lo, preferred_element_type=F32))
    b_end = (jnp.dot(end_m, lf_hi, preferred_element_type=F32)
             + jnp.dot(end_m, lf_lo, preferred_element_type=F32))
    q_dec = (q * jnp.exp(b)).astype(BF16)
    k_inv = (k * jnp.exp(-b)).astype(BF16)
    k_end = (k * jnp.exp(b_end - b)).astype(BF16)
    dec = jnp.exp(b_end)
    vb = zi.astype(BF16)

    r128, c128 = _iota2((128, 128))
    intra_mask = _same_block(r128, c128, HGRN_CHUNK) & (c128 <= r128)
    og = og_ref[...]
    o_heads = []
    for h in range(HGRN_HEADS):
        hs = slice(h * HGRN_DK, (h + 1) * HGRN_DK)
        intra = []
        for rb in range(tm // 128):
            rs = slice(rb * 128, (rb + 1) * 128)
            sc = _dot_nt(q_dec[rs, hs], k_inv[rs, hs])
            sc = jnp.where(intra_mask, sc, 0.0)
            intra.append(_dot(sc, vb[rs, hs]))
        o_intra = jnp.concatenate(intra, axis=0)
        st = st_ref[h]
        inter = []
        for c in range(tm // HGRN_CHUNK):
            cs = slice(c * HGRN_CHUNK, (c + 1) * HGRN_CHUNK)
            inter.append(_dot_nt(q_dec[cs, hs], st))
            inc_t = _dot_tn(vb[cs, hs], k_end[cs, hs])
            st = st * dec[c * HGRN_CHUNK:c * HGRN_CHUNK + 1, hs] + inc_t
        st_ref[h] = st
        o_h = o_intra + jnp.concatenate(inter, axis=0)
        o_h = o_h * lax.rsqrt(jnp.mean(o_h * o_h, axis=-1, keepdims=True) + RMS_EPS) * og
        o_heads.append(o_h * _silu(zg[:, hs]))

    y = jnp.concatenate([y_a] + o_heads, axis=1)
    o_ref[0] = x + _dot(y, wout_ref[...])


def _mixer0(x, g, w_in, w_out, gn_g, w_s, bias_b, lb, og):
    bsz, t, d = x.shape
    grid = (bsz, t // MIX_TM)
    const = lambda shape: pl.BlockSpec(shape, lambda b, i: (0,) * len(shape))
    return pl.pallas_call(
        _mixer0_kernel,
        grid=grid,
        in_specs=[
            pl.BlockSpec((1, MIX_TM, d), lambda b, i: (b, i, 0)),
            const((1, d)),
            const((d, MIX_IN)),
            const((2 * MIX_HALF, d)),
            const((1, MIX_HALF)),
            const((GMLP_GROUPS, GMLP_CHUNK, GMLP_CHUNK)),
            const((GMLP_CHUNK, MIX_HALF)),
            const((1, MIX_HALF)),
            const((1, HGRN_DK)),
        ],
        out_specs=pl.BlockSpec((1, MIX_TM, d), lambda b, i: (b, i, 0)),
        out_shape=jax.ShapeDtypeStruct((bsz, t, d), F32),
        scratch_shapes=[pltpu.VMEM((HGRN_HEADS, HGRN_DK, HGRN_DK), F32)],
        compiler_params=_cparams(("arbitrary", "arbitrary")),
        name="mixer0",
    )(x, g, w_in, w_out, gn_g, w_s, bias_b, lb, og)


def _swiglu(xb, wgu, wd):
    dff = wd.shape[0]
    ab = jnp.dot(xb, wgu, preferred_element_type=F32)
    return _dot(_silu(ab[:, 0:dff]) * ab[:, dff:2 * dff], wd)


def _ffn_kernel(x_ref, g_ref, wgu_ref, wd_ref, o_ref, hn_ref, acc_ref, *, n_j):
    j = pl.program_id(1)

    @pl.when(j == 0)
    def _():
        hn_ref[...] = _rms(x_ref[...], g_ref[...]).astype(BF16)
        acc_ref[...] = jnp.zeros_like(acc_ref)

    acc_ref[...] += _swiglu(hn_ref[...], wgu_ref[0], wd_ref[0])

    @pl.when(j == n_j - 1)
    def _():
        o_ref[...] = x_ref[...] + acc_ref[...]


def _ffn(x, g, wgu, wd):
    n, d = x.shape
    n_j, dff, _ = wd.shape
    tm = MLP_TM
    return pl.pallas_call(
        functools.partial(_ffn_kernel, n_j=n_j),
        grid=(n // tm, n_j),
        in_specs=[pl.BlockSpec((tm, d), lambda i, j: (i, 0)),
                  pl.BlockSpec((1, d), lambda i, j: (0, 0)),
                  pl.BlockSpec((1, d, 2 * dff), lambda i, j: (j, 0, 0)),
                  pl.BlockSpec((1, dff, d), lambda i, j: (j, 0, 0))],
        out_specs=pl.BlockSpec((tm, d), lambda i, j: (i, 0)),
        out_shape=jax.ShapeDtypeStruct((n, d), F32),
        scratch_shapes=[pltpu.VMEM((tm, d), BF16), pltpu.VMEM((tm, d), F32)],
        compiler_params=_cparams(("arbitrary", "arbitrary")),
        name="ffn_mlp",
    )(x, g, wgu, wd)


def _moe_router_kernel(x_ref, y_ref, wo_ref, g_ref, router_ref, rb_ref, tri_ref,
                       h_ref, hn_ref, gates_ref, rank_ref, cnt_ref):
    h = x_ref[...] + jnp.dot(y_ref[...], wo_ref[...], preferred_element_type=F32)
    h_ref[...] = h
    hn = _rms(h, g_ref[...])
    hn_ref[...] = hn.astype(BF16)
    logits = _dot3(hn, router_ref[...]) + rb_ref[...]
    lane = lax.broadcasted_iota(jnp.int32, logits.shape, 1)
    m1 = jnp.max(logits, axis=-1, keepdims=True)
    i1 = jnp.min(jnp.where(logits == m1, lane, LANES), axis=-1, keepdims=True)
    rest = jnp.where(lane == i1, -jnp.inf, logits)
    m2 = jnp.max(rest, axis=-1, keepdims=True)
    i2 = jnp.min(jnp.where(rest == m2, lane, LANES), axis=-1, keepdims=True)
    e2 = jnp.exp(m2 - m1)
    den = 1.0 + e2
    gates_ref[...] = jnp.where(lane == i1, 1.0 / den, 0.0) + jnp.where(lane == i2, e2 / den, 0.0)
    sel = (lane == i1) | (lane == i2)
    sel_f = jnp.where(sel, 1.0, 0.0)
    rank = jnp.dot(tri_ref[...], sel_f.astype(BF16), preferred_element_type=F32)
    rank_ref[0] = jnp.where(sel, rank, -1.0).T
    cnt_ref[0] = jnp.broadcast_to(jnp.sum(sel_f, axis=0, keepdims=True), cnt_ref.shape[1:])


def _moe_router(x, y, w_o, g, router, router_b, tri):
    n, d = x.shape
    tm = MOE_TT
    tile = lambda w: pl.BlockSpec((tm, w), lambda i: (i, 0))
    full = lambda arr: pl.BlockSpec(arr.shape, lambda i: (0,) * arr.ndim)
    return pl.pallas_call(
        _moe_router_kernel,
        grid=(n // tm,),
        in_specs=[tile(d), tile(d), full(w_o), full(g), full(router), full(router_b), full(tri)],
        out_specs=[tile(d), tile(d), tile(LANES), pl.BlockSpec((1, LANES, tm), lambda i: (i, 0, 0)),
                   pl.BlockSpec((1, 8, LANES), lambda i: (i, 0, 0))],
        out_shape=[jax.ShapeDtypeStruct((n, d), F32), jax.ShapeDtypeStruct((n, d), BF16),
                   jax.ShapeDtypeStruct((n, LANES), F32), jax.ShapeDtypeStruct((n // tm, LANES, tm), F32),
                   jax.ShapeDtypeStruct((n // tm, 8, LANES), F32)],
        compiler_params=_cparams(("arbitrary",)),
        name="moe_router",
    )(x, y, w_o, g, router, router_b, tri)


def _moe_expert_kernel(nch_ref, x_ref, hn_ref, gates_ref, rank_ref, wgu_ref, wd_ref, gout_ref,
                       o_ref, acc_ref, *, n_e):
    i, e = pl.program_id(0), pl.program_id(1)
    tt, ch = MOE_TT, MOE_CH

    @pl.when(e == 0)
    def _():
        acc_ref[...] = jnp.zeros_like(acc_ref)

    lane = lax.broadcasted_iota(jnp.int32, (tt, LANES), 1)
    rank_row = rank_ref[0, pl.ds(e, 1), :]
    g_hi, g_lo = _split(jnp.where(lane == e, gates_ref[...], 0.0))
    slot = lax.broadcasted_iota(jnp.int32, (ch, tt), 0).astype(F32)
    pad = jnp.zeros((MOE_CH_PAD - ch, tt), BF16)
    ypad = jnp.zeros((MOE_CH_PAD - ch, x_ref.shape[1]), BF16)

    def chunk(k, carry):
        p = jnp.where(rank_row - (k * ch).astype(F32) == slot, 1.0, 0.0).astype(BF16)
        xs = jnp.dot(p, hn_ref[...], preferred_element_type=F32).astype(BF16)
        y = _swiglu(xs, wgu_ref[0], wd_ref[0])
        gcol = jnp.sum(jnp.dot(p, g_hi, preferred_element_type=F32)
                       + jnp.dot(p, g_lo, preferred_element_type=F32), axis=-1, keepdims=True)
        yw = jnp.concatenate([(y * gcol).astype(BF16), ypad], axis=0)
        acc_ref[...] += _dot_tn(jnp.concatenate([p, pad], axis=0), yw)
        return carry

    lax.fori_loop(0, nch_ref[i, e], chunk, 0)

    @pl.when(e == n_e - 1)
    def _():
        o_ref[...] = _rms(x_ref[...] + acc_ref[...], gout_ref[...])


def _moe_experts(nch, x, hn, gates, rank, wgu, wd, g_out):
    n, d = x.shape
    n_e, dff, _ = wd.shape
    tt = MOE_TT
    tile = lambda w: pl.BlockSpec((tt, w), lambda i, e, nch: (i, 0))
    grid_spec = pltpu.PrefetchScalarGridSpec(
        num_scalar_prefetch=1,
        grid=(n // tt, n_e),
        in_specs=[tile(d), tile(d), tile(LANES), pl.BlockSpec((1, LANES, tt), lambda i, e, nch: (i, 0, 0)),
                  pl.BlockSpec((1, d, 2 * dff), lambda i, e, nch: (e, 0, 0)),
                  pl.BlockSpec((1, dff, d), lambda i, e, nch: (e, 0, 0)),
                  pl.BlockSpec((1, d), lambda i, e, nch: (0, 0))],
        out_specs=tile(d),
        scratch_shapes=[pltpu.VMEM((tt, d), F32)],
    )
    return pl.pallas_call(
        functools.partial(_moe_expert_kernel, n_e=n_e),
        grid_spec=grid_spec,
        out_shape=jax.ShapeDtypeStruct((n, d), F32),
        compiler_params=_cparams(("arbitrary", "arbitrary")),
        name="moe_experts",
    )(nch, x, hn, gates, rank, wgu, wd, g_out)


def _rwkv_proj_kernel(h_ref, hp_ref, g_ref, mix_ref, wr_ref, wk_ref, wv_ref, w1_ref, w2_ref,
                      a1_ref, a2_ref, g1_ref, g2_ref, w0_ref, a0_ref, kk_ref, ka_ref, rk_ref,
                      hsum_ref,
                      r_out, k_out, v_out, kk_out, bb_out, lw_out, g_out, bonus_out,
                      *, tiles_per_seq):
    i = pl.program_id(0)
    g = g_ref[...]
    hn = _rms(h_ref[...], g)
    prev = _rms(hp_ref[7:8, :], g)
    prev = jnp.where(i % tiles_per_seq == 0, 0.0, prev)
    row = lax.broadcasted_iota(jnp.int32, hn.shape, 0)
    shifted = jnp.where(row == 0, prev, pltpu.roll(hn, 1, axis=0))
    dx = (shifted - hn).astype(BF16)
    hn_b = hn.astype(BF16)
    mix = mix_ref[...].astype(BF16)
    xr, xw, xk, xv, xa, xg = (hn_b + dx * mix[m:m + 1, :] for m in range(6))
    r = _dot(xr, wr_ref[...])
    k = _dot(xk, wk_ref[...])
    v = _dot(xv, wv_ref[...])
    wl = w0_ref[...] + _dot(jnp.tanh(_dot(xw, w1_ref[...])), w2_ref[...])
    nwl = -wl
    w_log = -(jnp.maximum(nwl, 0.0) + jnp.log(1.0 + jnp.exp(-jnp.abs(nwl)))) - 0.5
    a = _sigmoid(a0_ref[...] + _dot(_dot(xa, a1_ref[...]), a2_ref[...]))
    gate = _dot(_sigmoid(_dot(xg, g1_ref[...])), g2_ref[...])
    hsum = hsum_ref[...]

    def head_sum(z):
        w = hsum.shape[0]
        return jnp.concatenate([_dot(z[:, s:s + w], hsum) for s in range(0, z.shape[1], w)], axis=1)

    kk = k * kk_ref[...]
    kk = kk / jnp.maximum(jnp.sqrt(head_sum(kk * kk)), 1e-12)
    k2 = k * (1.0 + (a - 1.0) * ka_ref[...])
    r_out[...] = r.astype(r_out.dtype)
    k_out[...] = k2.astype(k_out.dtype)
    v_out[...] = v.astype(v_out.dtype)
    kk_out[...] = kk.astype(kk_out.dtype)
    bb_out[...] = (kk * a).astype(bb_out.dtype)
    lw_out[...] = -jnp.exp(w_log)
    g_out[...] = gate.astype(g_out.dtype)
    bonus_out[...] = (head_sum(r * k2 * rk_ref[...]) * v).astype(bonus_out.dtype)


def _rwkv_proj(h, t, g, mix8, wr, wk, wv, w1, w2, a1, a2, g1, g2, w0, a0, k_k, k_a, r_k, hsum):
    n, d = h.shape
    tm = PROJ_TM
    full = lambda arr: pl.BlockSpec(arr.shape, lambda i: (0,) * arr.ndim)
    tile = pl.BlockSpec((tm, d), lambda i: (i, 0))
    prev = pl.BlockSpec((8, d), lambda i: (jnp.maximum(i * (tm // 8) - 1, 0), 0))
    consts = [g, mix8, wr, wk, wv, w1, w2, a1, a2, g1, g2, w0, a0, k_k, k_a, r_k, hsum]
    return pl.pallas_call(
        functools.partial(_rwkv_proj_kernel, tiles_per_seq=t // tm),
        grid=(n // tm,),
        in_specs=[tile, prev] + [full(c) for c in consts],
        out_specs=[tile] * 8,
        out_shape=[jax.ShapeDtypeStruct((n, d), F32 if idx == 5 else BF16) for idx in range(8)],
        compiler_params=_cparams(("arbitrary",)),
        name="rwkv_proj",
    )(h, h, *consts)


def _rwkv_rec_kernel(r_ref, k_ref, v_ref, kk_ref, bb_ref, lw_ref, bonus_ref, gate_ref,
                     lng_ref, lnb_ref, o_ref, st_ref):
    c, hd, w = REC_C, RWKV_HEAD, REC_W
    bsz, _, d = r_ref.shape
    units = [(b, u) for b in range(bsz) for u in range(d // w)]

    @pl.when(pl.program_id(0) == 0)
    def _():
        st_ref[...] = jnp.zeros_like(st_ref)

    rr, cc = _iota2((c, c))
    cum_m = jnp.where(cc <= rr, 1.0, 0.0).astype(BF16)
    lane_c = lax.broadcasted_iota(jnp.int32, (c, w), 1)
    row_c = lax.broadcasted_iota(jnp.int32, (c, w), 0)
    head_c = lane_c >> (hd.bit_length() - 1)
    colm = lane_c & (hd - 1)
    strict = colm < row_c
    incl = colm <= row_c
    eye_cat = jnp.where(colm == row_c, 1.0, 0.0)
    r2, c2 = _iota2((w, w))
    bd = _same_block(r2, c2, hd)
    diag = r2 == c2
    hmean = jnp.where(bd, 1.0 / hd, 0.0).astype(BF16)

    def stack(xm):
        return jnp.concatenate([jnp.where(head_c == h, xm, 0.0) for h in range(w // hd)], axis=0)

    prep = []
    for b in range(bsz):
        lw = lw_ref[b]
        lp = _dot_exact_lhs(cum_m, lw)
        pend = lp[c - 1:c, :]
        e_neg = jnp.exp(-lp)
        e_end = jnp.exp(pend - lp)
        kb, bbv = k_ref[b].astype(F32), bb_ref[b].astype(F32)
        prep.append(dict(rt=r_ref[b].astype(F32) * jnp.exp(lp), kt=kk_ref[b].astype(F32) * jnp.exp(lp - lw),
                         kh=kb * e_neg, bh=bbv * e_neg, khp=kb * e_end, bhp=bbv * e_end,
                         p_end=jnp.exp(pend), v=v_ref[b]))

    def usl(name, b, u):
        return prep[b][name][:, u * w:(u + 1) * w]

    m_kk, m_kb, a_rk, a_rb = [], [], [], []
    for (b, u) in units:
        lhs = jnp.concatenate([usl('kt', b, u), usl('rt', b, u)], axis=0)
        rhs = jnp.concatenate([stack(usl('kh', b, u)), stack(usl('bh', b, u))], axis=0)
        gram = _dot_nt(lhs, rhs)
        m_kk.append(jnp.where(strict, gram[0:c, 0:w], 0.0))
        m_kb.append(jnp.where(strict, gram[0:c, w:2 * w], 0.0))
        a_rk.append(jnp.where(incl, gram[c:2 * c, 0:w], 0.0))
        a_rb.append(jnp.where(incl, gram[c:2 * c, w:2 * w], 0.0))
    tinv = [eye_cat - m for m in m_kb]
    pw = [_dot(m, stack(m)) for m in m_kb]
    mv = [_dot(jnp.concatenate([m_kk[i], a_rk[i]], axis=0), stack(usl('v', b, u)))
          for i, (b, u) in enumerate(units)]
    n_lvl = (c - 1).bit_length() - 1
    for lvl in range(n_lvl):
        last = lvl == n_lvl - 1
        for i in range(len(units)):
            spw = stack(pw[i])
            if last:
                tinv[i] = tinv[i] + _dot(tinv[i], spw)
            else:
                res = _dot(jnp.concatenate([tinv[i], pw[i]], axis=0), spw)
                tinv[i] = tinv[i] + res[0:c]
                pw[i] = res[c:2 * c]
    kv = [_dot(tinv[i], jnp.concatenate([stack(usl('kt', b, u)), stack(mv[i][0:c])], axis=1))
          for i, (b, u) in enumerate(units)]
    rp, y0, phi, psi = [], [], [], []
    for i, (b, u) in enumerate(units):
        kp, vp = kv[i][:, 0:w], kv[i][:, w:2 * w]
        ab = _dot(a_rb[i], jnp.concatenate([stack(kp), stack(vp)], axis=1))
        rp.append(usl('rt', b, u) - ab[:, 0:w])
        y0.append(mv[i][c:2 * c] - ab[:, w:2 * w])
        vv = usl('v', b, u)
        xs = jnp.concatenate([usl('khp', b, u), -usl('bhp', b, u)], axis=0)
        rhs2 = jnp.concatenate([jnp.concatenate([vv, jnp.zeros_like(vv)], axis=1),
                                jnp.concatenate([vp, kp], axis=1)], axis=0)
        pp = _dot_tn(xs, rhs2)
        psi.append(jnp.where(bd, pp[:, 0:w], 0.0))
        phi.append(jnp.where(bd, pp[:, w:2 * w], 0.0) + jnp.where(diag, usl('p_end', b, u), 0.0))
    ys = []
    for i, (b, u) in enumerate(units):
        a_bd = st_ref[b, u]
        ys.append(_dot(rp[i], a_bd) + y0[i])
        st_ref[b, u] = _dot(phi[i], a_bd) + psi[i]
    y_all = jnp.concatenate(ys, axis=0)
    dlt = y_all - _dot(y_all, hmean)
    yn_all = dlt * lax.rsqrt(_dot(dlt * dlt, hmean) + GN_EPS)
    for i, (b, u) in enumerate(units):
        ls = slice(u * w, (u + 1) * w)
        yn = yn_all[i * c:(i + 1) * c] * lng_ref[:, ls] + lnb_ref[:, ls]
        o_ref[b, :, ls] = ((yn + bonus_ref[b, :, ls].astype(F32))
                           * gate_ref[b, :, ls].astype(F32)).astype(o_ref.dtype)


def _rwkv_rec(r, k, v, kk, bb, lw, bonus, gate, ln_g, ln_b):
    bsz, t, d = r.shape
    blk = pl.BlockSpec((bsz, REC_C, d), lambda i: (0, i, 0))
    vec = pl.BlockSpec((1, d), lambda i: (0, 0))
    return pl.pallas_call(
        _rwkv_rec_kernel,
        grid=(t // REC_C,),
        in_specs=[blk] * 8 + [vec, vec],
        out_specs=blk,
        out_shape=jax.ShapeDtypeStruct((bsz, t, d), BF16),
        scratch_shapes=[pltpu.VMEM((bsz, d // REC_W, REC_W, REC_W), F32)],
        compiler_params=_cparams(("arbitrary",)),
        name="rwkv_rec",
    )(r, k, v, kk, bb, lw, bonus, gate, ln_g, ln_b)


def kernel(x, norm_mix_g, norm_ffn_g, norm_out_g, mix_w_in, mix_w_out, gmlp_norm_g, gmlp_w_s, gmlp_b_s, hgrn_lb_logits, hgrn_onorm_g, ffn_w_gate, ffn_w_up, ffn_w_down, rwkv_mix, rwkv_w_r, rwkv_w_k, rwkv_w_v, rwkv_w_o, rwkv_w0, rwkv_w1, rwkv_w2, rwkv_a0, rwkv_a1, rwkv_a2, rwkv_g1, rwkv_g2, rwkv_k_k, rwkv_k_a, rwkv_r_k, rwkv_ln_g, rwkv_ln_b, moe_router, moe_router_b, moe_w_gate, moe_w_up, moe_w_down):
    bsz, t, d = x.shape
    n = bsz * t
    row = lambda vec: vec.reshape(1, -1).astype(F32)
    bf = lambda w: w.astype(BF16)

    lower_bounds = jnp.cumsum(jax.nn.softmax(hgrn_lb_logits.astype(F32), axis=0), axis=0)
    bias_b = jnp.repeat(gmlp_b_s[0].astype(F32).T, GMLP_DIM, axis=1)
    h = _mixer0(x, row(norm_mix_g[0]), bf(mix_w_in[0]), bf(mix_w_out[0]), row(gmlp_norm_g[0]),
                gmlp_w_s[0].astype(F32), bias_b, row(lower_bounds[0]), row(hgrn_onorm_g[0]))
    h = h.reshape(n, d)
    n_half = ffn_w_gate.shape[2] // D_FF_EXPERT
    halves = lambda w: bf(w).reshape(d, n_half, D_FF_EXPERT).transpose(1, 0, 2)
    wgu = jnp.concatenate([halves(ffn_w_gate[0]), halves(ffn_w_up[0])], axis=-1)
    h = _ffn(h, row(norm_ffn_g[0]), wgu, bf(ffn_w_down[0]).reshape(n_half, D_FF_EXPERT, d))

    mix8 = jnp.concatenate([rwkv_mix[0].astype(F32), jnp.zeros((2, d), F32)], axis=0)
    head_id = jnp.arange(REC_W) // RWKV_HEAD
    hsum = (head_id[:, None] == head_id[None, :]).astype(BF16)
    r, k2, v, kk, bb, lw, gate, bonus = _rwkv_proj(
        h, t, row(norm_mix_g[1]), mix8, bf(rwkv_w_r[0]), bf(rwkv_w_k[0]), bf(rwkv_w_v[0]),
        bf(rwkv_w1[0]), bf(rwkv_w2[0]), bf(rwkv_a1[0]), bf(rwkv_a2[0]), bf(rwkv_g1[0]), bf(rwkv_g2[0]),
        row(rwkv_w0[0]), row(rwkv_a0[0]), row(rwkv_k_k[0]), row(rwkv_k_a[0]), row(rwkv_r_k[0]), hsum)
    sh = lambda z: z.reshape(bsz, t, d)
    yg = _rwkv_rec(sh(r), sh(k2), sh(v), sh(kk), sh(bb), sh(lw), sh(bonus), sh(gate),
                   row(rwkv_ln_g[0]), row(rwkv_ln_b[0]))
    router = jnp.zeros((d, LANES), F32).at[:, :N_EXPERTS].set(moe_router[0].astype(F32))
    router_b = jnp.full((1, LANES), -1e30, F32).at[0, :N_EXPERTS].set(moe_router_b[0].astype(F32))
    tok = jnp.arange(MOE_TT)
    tri = (tok[None, :] < tok[:, None]).astype(BF16)
    h, hn, gates, rank, counts = _moe_router(h, yg.reshape(n, d), bf(rwkv_w_o[0]), row(norm_ffn_g[1]),
                                             router, router_b, tri)
    n_chunks = ((counts[:, 0, :N_EXPERTS].astype(jnp.int32) + (MOE_CH - 1)) // MOE_CH)
    moe_wgu = jnp.concatenate([bf(moe_w_gate[0]), bf(moe_w_up[0])], axis=-1)
    out = _moe_experts(n_chunks, h, hn, gates, rank, moe_wgu, bf(moe_w_down[0]), row(norm_out_g))
    return out.reshape(bsz, t, d)
```

```python
import functools

import jax
import jax.numpy as jnp
from jax import lax
from jax.experimental import pallas as pl
from jax.experimental.pallas import tpu as pltpu

F32 = jnp.float32
BF16 = jnp.bfloat16

D_MODEL = 1024
GMLP_GROUPS = 4
GMLP_DIM = 128
GMLP_CHUNK = 128
HGRN_HEADS = 4
HGRN_DK = 128
HGRN_CHUNK = 32
MIX_HALF = 512
MIX_IN = 3072
RWKV_HEAD = 64
N_EXPERTS = 8
D_FF_EXPERT = 1408
RMS_EPS = 1e-6
LN_EPS = 1e-5
GN_EPS = 64e-5

LANES = 128
VMEM_LIMIT_BYTES = 56 * 1024 * 1024

MIX_TM = 256
MLP_TM = 512
MOE_TT = 1024
MOE_CH = 144
MOE_CH_PAD = 256
PROJ_TM = 256
REC_C = 64
REC_HPU = 4
REC_W = REC_HPU * RWKV_HEAD


def _cparams(sem):
    return pltpu.CompilerParams(dimension_semantics=sem, vmem_limit_bytes=VMEM_LIMIT_BYTES)


def _dot(a, b):
    return jnp.dot(a.astype(BF16), b.astype(BF16), preferred_element_type=F32)


def _dot_nt(a, b):
    return lax.dot_general(a.astype(BF16), b.astype(BF16), (((1,), (1,)), ((), ())),
                           preferred_element_type=F32)


def _dot_tn(a, b):
    return lax.dot_general(a.astype(BF16), b.astype(BF16), (((0,), (0,)), ((), ())),
                           preferred_element_type=F32)


def _split(a):
    hi = a.astype(BF16)
    lo = (a - hi.astype(F32)).astype(BF16)
    return hi, lo


def _dot_exact_lhs(m_bf16, a):
    hi, lo = _split(a)
    return (jnp.dot(m_bf16, hi, preferred_element_type=F32)
            + jnp.dot(m_bf16, lo, preferred_element_type=F32))


def _dot3(a, b):
    a_hi, a_lo = _split(a)
    b_hi, b_lo = _split(b)
    return (jnp.dot(a_hi, b_hi, preferred_element_type=F32)
            + jnp.dot(a_hi, b_lo, preferred_element_type=F32)
            + jnp.dot(a_lo, b_hi, preferred_element_type=F32))


def _rms(x, g):
    return x * lax.rsqrt(jnp.mean(x * x, axis=-1, keepdims=True) + RMS_EPS) * g


def _sigmoid(x):
    return jax.nn.sigmoid(x)


def _silu(x):
    return x * jax.nn.sigmoid(x)


def _iota2(shape):
    return (lax.broadcasted_iota(jnp.int32, shape, 0), lax.broadcasted_iota(jnp.int32, shape, 1))


def _same_block(a, b, size):
    shift = size.bit_length() - 1
    assert 1 << shift == size
    return (a >> shift) == (b >> shift)


def _mixer0_kernel(x_ref, g_ref, win_ref, wout_ref, gng_ref, ws_ref, bias_ref, lb_ref, og_ref,
                   o_ref, st_ref):
    tm = MIX_TM

    @pl.when(pl.program_id(1) == 0)
    def _():
        st_ref[...] = jnp.zeros_like(st_ref)

    x = x_ref[0]
    z = _dot(_rms(x, g_ref[...]), win_ref[...])

    u = jax.nn.gelu(z[:, 0:MIX_HALF])
    v = jax.nn.gelu(z[:, MIX_HALF:2 * MIX_HALF])
    row, col = _iota2((GMLP_CHUNK, GMLP_CHUNK))
    tril = col <= row
    mixed_groups = []
    for g in range(GMLP_GROUPS):
        gs = slice(g * GMLP_DIM, (g + 1) * GMLP_DIM)
        vg = v[:, gs]
        mu = jnp.mean(vg, axis=-1, keepdims=True)
        d = vg - mu
        var = jnp.mean(d * d, axis=-1, keepdims=True)
        vn = (d * lax.rsqrt(var + LN_EPS) * gng_ref[:, gs]).astype(BF16)
        wg = jnp.where(tril, ws_ref[g], 0.0).astype(BF16)
        parts = [jnp.dot(wg, vn[c * GMLP_CHUNK:(c + 1) * GMLP_CHUNK], preferred_element_type=F32)
                 for c in range(tm // GMLP_CHUNK)]
        mixed_groups.append(jnp.concatenate(parts, axis=0))
    bias = jnp.concatenate([bias_ref[...]] * (tm // GMLP_CHUNK), axis=0)
    y_a = u * (jnp.concatenate(mixed_groups, axis=1) + bias)

    o0 = 2 * MIX_HALF
    zq = z[:, o0:o0 + MIX_HALF]
    zf = z[:, o0 + MIX_HALF:o0 + 2 * MIX_HALF]
    zi = z[:, o0 + 2 * MIX_HALF:o0 + 3 * MIX_HALF]
    zg = z[:, o0 + 3 * MIX_HALF:o0 + 4 * MIX_HALF]
    lb = lb_ref[...]
    q = _silu(zq)
    f = lb + (1.0 - lb) * _sigmoid(zf)
    k = 1.0 - f
    lf = jnp.log(f)
    rr, cc = _iota2((tm, tm))
    same = _same_block(rr, cc, HGRN_CHUNK)
    cum_m = jnp.where(same & (cc <= rr), 1.0, 0.0).astype(BF16)
    end_m = jnp.where(same, 1.0, 0.0).astype(BF16)
    lf_hi, lf_lo = _split(lf)
    b = (jnp.dot(cum_m, lf_hi, preferred_element_type=F32)
         + jnp.dot(cum_m, lf_lo, preferred_element_type=F32))
    b_end = (jnp.dot(end_m, lf_hi, preferred_element_type=F32)
             + jnp.dot(end_m, lf_lo, preferred_element_type=F32))
    q_dec = (q * jnp.exp(b)).astype(BF16)
    k_inv = (k * jnp.exp(-b)).astype(BF16)
    k_end = (k * jnp.exp(b_end - b)).astype(BF16)
    dec = jnp.exp(b_end)
    vb = zi.astype(BF16)

    r128, c128 = _iota2((128, 128))
    intra_mask = _same_block(r128, c128, HGRN_CHUNK) & (c128 <= r128)
    og = og_ref[...]
    o_heads = []
    for h in range(HGRN_HEADS):
        hs = slice(h * HGRN_DK, (h + 1) * HGRN_DK)
        intra = []
        for rb in range(tm // 128):
            rs = slice(rb * 128, (rb + 1) * 128)
            sc = _dot_nt(q_dec[rs, hs], k_inv[rs, hs])
            sc = jnp.where(intra_mask, sc, 0.0)
            intra.append(_dot(sc, vb[rs, hs]))
        o_intra = jnp.concatenate(intra, axis=0)
        st = st_ref[h]
        inter = []
        for c in range(tm // HGRN_CHUNK):
            cs = slice(c * HGRN_CHUNK, (c + 1) * HGRN_CHUNK)
            inter.append(_dot_nt(q_dec[cs, hs], st))
            inc_t = _dot_tn(vb[cs, hs], k_end[cs, hs])
            st = st * dec[c * HGRN_CHUNK:c * HGRN_CHUNK + 1, hs] + inc_t
        st_ref[h] = st
        o_h = o_intra + jnp.concatenate(inter, axis=0)
        o_h = o_h * lax.rsqrt(jnp.mean(o_h * o_h, axis=-1, keepdims=True) + RMS_EPS) * og
        o_heads.append(o_h * _silu(zg[:, hs]))

    y = jnp.concatenate([y_a] + o_heads, axis=1)
    o_ref[0] = x + _dot(y, wout_ref[...])


def _mixer0(x, g, w_in, w_out, gn_g, w_s, bias_b, lb, og):
    bsz, t, d = x.shape
    grid = (bsz, t // MIX_TM)
    const = lambda shape: pl.BlockSpec(shape, lambda b, i: (0,) * len(shape))
    return pl.pallas_call(
        _mixer0_kernel,
        grid=grid,
        in_specs=[
            pl.BlockSpec((1, MIX_TM, d), lambda b, i: (b, i, 0)),
            const((1, d)),
            const((d, MIX_IN)),
            const((2 * MIX_HALF, d)),
            const((1, MIX_HALF)),
            const((GMLP_GROUPS, GMLP_CHUNK, GMLP_CHUNK)),
            const((GMLP_CHUNK, MIX_HALF)),
            const((1, MIX_HALF)),
            const((1, HGRN_DK)),
        ],
        out_specs=pl.BlockSpec((1, MIX_TM, d), lambda b, i: (b, i, 0)),
        out_shape=jax.ShapeDtypeStruct((bsz, t, d), F32),
        scratch_shapes=[pltpu.VMEM((HGRN_HEADS, HGRN_DK, HGRN_DK), F32)],
        compiler_params=_cparams(("arbitrary", "arbitrary")),
        name="mixer0",
    )(x, g, w_in, w_out, gn_g, w_s, bias_b, lb, og)


def _swiglu(xb, wg, wu, wd):
    a = jnp.dot(xb, wg, preferred_element_type=F32)
    b = jnp.dot(xb, wu, preferred_element_type=F32)
    return _dot(_silu(a) * b, wd)


def _ffn_kernel(x_ref, g_ref, wg_ref, wu_ref, wd_ref, o_ref):
    x = x_ref[...]
    o_ref[...] = x + _swiglu(_rms(x, g_ref[...]).astype(BF16), wg_ref[...], wu_ref[...], wd_ref[...])


def _ffn(x, g, wg, wu, wd):
    n, d = x.shape
    tm = MLP_TM
    resident = lambda w: pl.BlockSpec(w.shape, lambda i: (0, 0), pipeline_mode=pl.Buffered(1))
    return pl.pallas_call(
        _ffn_kernel,
        grid=(n // tm,),
        in_specs=[pl.BlockSpec((tm, d), lambda i: (i, 0)),
                  pl.BlockSpec((1, d), lambda i: (0, 0)),
                  resident(wg), resident(wu), resident(wd)],
        out_specs=pl.BlockSpec((tm, d), lambda i: (i, 0)),
        out_shape=jax.ShapeDtypeStruct((n, d), F32),
        compiler_params=_cparams(("arbitrary",)),
        name="ffn_mlp",
    )(x, g, wg, wu, wd)


def _moe_router_kernel(x_ref, y_ref, wo_ref, g_ref, router_ref, rb_ref, tri_ref,
                       h_ref, hn_ref, gates_ref, rank_ref, cnt_ref):
    h = x_ref[...] + jnp.dot(y_ref[...], wo_ref[...], preferred_element_type=F32)
    h_ref[...] = h
    hn = _rms(h, g_ref[...])
    hn_ref[...] = hn.astype(BF16)
    logits = _dot3(hn, router_ref[...]) + rb_ref[...]
    lane = lax.broadcasted_iota(jnp.int32, logits.shape, 1)
    m1 = jnp.max(logits, axis=-1, keepdims=True)
    i1 = jnp.min(jnp.where(logits == m1, lane, LANES), axis=-1, keepdims=True)
    rest = jnp.where(lane == i1, -jnp.inf, logits)
    m2 = jnp.max(rest, axis=-1, keepdims=True)
    i2 = jnp.min(jnp.where(rest == m2, lane, LANES), axis=-1, keepdims=True)
    e2 = jnp.exp(m2 - m1)
    den = 1.0 + e2
    gates_ref[...] = jnp.where(lane == i1, 1.0 / den, 0.0) + jnp.where(lane == i2, e2 / den, 0.0)
    sel = (lane == i1) | (lane == i2)
    sel_f = jnp.where(sel, 1.0, 0.0)
    rank = jnp.dot(tri_ref[...], sel_f.astype(BF16), preferred_element_type=F32)
    rank_ref[0] = jnp.where(sel, rank, -1.0).T
    cnt_ref[0] = jnp.broadcast_to(jnp.sum(sel_f, axis=0, keepdims=True), cnt_ref.shape[1:])


def _moe_router(x, y, w_o, g, router, router_b, tri):
    n, d = x.shape
    tm = MOE_TT
    tile = lambda w: pl.BlockSpec((tm, w), lambda i: (i, 0))
    full = lambda arr: pl.BlockSpec(arr.shape, lambda i: (0,) * arr.ndim)
    return pl.pallas_call(
        _moe_router_kernel,
        grid=(n // tm,),
        in_specs=[tile(d), tile(d), full(w_o), full(g), full(router), full(router_b), full(tri)],
        out_specs=[tile(d), tile(d), tile(LANES), pl.BlockSpec((1, LANES, tm), lambda i: (i, 0, 0)),
                   pl.BlockSpec((1, 8, LANES), lambda i: (i, 0, 0))],
        out_shape=[jax.ShapeDtypeStruct((n, d), F32), jax.ShapeDtypeStruct((n, d), BF16),
                   jax.ShapeDtypeStruct((n, LANES), F32), jax.ShapeDtypeStruct((n // tm, LANES, tm), F32),
                   jax.ShapeDtypeStruct((n // tm, 8, LANES), F32)],
        compiler_params=_cparams(("arbitrary",)),
        name="moe_router",
    )(x, y, w_o, g, router, router_b, tri)


def _moe_expert_kernel(nch_ref, x_ref, hn_ref, gates_ref, rank_ref, wg_ref, wu_ref, wd_ref, gout_ref,
                       o_ref, acc_ref, *, n_e):
    i, e = pl.program_id(0), pl.program_id(1)
    tt, ch = MOE_TT, MOE_CH

    @pl.when(e == 0)
    def _():
        acc_ref[...] = jnp.zeros_like(acc_ref)

    lane = lax.broadcasted_iota(jnp.int32, (tt, LANES), 1)
    rank_row = rank_ref[0, pl.ds(e, 1), :]
    g_hi, g_lo = _split(jnp.where(lane == e, gates_ref[...], 0.0))
    slot = lax.broadcasted_iota(jnp.int32, (ch, tt), 0).astype(F32)
    pad = jnp.zeros((MOE_CH_PAD - ch, tt), BF16)
    ypad = jnp.zeros((MOE_CH_PAD - ch, x_ref.shape[1]), BF16)

    def chunk(k, carry):
        p = jnp.where(rank_row - (k * ch).astype(F32) == slot, 1.0, 0.0).astype(BF16)
        xs = jnp.dot(p, hn_ref[...], preferred_element_type=F32).astype(BF16)
        y = _swiglu(xs, wg_ref[0], wu_ref[0], wd_ref[0])
        gcol = jnp.sum(jnp.dot(p, g_hi, preferred_element_type=F32)
                       + jnp.dot(p, g_lo, preferred_element_type=F32), axis=-1, keepdims=True)
        yw = jnp.concatenate([(y * gcol).astype(BF16), ypad], axis=0)
        acc_ref[...] += _dot_tn(jnp.concatenate([p, pad], axis=0), yw)
        return carry

    lax.fori_loop(0, nch_ref[i, e], chunk, 0)

    @pl.when(e == n_e - 1)
    def _():
        o_ref[...] = _rms(x_ref[...] + acc_ref[...], gout_ref[...])


def _moe_experts(nch, x, hn, gates, rank, wg, wu, wd, g_out):
    n, d = x.shape
    n_e, _, dff = wg.shape
    tt = MOE_TT
    tile = lambda w: pl.BlockSpec((tt, w), lambda i, e, nch: (i, 0))
    grid_spec = pltpu.PrefetchScalarGridSpec(
        num_scalar_prefetch=1,
        grid=(n // tt, n_e),
        in_specs=[tile(d), tile(d), tile(LANES), pl.BlockSpec((1, LANES, tt), lambda i, e, nch: (i, 0, 0)),
                  pl.BlockSpec((1, d, dff), lambda i, e, nch: (e, 0, 0)),
                  pl.BlockSpec((1, d, dff), lambda i, e, nch: (e, 0, 0)),
                  pl.BlockSpec((1, dff, d), lambda i, e, nch: (e, 0, 0)),
                  pl.BlockSpec((1, d), lambda i, e, nch: (0, 0))],
        out_specs=tile(d),
        scratch_shapes=[pltpu.VMEM((tt, d), F32)],
    )
    return pl.pallas_call(
        functools.partial(_moe_expert_kernel, n_e=n_e),
        grid_spec=grid_spec,
        out_shape=jax.ShapeDtypeStruct((n, d), F32),
        compiler_params=_cparams(("arbitrary", "arbitrary")),
        name="moe_experts",
    )(nch, x, hn, gates, rank, wg, wu, wd, g_out)


def _rwkv_proj_kernel(h_ref, hp_ref, g_ref, mix_ref, wr_ref, wk_ref, wv_ref, w1_ref, w2_ref,
                      a1_ref, a2_ref, g1_ref, g2_ref, w0_ref, a0_ref, kk_ref, ka_ref, rk_ref,
                      hsum_ref,
                      r_out, k_out, v_out, kk_out, bb_out, lw_out, g_out, bonus_out,
                      *, tiles_per_seq):
    i = pl.program_id(0)
    g = g_ref[...]
    hn = _rms(h_ref[...], g)
    prev = _rms(hp_ref[7:8, :], g)
    prev = jnp.where(i % tiles_per_seq == 0, 0.0, prev)
    row = lax.broadcasted_iota(jnp.int32, hn.shape, 0)
    shifted = jnp.where(row == 0, prev, pltpu.roll(hn, 1, axis=0))
    dx = shifted - hn
    xr, xw, xk, xv, xa, xg = (hn + dx * mix_ref[m:m + 1, :] for m in range(6))
    r = _dot(xr, wr_ref[...])
    k = _dot(xk, wk_ref[...])
    v = _dot(xv, wv_ref[...])
    wl = w0_ref[...] + _dot(jnp.tanh(_dot(xw, w1_ref[...])), w2_ref[...])
    nwl = -wl
    w_log = -(jnp.maximum(nwl, 0.0) + jnp.log(1.0 + jnp.exp(-jnp.abs(nwl)))) - 0.5
    a = _sigmoid(a0_ref[...] + _dot(_dot(xa, a1_ref[...]), a2_ref[...]))
    gate = _dot(_sigmoid(_dot(xg, g1_ref[...])), g2_ref[...])
    hsum = hsum_ref[...]

    def head_sum(z):
        w = hsum.shape[0]
        return jnp.concatenate([_dot(z[:, s:s + w], hsum) for s in range(0, z.shape[1], w)], axis=1)

    kk = k * kk_ref[...]
    kk = kk / jnp.maximum(jnp.sqrt(head_sum(kk * kk)), 1e-12)
    k2 = k * (1.0 + (a - 1.0) * ka_ref[...])
    r_out[...] = r.astype(r_out.dtype)
    k_out[...] = k2.astype(k_out.dtype)
    v_out[...] = v.astype(v_out.dtype)
    kk_out[...] = kk.astype(kk_out.dtype)
    bb_out[...] = (kk * a).astype(bb_out.dtype)
    lw_out[...] = -jnp.exp(w_log)
    g_out[...] = gate.astype(g_out.dtype)
    bonus_out[...] = (head_sum(r * k2 * rk_ref[...]) * v).astype(bonus_out.dtype)


def _rwkv_proj(h, t, g, mix8, wr, wk, wv, w1, w2, a1, a2, g1, g2, w0, a0, k_k, k_a, r_k, hsum):
    n, d = h.shape
    tm = PROJ_TM
    full = lambda arr: pl.BlockSpec(arr.shape, lambda i: (0,) * arr.ndim)
    tile = pl.BlockSpec((tm, d), lambda i: (i, 0))
    prev = pl.BlockSpec((8, d), lambda i: (jnp.maximum(i * (tm // 8) - 1, 0), 0))
    consts = [g, mix8, wr, wk, wv, w1, w2, a1, a2, g1, g2, w0, a0, k_k, k_a, r_k, hsum]
    return pl.pallas_call(
        functools.partial(_rwkv_proj_kernel, tiles_per_seq=t // tm),
        grid=(n // tm,),
        in_specs=[tile, prev] + [full(c) for c in consts],
        out_specs=[tile] * 8,
        out_shape=[jax.ShapeDtypeStruct((n, d), F32 if idx == 5 else BF16) for idx in range(8)],
        compiler_params=_cparams(("arbitrary",)),
        name="rwkv_proj",
    )(h, h, *consts)


def _rwkv_rec_kernel(r_ref, k_ref, v_ref, kk_ref, bb_ref, lw_ref, bonus_ref, gate_ref,
                     lng_ref, lnb_ref, o_ref, st_ref):
    c, hd, w = REC_C, RWKV_HEAD, REC_W
    bsz, _, d = r_ref.shape
    units = [(b, u) for b in range(bsz) for u in range(d // w)]

    @pl.when(pl.program_id(0) == 0)
    def _():
        st_ref[...] = jnp.zeros_like(st_ref)

    rr, cc = _iota2((c, c))
    cum_m = jnp.where(cc <= rr, 1.0, 0.0).astype(BF16)
    lane_c = lax.broadcasted_iota(jnp.int32, (c, w), 1)
    row_c = lax.broadcasted_iota(jnp.int32, (c, w), 0)
    head_c = lane_c >> (hd.bit_length() - 1)
    colm = lane_c & (hd - 1)
    strict = colm < row_c
    incl = colm <= row_c
    eye_cat = jnp.where(colm == row_c, 1.0, 0.0)
    r2, c2 = _iota2((w, w))
    bd = _same_block(r2, c2, hd)
    diag = r2 == c2
    hmean = jnp.where(bd, 1.0 / hd, 0.0).astype(BF16)

    def stack(xm):
        return jnp.concatenate([jnp.where(head_c == h, xm, 0.0) for h in range(w // hd)], axis=0)

    prep = []
    for b in range(bsz):
        lw = lw_ref[b]
        lp = _dot_exact_lhs(cum_m, lw)
        pend = lp[c - 1:c, :]
        e_neg = jnp.exp(-lp)
        e_end = jnp.exp(pend - lp)
        kb, bbv = k_ref[b].astype(F32), bb_ref[b].astype(F32)
        prep.append(dict(rt=r_ref[b].astype(F32) * jnp.exp(lp), kt=kk_ref[b].astype(F32) * jnp.exp(lp - lw),
                         kh=kb * e_neg, bh=bbv * e_neg, khp=kb * e_end, bhp=bbv * e_end,
                         p_end=jnp.exp(pend), v=v_ref[b]))

    def usl(name, b, u):
        return prep[b][name][:, u * w:(u + 1) * w]

    m_kk, m_kb, a_rk, a_rb = [], [], [], []
    for (b, u) in units:
        lhs = jnp.concatenate([usl('kt', b, u), usl('rt', b, u)], axis=0)
        rhs = jnp.concatenate([stack(usl('kh', b, u)), stack(usl('bh', b, u))], axis=0)
        gram = _dot_nt(lhs, rhs)
        m_kk.append(jnp.where(strict, gram[0:c, 0:w], 0.0))
        m_kb.append(jnp.where(strict, gram[0:c, w:2 * w], 0.0))
        a_rk.append(jnp.where(incl, gram[c:2 * c, 0:w], 0.0))
        a_rb.append(jnp.where(incl, gram[c:2 * c, w:2 * w], 0.0))
    tinv = [eye_cat - m for m in m_kb]
    pw = [_dot(m, stack(m)) for m in m_kb]
    mv = [_dot(jnp.concatenate([m_kk[i], a_rk[i]], axis=0), stack(usl('v', b, u)))
          for i, (b, u) in enumerate(units)]
    n_lvl = (c - 1).bit_length() - 1
    for lvl in range(n_lvl):
        last = lvl == n_lvl - 1
        for i in range(len(units)):
            spw = stack(pw[i])
            if last:
                tinv[i] = tinv[i] + _dot(tinv[i], spw)
            else:
                res = _dot(jnp.concatenate([tinv[i], pw[i]], axis=0), spw)
                tinv[i] = tinv[i] + res[0:c]
                pw[i] = res[c:2 * c]
    kv = [_dot(tinv[i], jnp.concatenate([stack(usl('kt', b, u)), stack(mv[i][0:c])], axis=1))
          for i, (b, u) in enumerate(units)]
    rp, y0, phi, psi = [], [], [], []
    for i, (b, u) in enumerate(units):
        kp, vp = kv[i][:, 0:w], kv[i][:, w:2 * w]
        ab = _dot(a_rb[i], jnp.concatenate([stack(kp), stack(vp)], axis=1))
        rp.append(usl('rt', b, u) - ab[:, 0:w])
        y0.append(mv[i][c:2 * c] - ab[:, w:2 * w])
        vv = usl('v', b, u)
        xs = jnp.concatenate([usl('khp', b, u), -usl('bhp', b, u)], axis=0)
        rhs2 = jnp.concatenate([jnp.concatenate([vv, jnp.zeros_like(vv)], axis=1),
                                jnp.concatenate([vp, kp], axis=1)], axis=0)
        pp = _dot_tn(xs, rhs2)
        psi.append(jnp.where(bd, pp[:, 0:w], 0.0))
        phi.append(jnp.where(bd, pp[:, w:2 * w], 0.0) + jnp.where(diag, usl('p_end', b, u), 0.0))
    ys = []
    for i, (b, u) in enumerate(units):
        a_bd = st_ref[b, u]
        ys.append(_dot(rp[i], a_bd) + y0[i])
        st_ref[b, u] = _dot(phi[i], a_bd) + psi[i]
    y_all = jnp.concatenate(ys, axis=0)
    dlt = y_all - _dot(y_all, hmean)
    yn_all = dlt * lax.rsqrt(_dot(dlt * dlt, hmean) + GN_EPS)
    for i, (b, u) in enumerate(units):
        ls = slice(u * w, (u + 1) * w)
        yn = yn_all[i * c:(i + 1) * c] * lng_ref[:, ls] + lnb_ref[:, ls]
        o_ref[b, :, ls] = ((yn + bonus_ref[b, :, ls].astype(F32))
                           * gate_ref[b, :, ls].astype(F32)).astype(o_ref.dtype)


def _rwkv_rec(r, k, v, kk, bb, lw, bonus, gate, ln_g, ln_b):
    bsz, t, d = r.shape
    blk = pl.BlockSpec((bsz, REC_C, d), lambda i: (0, i, 0))
    vec = pl.BlockSpec((1, d), lambda i: (0, 0))
    return pl.pallas_call(
        _rwkv_rec_kernel,
        grid=(t // REC_C,),
        in_specs=[blk] * 8 + [vec, vec],
        out_specs=blk,
        out_shape=jax.ShapeDtypeStruct((bsz, t, d), BF16),
        scratch_shapes=[pltpu.VMEM((bsz, d // REC_W, REC_W, REC_W), F32)],
        compiler_params=_cparams(("arbitrary",)),
        name="rwkv_rec",
    )(r, k, v, kk, bb, lw, bonus, gate, ln_g, ln_b)


def kernel(x, norm_mix_g, norm_ffn_g, norm_out_g, mix_w_in, mix_w_out, gmlp_norm_g, gmlp_w_s, gmlp_b_s, hgrn_lb_logits, hgrn_onorm_g, ffn_w_gate, ffn_w_up, ffn_w_down, rwkv_mix, rwkv_w_r, rwkv_w_k, rwkv_w_v, rwkv_w_o, rwkv_w0, rwkv_w1, rwkv_w2, rwkv_a0, rwkv_a1, rwkv_a2, rwkv_g1, rwkv_g2, rwkv_k_k, rwkv_k_a, rwkv_r_k, rwkv_ln_g, rwkv_ln_b, moe_router, moe_router_b, moe_w_gate, moe_w_up, moe_w_down):
    bsz, t, d = x.shape
    n = bsz * t
    row = lambda vec: vec.reshape(1, -1).astype(F32)
    bf = lambda w: w.astype(BF16)

    lower_bounds = jnp.cumsum(jax.nn.softmax(hgrn_lb_logits.astype(F32), axis=0), axis=0)
    bias_b = jnp.repeat(gmlp_b_s[0].astype(F32).T, GMLP_DIM, axis=1)
    h = _mixer0(x, row(norm_mix_g[0]), bf(mix_w_in[0]), bf(mix_w_out[0]), row(gmlp_norm_g[0]),
                gmlp_w_s[0].astype(F32), bias_b, row(lower_bounds[0]), row(hgrn_onorm_g[0]))
    h = h.reshape(n, d)
    h = _ffn(h, row(norm_ffn_g[0]), bf(ffn_w_gate[0]), bf(ffn_w_up[0]), bf(ffn_w_down[0]))

    mix8 = jnp.concatenate([rwkv_mix[0].astype(F32), jnp.zeros((2, d), F32)], axis=0)
    head_id = jnp.arange(REC_W) // RWKV_HEAD
    hsum = (head_id[:, None] == head_id[None, :]).astype(BF16)
    r, k2, v, kk, bb, lw, gate, bonus = _rwkv_proj(
        h, t, row(norm_mix_g[1]), mix8, bf(rwkv_w_r[0]), bf(rwkv_w_k[0]), bf(rwkv_w_v[0]),
        bf(rwkv_w1[0]), bf(rwkv_w2[0]), bf(rwkv_a1[0]), bf(rwkv_a2[0]), bf(rwkv_g1[0]), bf(rwkv_g2[0]),
        row(rwkv_w0[0]), row(rwkv_a0[0]), row(rwkv_k_k[0]), row(rwkv_k_a[0]), row(rwkv_r_k[0]), hsum)
    sh = lambda z: z.reshape(bsz, t, d)
    yg = _rwkv_rec(sh(r), sh(k2), sh(v), sh(kk), sh(bb), sh(lw), sh(bonus), sh(gate),
                   row(rwkv_ln_g[0]), row(rwkv_ln_b[0]))
    router = jnp.zeros((d, LANES), F32).at[:, :N_EXPERTS].set(moe_router[0].astype(F32))
    router_b = jnp.full((1, LANES), -1e30, F32).at[0, :N_EXPERTS].set(moe_router_b[0].astype(F32))
    tok = jnp.arange(MOE_TT)
    tri = (tok[None, :] < tok[:, None]).astype(BF16)
    h, hn, gates, rank, counts = _moe_router(h, yg.reshape(n, d), bf(rwkv_w_o[0]), row(norm_ffn_g[1]),
                                             router, router_b, tri)
    n_chunks = ((counts[:, 0, :N_EXPERTS].astype(jnp.int32) + (MOE_CH - 1)) // MOE_CH)
    out = _moe_experts(n_chunks, h, hn, gates, rank, bf(moe_w_gate[0]), bf(moe_w_up[0]), bf(moe_w_down[0]),
                       row(norm_out_g))
    return out.reshape(bsz, t, d)
```

```python
import functools

import jax
import jax.numpy as jnp
from jax import lax
from jax.experimental import pallas as pl
from jax.experimental.pallas import tpu as pltpu

F32 = jnp.float32
BF16 = jnp.bfloat16

D_MODEL = 1024
GMLP_GROUPS = 4
GMLP_DIM = 128
GMLP_CHUNK = 128
HGRN_HEADS = 4
HGRN_DK = 128
HGRN_CHUNK = 32
MIX_HALF = 512
MIX_IN = 3072
RWKV_HEAD = 64
N_EXPERTS = 8
D_FF_EXPERT = 1408
RMS_EPS = 1e-6
LN_EPS = 1e-5
GN_EPS = 64e-5

LANES = 128
VMEM_LIMIT_BYTES = 56 * 1024 * 1024

MIX_TM = 256
MLP_TM = 512
MOE_TT = 1024
MOE_CH = 128
MOE_ALIGN = 16
MOE_BLK = 256
MOE_SORTED = -(-(2 * MOE_TT + N_EXPERTS * (MOE_ALIGN - 1) + MOE_CH - 1) // MOE_BLK) * MOE_BLK
PROJ_TM = 256
REC_C = 64
REC_HPU = 4
REC_W = REC_HPU * RWKV_HEAD


def _cparams(sem):
    return pltpu.CompilerParams(dimension_semantics=sem, vmem_limit_bytes=VMEM_LIMIT_BYTES)


def _dot(a, b):
    return jnp.dot(a.astype(BF16), b.astype(BF16), preferred_element_type=F32)


def _dot_nt(a, b):
    return lax.dot_general(a.astype(BF16), b.astype(BF16), (((1,), (1,)), ((), ())),
                           preferred_element_type=F32)


def _dot_tn(a, b):
    return lax.dot_general(a.astype(BF16), b.astype(BF16), (((0,), (0,)), ((), ())),
                           preferred_element_type=F32)


def _split(a):
    hi = a.astype(BF16)
    lo = (a - hi.astype(F32)).astype(BF16)
    return hi, lo


def _dot_exact_lhs(m_bf16, a):
    hi, lo = _split(a)
    return (jnp.dot(m_bf16, hi, preferred_element_type=F32)
            + jnp.dot(m_bf16, lo, preferred_element_type=F32))


def _dot3(a, b):
    a_hi, a_lo = _split(a)
    b_hi, b_lo = _split(b)
    return (jnp.dot(a_hi, b_hi, preferred_element_type=F32)
            + jnp.dot(a_hi, b_lo, preferred_element_type=F32)
            + jnp.dot(a_lo, b_hi, preferred_element_type=F32))


def _rms(x, g):
    return x * lax.rsqrt(jnp.mean(x * x, axis=-1, keepdims=True) + RMS_EPS) * g


def _sigmoid(x):
    return jax.nn.sigmoid(x)


def _silu(x):
    return x * jax.nn.sigmoid(x)


def _iota2(shape):
    return (lax.broadcasted_iota(jnp.int32, shape, 0), lax.broadcasted_iota(jnp.int32, shape, 1))


def _same_block(a, b, size):
    shift = size.bit_length() - 1
    assert 1 << shift == size
    return (a >> shift) == (b >> shift)


def _mixer0_kernel(x_ref, g_ref, win_ref, wout_ref, gng_ref, ws_ref, bias_ref, lb_ref, og_ref,
                   o_ref, st_ref):
    tm = MIX_TM

    @pl.when(pl.program_id(1) == 0)
    def _():
        st_ref[...] = jnp.zeros_like(st_ref)

    x = x_ref[0]
    z = _dot(_rms(x, g_ref[...]), win_ref[...])

    u = jax.nn.gelu(z[:, 0:MIX_HALF])
    v = jax.nn.gelu(z[:, MIX_HALF:2 * MIX_HALF])
    row, col = _iota2((GMLP_CHUNK, GMLP_CHUNK))
    tril = col <= row
    mixed_groups = []
    for g in range(GMLP_GROUPS):
        gs = slice(g * GMLP_DIM, (g + 1) * GMLP_DIM)
        vg = v[:, gs]
        mu = jnp.mean(vg, axis=-1, keepdims=True)
        d = vg - mu
        var = jnp.mean(d * d, axis=-1, keepdims=True)
        vn = (d * lax.rsqrt(var + LN_EPS) * gng_ref[:, gs]).astype(BF16)
        wg = jnp.where(tril, ws_ref[g], 0.0).astype(BF16)
        parts = [jnp.dot(wg, vn[c * GMLP_CHUNK:(c + 1) * GMLP_CHUNK], preferred_element_type=F32)
                 for c in range(tm // GMLP_CHUNK)]
        mixed_groups.append(jnp.concatenate(parts, axis=0))
    bias = jnp.concatenate([bias_ref[...]] * (tm // GMLP_CHUNK), axis=0)
    y_a = u * (jnp.concatenate(mixed_groups, axis=1) + bias)

    o0 = 2 * MIX_HALF
    zq = z[:, o0:o0 + MIX_HALF]
    zf = z[:, o0 + MIX_HALF:o0 + 2 * MIX_HALF]
    zi = z[:, o0 + 2 * MIX_HALF:o0 + 3 * MIX_HALF]
    zg = z[:, o0 + 3 * MIX_HALF:o0 + 4 * MIX_HALF]
    lb = lb_ref[...]
    q = _silu(zq)
    f = lb + (1.0 - lb) * _sigmoid(zf)
    k = 1.0 - f
    lf = jnp.log(f)
    rr, cc = _iota2((tm, tm))
    same = _same_block(rr, cc, HGRN_CHUNK)
    cum_m = jnp.where(same & (cc <= rr), 1.0, 0.0).astype(BF16)
    end_m = jnp.where(same, 1.0, 0.0).astype(BF16)
    lf_hi, lf_lo = _split(lf)
    b = (jnp.dot(cum_m, lf_hi, preferred_element_type=F32)
         + jnp.dot(cum_m, lf_lo, preferred_element_type=F32))
    b_end = (jnp.dot(end_m, lf_hi, preferred_element_type=F32)
             + jnp.dot(end_m, lf_lo, preferred_element_type=F32))
    q_dec = (q * jnp.exp(b)).astype(BF16)
    k_inv = (k * jnp.exp(-b)).astype(BF16)
    k_end = (k * jnp.exp(b_end - b)).astype(BF16)
    dec = jnp.exp(b_end)
    vb = zi.astype(BF16)

    r128, c128 = _iota2((128, 128))
    intra_mask = _same_block(r128, c128, HGRN_CHUNK) & (c128 <= r128)
    og = og_ref[...]
    o_heads = []
    for h in range(HGRN_HEADS):
        hs = slice(h * HGRN_DK, (h + 1) * HGRN_DK)
        intra = []
        for rb in range(tm // 128):
            rs = slice(rb * 128, (rb + 1) * 128)
            sc = _dot_nt(q_dec[rs, hs], k_inv[rs, hs])
            sc = jnp.where(intra_mask, sc, 0.0)
            intra.append(_dot(sc, vb[rs, hs]))
        o_intra = jnp.concatenate(intra, axis=0)
        st = st_ref[h]
        inter = []
        for c in range(tm // HGRN_CHUNK):
            cs = slice(c * HGRN_CHUNK, (c + 1) * HGRN_CHUNK)
            inter.append(_dot_nt(q_dec[cs, hs], st))
            inc_t = _dot_tn(vb[cs, hs], k_end[cs, hs])
            st = st * dec[c * HGRN_CHUNK:c * HGRN_CHUNK + 1, hs] + inc_t
        st_ref[h] = st
        o_h = o_intra + jnp.concatenate(inter, axis=0)
        o_h = o_h * lax.rsqrt(jnp.mean(o_h * o_h, axis=-1, keepdims=True) + RMS_EPS) * og
        o_heads.append(o_h * _silu(zg[:, hs]))

    y = jnp.concatenate([y_a] + o_heads, axis=1)
    o_ref[0] = x + _dot(y, wout_ref[...])


def _mixer0(x, g, w_in, w_out, gn_g, w_s, bias_b, lb, og):
    bsz, t, d = x.shape
    grid = (bsz, t // MIX_TM)
    const = lambda shape: pl.BlockSpec(shape, lambda b, i: (0,) * len(shape))
    return pl.pallas_call(
        _mixer0_kernel,
        grid=grid,
        in_specs=[
            pl.BlockSpec((1, MIX_TM, d), lambda b, i: (b, i, 0)),
            const((1, d)),
            const((d, MIX_IN)),
            const((2 * MIX_HALF, d)),
            const((1, MIX_HALF)),
            const((GMLP_GROUPS, GMLP_CHUNK, GMLP_CHUNK)),
            const((GMLP_CHUNK, MIX_HALF)),
            const((1, MIX_HALF)),
            const((1, HGRN_DK)),
        ],
        out_specs=pl.BlockSpec((1, MIX_TM, d), lambda b, i: (b, i, 0)),
        out_shape=jax.ShapeDtypeStruct((bsz, t, d), F32),
        scratch_shapes=[pltpu.VMEM((HGRN_HEADS, HGRN_DK, HGRN_DK), F32)],
        compiler_params=_cparams(("arbitrary", "arbitrary")),
        name="mixer0",
    )(x, g, w_in, w_out, gn_g, w_s, bias_b, lb, og)


def _swiglu(xb, wg, wu, wd):
    a = jnp.dot(xb, wg, preferred_element_type=F32)
    b = jnp.dot(xb, wu, preferred_element_type=F32)
    return _dot(_silu(a) * b, wd)


def _ffn_kernel(x_ref, g_ref, wg_ref, wu_ref, wd_ref, o_ref):
    x = x_ref[...]
    o_ref[...] = x + _swiglu(_rms(x, g_ref[...]).astype(BF16), wg_ref[...], wu_ref[...], wd_ref[...])


def _ffn(x, g, wg, wu, wd):
    n, d = x.shape
    tm = MLP_TM
    resident = lambda w: pl.BlockSpec(w.shape, lambda i: (0, 0), pipeline_mode=pl.Buffered(1))
    return pl.pallas_call(
        _ffn_kernel,
        grid=(n // tm,),
        in_specs=[pl.BlockSpec((tm, d), lambda i: (i, 0)),
                  pl.BlockSpec((1, d), lambda i: (0, 0)),
                  resident(wg), resident(wu), resident(wd)],
        out_specs=pl.BlockSpec((tm, d), lambda i: (i, 0)),
        out_shape=jax.ShapeDtypeStruct((n, d), F32),
        compiler_params=_cparams(("arbitrary",)),
        name="ffn_mlp",
    )(x, g, wg, wu, wd)


def _moe_router_kernel(x_ref, y_ref, wo_ref, g_ref, router_ref, rb_ref, tri_ref,
                       h_ref, hn_ref, posc_ref, posr_ref, cnt_ref):
    h = x_ref[...] + jnp.dot(y_ref[...], wo_ref[...], preferred_element_type=F32)
    h_ref[...] = h
    hn = _rms(h, g_ref[...])
    hn_ref[...] = hn.astype(BF16)
    logits = _dot3(hn, router_ref[...]) + rb_ref[...]
    lane = lax.broadcasted_iota(jnp.int32, logits.shape, 1)
    m1 = jnp.max(logits, axis=-1, keepdims=True)
    i1 = jnp.min(jnp.where(logits == m1, lane, LANES), axis=-1, keepdims=True)
    rest = jnp.where(lane == i1, -jnp.inf, logits)
    m2 = jnp.max(rest, axis=-1, keepdims=True)
    i2 = jnp.min(jnp.where(rest == m2, lane, LANES), axis=-1, keepdims=True)
    e2 = jnp.exp(m2 - m1)
    den = 1.0 + e2
    sel1, sel2 = lane == i1, lane == i2
    sel_f = jnp.where(sel1 | sel2, 1.0, 0.0)
    rank = jnp.dot(tri_ref[...], sel_f.astype(BF16), preferred_element_type=F32)
    cnt = jnp.sum(sel_f, axis=0, keepdims=True)
    seg = jnp.floor((cnt + (MOE_ALIGN - 1)) * (1.0 / MOE_ALIGN)) * MOE_ALIGN
    r2, c2 = _iota2((LANES, LANES))
    before = jnp.where(r2 < c2, 1.0, 0.0).astype(BF16)
    start = jnp.dot(jnp.broadcast_to(seg, (8, LANES)).astype(BF16), before,
                    preferred_element_type=F32)[0:1]
    pos = start + rank
    pos1 = jnp.sum(jnp.where(sel1, pos, 0.0), axis=-1, keepdims=True)
    pos2 = jnp.sum(jnp.where(sel2, pos, 0.0), axis=-1, keepdims=True)
    posc = (jnp.where(lane == 0, pos1, 0.0) + jnp.where(lane == 1, pos2, 0.0)
            + jnp.where(lane == 2, 1.0 / den, 0.0) + jnp.where(lane == 3, e2 / den, 0.0))
    posc_ref[...] = posc
    posr_ref[0] = posc.T[0:8]
    cnt_ref[0] = jnp.broadcast_to(cnt, cnt_ref.shape[1:])


def _moe_router(x, y, w_o, g, router, router_b, tri):
    n, d = x.shape
    tm = MOE_TT
    tile = lambda w: pl.BlockSpec((tm, w), lambda i: (i, 0))
    full = lambda arr: pl.BlockSpec(arr.shape, lambda i: (0,) * arr.ndim)
    small = lambda w: pl.BlockSpec((1, 8, w), lambda i: (i, 0, 0))
    return pl.pallas_call(
        _moe_router_kernel,
        grid=(n // tm,),
        in_specs=[tile(d), tile(d), full(w_o), full(g), full(router), full(router_b), full(tri)],
        out_specs=[tile(d), tile(d), tile(LANES), small(tm), small(LANES)],
        out_shape=[jax.ShapeDtypeStruct((n, d), F32), jax.ShapeDtypeStruct((n, d), BF16),
                   jax.ShapeDtypeStruct((n, LANES), F32), jax.ShapeDtypeStruct((n // tm, 8, tm), F32),
                   jax.ShapeDtypeStruct((n // tm, 8, LANES), F32)],
        compiler_params=_cparams(("arbitrary",)),
        name="moe_router",
    )(x, y, w_o, g, router, router_b, tri)


def _moe_expert_kernel(start_ref, cnt_ref, x_ref, hn_ref, posc_ref, posr_ref, wg_ref, wu_ref, wd_ref, gout_ref,
                       o_ref, xs_ref, ys_ref, *, n_e):
    i, e = pl.program_id(0), pl.program_id(1)
    tt, ch, blk = MOE_TT, MOE_CH, MOE_BLK
    n_sorted = xs_ref.shape[0]

    @pl.when(e == 0)
    def _():
        pos1, pos2 = posr_ref[0, 0:1, :], posr_ref[0, 1:2, :]
        hn = hn_ref[...]
        for rb in range(n_sorted // blk):
            srow = (lax.broadcasted_iota(jnp.int32, (blk, tt), 0) + rb * blk).astype(F32)
            p = jnp.where((srow == pos1) | (srow == pos2), 1.0, 0.0).astype(BF16)
            xs_ref[rb * blk:(rb + 1) * blk, :] = jnp.dot(p, hn, preferred_element_type=F32).astype(BF16)
        ys_ref[...] = jnp.zeros_like(ys_ref)

    start, count = start_ref[i, e], cnt_ref[i, e]
    row_in_chunk = lax.broadcasted_iota(jnp.int32, (ch, 1), 0)

    def chunk(k, carry):
        r0 = pl.multiple_of(start + k * ch, MOE_ALIGN)
        y = _swiglu(xs_ref[pl.ds(r0, ch), :], wg_ref[0], wu_ref[0], wd_ref[0])
        ys_ref[pl.ds(r0, ch), :] = jnp.where(row_in_chunk < count - k * ch, y, 0.0).astype(BF16)
        return carry

    lax.fori_loop(0, (count + (ch - 1)) // ch, chunk, 0)

    @pl.when(e == n_e - 1)
    def _():
        ys = ys_ref[...]
        for tb in range(tt // blk):
            ts = slice(tb * blk, (tb + 1) * blk)
            pc = posc_ref[ts, :]
            scol = lax.broadcasted_iota(jnp.int32, (blk, n_sorted), 1).astype(F32)
            pw = (jnp.where(scol == pc[:, 0:1], pc[:, 2:3], 0.0)
                  + jnp.where(scol == pc[:, 1:2], pc[:, 3:4], 0.0)).astype(BF16)
            moe = jnp.dot(pw, ys, preferred_element_type=F32)
            o_ref[ts, :] = _rms(x_ref[ts, :] + moe, gout_ref[...])


def _moe_experts(start, count, x, hn, posc, posr, wg, wu, wd, g_out):
    n, d = x.shape
    n_e, _, dff = wg.shape
    tt = MOE_TT
    tile = lambda w: pl.BlockSpec((tt, w), lambda i, e, *_: (i, 0))
    grid_spec = pltpu.PrefetchScalarGridSpec(
        num_scalar_prefetch=2,
        grid=(n // tt, n_e),
        in_specs=[tile(d), tile(d), tile(LANES), pl.BlockSpec((1, 8, tt), lambda i, e, *_: (i, 0, 0)),
                  pl.BlockSpec((1, d, dff), lambda i, e, *_: (e, 0, 0)),
                  pl.BlockSpec((1, d, dff), lambda i, e, *_: (e, 0, 0)),
                  pl.BlockSpec((1, dff, d), lambda i, e, *_: (e, 0, 0)),
                  pl.BlockSpec((1, d), lambda i, e, *_: (0, 0))],
        out_specs=tile(d),
        scratch_shapes=[pltpu.VMEM((MOE_SORTED, d), BF16), pltpu.VMEM((MOE_SORTED, d), BF16)],
    )
    return pl.pallas_call(
        functools.partial(_moe_expert_kernel, n_e=n_e),
        grid_spec=grid_spec,
        out_shape=jax.ShapeDtypeStruct((n, d), F32),
        compiler_params=_cparams(("arbitrary", "arbitrary")),
        name="moe_experts",
    )(start, count, x, hn, posc, posr, wg, wu, wd, g_out)


def _rwkv_proj_kernel(h_ref, hp_ref, g_ref, mix_ref, wr_ref, wk_ref, wv_ref, w1_ref, w2_ref,
                      a1_ref, a2_ref, g1_ref, g2_ref, w0_ref, a0_ref, kk_ref, ka_ref, rk_ref,
                      hsum_ref,
                      r_out, k_out, v_out, kk_out, bb_out, lw_out, g_out, bonus_out,
                      *, tiles_per_seq):
    i = pl.program_id(0)
    g = g_ref[...]
    hn = _rms(h_ref[...], g)
    prev = _rms(hp_ref[7:8, :], g)
    prev = jnp.where(i % tiles_per_seq == 0, 0.0, prev)
    row = lax.broadcasted_iota(jnp.int32, hn.shape, 0)
    shifted = jnp.where(row == 0, prev, pltpu.roll(hn, 1, axis=0))
    dx = shifted - hn
    xr, xw, xk, xv, xa, xg = (hn + dx * mix_ref[m:m + 1, :] for m in range(6))
    r = _dot(xr, wr_ref[...])
    k = _dot(xk, wk_ref[...])
    v = _dot(xv, wv_ref[...])
    wl = w0_ref[...] + _dot(jnp.tanh(_dot(xw, w1_ref[...])), w2_ref[...])
    nwl = -wl
    w_log = -(jnp.maximum(nwl, 0.0) + jnp.log(1.0 + jnp.exp(-jnp.abs(nwl)))) - 0.5
    a = _sigmoid(a0_ref[...] + _dot(_dot(xa, a1_ref[...]), a2_ref[...]))
    gate = _dot(_sigmoid(_dot(xg, g1_ref[...])), g2_ref[...])
    hsum = hsum_ref[...]

    def head_sum(z):
        w = hsum.shape[0]
        return jnp.concatenate([_dot(z[:, s:s + w], hsum) for s in range(0, z.shape[1], w)], axis=1)

    kk = k * kk_ref[...]
    kk = kk / jnp.maximum(jnp.sqrt(head_sum(kk * kk)), 1e-12)
    k2 = k * (1.0 + (a - 1.0) * ka_ref[...])
    r_out[...] = r.astype(r_out.dtype)
    k_out[...] = k2.astype(k_out.dtype)
    v_out[...] = v.astype(v_out.dtype)
    kk_out[...] = kk.astype(kk_out.dtype)
    bb_out[...] = (kk * a).astype(bb_out.dtype)
    lw_out[...] = -jnp.exp(w_log)
    g_out[...] = gate.astype(g_out.dtype)
    bonus_out[...] = (head_sum(r * k2 * rk_ref[...]) * v).astype(bonus_out.dtype)


def _rwkv_proj(h, t, g, mix8, wr, wk, wv, w1, w2, a1, a2, g1, g2, w0, a0, k_k, k_a, r_k, hsum):
    n, d = h.shape
    tm = PROJ_TM
    full = lambda arr: pl.BlockSpec(arr.shape, lambda i: (0,) * arr.ndim)
    tile = pl.BlockSpec((tm, d), lambda i: (i, 0))
    prev = pl.BlockSpec((8, d), lambda i: (jnp.maximum(i * (tm // 8) - 1, 0), 0))
    consts = [g, mix8, wr, wk, wv, w1, w2, a1, a2, g1, g2, w0, a0, k_k, k_a, r_k, hsum]
    return pl.pallas_call(
        functools.partial(_rwkv_proj_kernel, tiles_per_seq=t // tm),
        grid=(n // tm,),
        in_specs=[tile, prev] + [full(c) for c in consts],
        out_specs=[tile] * 8,
        out_shape=[jax.ShapeDtypeStruct((n, d), F32 if idx == 5 else BF16) for idx in range(8)],
        compiler_params=_cparams(("arbitrary",)),
        name="rwkv_proj",
    )(h, h, *consts)


def _rwkv_rec_kernel(r_ref, k_ref, v_ref, kk_ref, bb_ref, lw_ref, bonus_ref, gate_ref,
                     lng_ref, lnb_ref, o_ref, st_ref):
    c, hd, w = REC_C, RWKV_HEAD, REC_W
    bsz, _, d = r_ref.shape
    units = [(b, u) for b in range(bsz) for u in range(d // w)]

    @pl.when(pl.program_id(0) == 0)
    def _():
        st_ref[...] = jnp.zeros_like(st_ref)

    rr, cc = _iota2((c, c))
    cum_m = jnp.where(cc <= rr, 1.0, 0.0).astype(BF16)
    lane_c = lax.broadcasted_iota(jnp.int32, (c, w), 1)
    row_c = lax.broadcasted_iota(jnp.int32, (c, w), 0)
    head_c = lane_c >> (hd.bit_length() - 1)
    colm = lane_c & (hd - 1)
    strict = colm < row_c
    incl = colm <= row_c
    eye_cat = jnp.where(colm == row_c, 1.0, 0.0)
    r2, c2 = _iota2((w, w))
    bd = _same_block(r2, c2, hd)
    diag = r2 == c2
    hmean = jnp.where(bd, 1.0 / hd, 0.0).astype(BF16)

    def stack(xm):
        return jnp.concatenate([jnp.where(head_c == h, xm, 0.0) for h in range(w // hd)], axis=0)

    prep = []
    for b in range(bsz):
        lw = lw_ref[b]
        lp = _dot_exact_lhs(cum_m, lw)
        pend = lp[c - 1:c, :]
        e_neg = jnp.exp(-lp)
        e_end = jnp.exp(pend - lp)
        kb, bbv = k_ref[b].astype(F32), bb_ref[b].astype(F32)
        prep.append(dict(rt=r_ref[b].astype(F32) * jnp.exp(lp), kt=kk_ref[b].astype(F32) * jnp.exp(lp - lw),
                         kh=kb * e_neg, bh=bbv * e_neg, khp=kb * e_end, bhp=bbv * e_end,
                         p_end=jnp.exp(pend), v=v_ref[b]))

    def usl(name, b, u):
        return prep[b][name][:, u * w:(u + 1) * w]

    m_kk, m_kb, a_rk, a_rb = [], [], [], []
    for (b, u) in units:
        lhs = jnp.concatenate([usl('kt', b, u), usl('rt', b, u)], axis=0)
        rhs = jnp.concatenate([stack(usl('kh', b, u)), stack(usl('bh', b, u))], axis=0)
        gram = _dot_nt(lhs, rhs)
        m_kk.append(jnp.where(strict, gram[0:c, 0:w], 0.0))
        m_kb.append(jnp.where(strict, gram[0:c, w:2 * w], 0.0))
        a_rk.append(jnp.where(incl, gram[c:2 * c, 0:w], 0.0))
        a_rb.append(jnp.where(incl, gram[c:2 * c, w:2 * w], 0.0))
    tinv = [eye_cat - m for m in m_kb]
    pw = [_dot(m, stack(m)) for m in m_kb]
    mv = [_dot(jnp.concatenate([m_kk[i], a_rk[i]], axis=0), stack(usl('v', b, u)))
          for i, (b, u) in enumerate(units)]
    n_lvl = (c - 1).bit_length() - 1
    for lvl in range(n_lvl):
        last = lvl == n_lvl - 1
        for i in range(len(units)):
            spw = stack(pw[i])
            if last:
                tinv[i] = tinv[i] + _dot(tinv[i], spw)
            else:
                res = _dot(jnp.concatenate([tinv[i], pw[i]], axis=0), spw)
                tinv[i] = tinv[i] + res[0:c]
                pw[i] = res[c:2 * c]
    kv = [_dot(tinv[i], jnp.concatenate([stack(usl('kt', b, u)), stack(mv[i][0:c])], axis=1))
          for i, (b, u) in enumerate(units)]
    rp, y0, phi, psi = [], [], [], []
    for i, (b, u) in enumerate(units):
        kp, vp = kv[i][:, 0:w], kv[i][:, w:2 * w]
        ab = _dot(a_rb[i], jnp.concatenate([stack(kp), stack(vp)], axis=1))
        rp.append(usl('rt', b, u) - ab[:, 0:w])
        y0.append(mv[i][c:2 * c] - ab[:, w:2 * w])
        vv = usl('v', b, u)
        xs = jnp.concatenate([usl('khp', b, u), -usl('bhp', b, u)], axis=0)
        rhs2 = jnp.concatenate([jnp.concatenate([vv, jnp.zeros_like(vv)], axis=1),
                                jnp.concatenate([vp, kp], axis=1)], axis=0)
        pp = _dot_tn(xs, rhs2)
        psi.append(jnp.where(bd, pp[:, 0:w], 0.0))
        phi.append(jnp.where(bd, pp[:, w:2 * w], 0.0) + jnp.where(diag, usl('p_end', b, u), 0.0))
    ys = []
    for i, (b, u) in enumerate(units):
        a_bd = st_ref[b, u]
        ys.append(_dot(rp[i], a_bd) + y0[i])
        st_ref[b, u] = _dot(phi[i], a_bd) + psi[i]
    y_all = jnp.concatenate(ys, axis=0)
    dlt = y_all - _dot(y_all, hmean)
    yn_all = dlt * lax.rsqrt(_dot(dlt * dlt, hmean) + GN_EPS)
    for i, (b, u) in enumerate(units):
        ls = slice(u * w, (u + 1) * w)
        yn = yn_all[i * c:(i + 1) * c] * lng_ref[:, ls] + lnb_ref[:, ls]
        o_ref[b, :, ls] = ((yn + bonus_ref[b, :, ls].astype(F32))
                           * gate_ref[b, :, ls].astype(F32)).astype(o_ref.dtype)


def _rwkv_rec(r, k, v, kk, bb, lw, bonus, gate, ln_g, ln_b):
    bsz, t, d = r.shape
    blk = pl.BlockSpec((bsz, REC_C, d), lambda i: (0, i, 0))
    vec = pl.BlockSpec((1, d), lambda i: (0, 0))
    return pl.pallas_call(
        _rwkv_rec_kernel,
        grid=(t // REC_C,),
        in_specs=[blk] * 8 + [vec, vec],
        out_specs=blk,
        out_shape=jax.ShapeDtypeStruct((bsz, t, d), BF16),
        scratch_shapes=[pltpu.VMEM((bsz, d // REC_W, REC_W, REC_W), F32)],
        compiler_params=_cparams(("arbitrary",)),
        name="rwkv_rec",
    )(r, k, v, kk, bb, lw, bonus, gate, ln_g, ln_b)


def kernel(x, norm_mix_g, norm_ffn_g, norm_out_g, mix_w_in, mix_w_out, gmlp_norm_g, gmlp_w_s, gmlp_b_s, hgrn_lb_logits, hgrn_onorm_g, ffn_w_gate, ffn_w_up, ffn_w_down, rwkv_mix, rwkv_w_r, rwkv_w_k, rwkv_w_v, rwkv_w_o, rwkv_w0, rwkv_w1, rwkv_w2, rwkv_a0, rwkv_a1, rwkv_a2, rwkv_g1, rwkv_g2, rwkv_k_k, rwkv_k_a, rwkv_r_k, rwkv_ln_g, rwkv_ln_b, moe_router, moe_router_b, moe_w_gate, moe_w_up, moe_w_down):
    bsz, t, d = x.shape
    n = bsz * t
    row = lambda vec: vec.reshape(1, -1).astype(F32)
    bf = lambda w: w.astype(BF16)

    lower_bounds = jnp.cumsum(jax.nn.softmax(hgrn_lb_logits.astype(F32), axis=0), axis=0)
    bias_b = jnp.repeat(gmlp_b_s[0].astype(F32).T, GMLP_DIM, axis=1)
    h = _mixer0(x, row(norm_mix_g[0]), bf(mix_w_in[0]), bf(mix_w_out[0]), row(gmlp_norm_g[0]),
                gmlp_w_s[0].astype(F32), bias_b, row(lower_bounds[0]), row(hgrn_onorm_g[0]))
    h = h.reshape(n, d)
    h = _ffn(h, row(norm_ffn_g[0]), bf(ffn_w_gate[0]), bf(ffn_w_up[0]), bf(ffn_w_down[0]))

    mix8 = jnp.concatenate([rwkv_mix[0].astype(F32), jnp.zeros((2, d), F32)], axis=0)
    head_id = jnp.arange(REC_W) // RWKV_HEAD
    hsum = (head_id[:, None] == head_id[None, :]).astype(BF16)
    r, k2, v, kk, bb, lw, gate, bonus = _rwkv_proj(
        h, t, row(norm_mix_g[1]), mix8, bf(rwkv_w_r[0]), bf(rwkv_w_k[0]), bf(rwkv_w_v[0]),
        bf(rwkv_w1[0]), bf(rwkv_w2[0]), bf(rwkv_a1[0]), bf(rwkv_a2[0]), bf(rwkv_g1[0]), bf(rwkv_g2[0]),
        row(rwkv_w0[0]), row(rwkv_a0[0]), row(rwkv_k_k[0]), row(rwkv_k_a[0]), row(rwkv_r_k[0]), hsum)
    sh = lambda z: z.reshape(bsz, t, d)
    yg = _rwkv_rec(sh(r), sh(k2), sh(v), sh(kk), sh(bb), sh(lw), sh(bonus), sh(gate),
                   row(rwkv_ln_g[0]), row(rwkv_ln_b[0]))
    router = jnp.zeros((d, LANES), F32).at[:, :N_EXPERTS].set(moe_router[0].astype(F32))
    router_b = jnp.full((1, LANES), -1e30, F32).at[0, :N_EXPERTS].set(moe_router_b[0].astype(F32))
    tok = jnp.arange(MOE_TT)
    tri = (tok[None, :] < tok[:, None]).astype(BF16)
    h, hn, posc, posr, counts = _moe_router(h, yg.reshape(n, d), bf(rwkv_w_o[0]), row(norm_ffn_g[1]),
                                            router, router_b, tri)
    count = counts[:, 0, :N_EXPERTS].astype(jnp.int32)
    seg = (count + (MOE_ALIGN - 1)) // MOE_ALIGN * MOE_ALIGN
    start = jnp.cumsum(seg, axis=1) - seg
    out = _moe_experts(start, count, h, hn, posc, posr, bf(moe_w_gate[0]), bf(moe_w_up[0]), bf(moe_w_down[0]),
                       row(norm_out_g))
    return out.reshape(bsz, t, d)
```

```python
import functools

import jax
import jax.numpy as jnp
from jax import lax
from jax.experimental import pallas as pl
from jax.experimental.pallas import tpu as pltpu

F32 = jnp.float32
BF16 = jnp.bfloat16

D_MODEL = 1024
GMLP_GROUPS = 4
GMLP_DIM = 128
GMLP_CHUNK = 128
HGRN_HEADS = 4
HGRN_DK = 128
HGRN_CHUNK = 32
MIX_HALF = 512
MIX_IN = 3072
RWKV_HEAD = 64
N_EXPERTS = 8
D_FF_EXPERT = 1408
RMS_EPS = 1e-6
LN_EPS = 1e-5
GN_EPS = 64e-5

LANES = 128
VMEM_LIMIT_BYTES = 56 * 1024 * 1024

MIX_TM = 256
MLP_TM = 512
MOE_TT = 1024
MOE_CH = 128
MOE_ALIGN = 16
MOE_BLK = 256
MOE_SORTED = -(-(2 * MOE_TT + N_EXPERTS * (MOE_ALIGN - 1) + MOE_CH - 1) // MOE_BLK) * MOE_BLK
PROJ_TM = 256
REC_C = 64
REC_HPU = 4
REC_W = REC_HPU * RWKV_HEAD


def _cparams(sem):
    return pltpu.CompilerParams(dimension_semantics=sem, vmem_limit_bytes=VMEM_LIMIT_BYTES)


def _dot(a, b):
    return jnp.dot(a.astype(BF16), b.astype(BF16), preferred_element_type=F32)


def _dot_nt(a, b):
    return lax.dot_general(a.astype(BF16), b.astype(BF16), (((1,), (1,)), ((), ())),
                           preferred_element_type=F32)


def _dot_tn(a, b):
    return lax.dot_general(a.astype(BF16), b.astype(BF16), (((0,), (0,)), ((), ())),
                           preferred_element_type=F32)


def _split(a):
    hi = a.astype(BF16)
    lo = (a - hi.astype(F32)).astype(BF16)
    return hi, lo


def _dot_exact_lhs(m_bf16, a):
    hi, lo = _split(a)
    return (jnp.dot(m_bf16, hi, preferred_element_type=F32)
            + jnp.dot(m_bf16, lo, preferred_element_type=F32))


def _dot3(a, b):
    a_hi, a_lo = _split(a)
    b_hi, b_lo = _split(b)
    return (jnp.dot(a_hi, b_hi, preferred_element_type=F32)
            + jnp.dot(a_hi, b_lo, preferred_element_type=F32)
            + jnp.dot(a_lo, b_hi, preferred_element_type=F32))


def _rms(x, g):
    return x * lax.rsqrt(jnp.mean(x * x, axis=-1, keepdims=True) + RMS_EPS) * g


def _sigmoid(x):
    return jax.nn.sigmoid(x)


def _silu(x):
    return x * jax.nn.sigmoid(x)


def _iota2(shape):
    return (lax.broadcasted_iota(jnp.int32, shape, 0), lax.broadcasted_iota(jnp.int32, shape, 1))


def _same_block(a, b, size):
    shift = size.bit_length() - 1
    assert 1 << shift == size
    return (a >> shift) == (b >> shift)


def _mixer0_kernel(x_ref, g_ref, win_ref, wout_ref, gng_ref, ws_ref, bias_ref, lb_ref, og_ref,
                   o_ref, st_ref):
    tm = MIX_TM

    @pl.when(pl.program_id(1) == 0)
    def _():
        st_ref[...] = jnp.zeros_like(st_ref)

    x = x_ref[0]
    z = _dot(_rms(x, g_ref[...]), win_ref[...])

    u = jax.nn.gelu(z[:, 0:MIX_HALF])
    v = jax.nn.gelu(z[:, MIX_HALF:2 * MIX_HALF])
    row, col = _iota2((GMLP_CHUNK, GMLP_CHUNK))
    tril = col <= row
    mixed_groups = []
    for g in range(GMLP_GROUPS):
        gs = slice(g * GMLP_DIM, (g + 1) * GMLP_DIM)
        vg = v[:, gs]
        mu = jnp.mean(vg, axis=-1, keepdims=True)
        d = vg - mu
        var = jnp.mean(d * d, axis=-1, keepdims=True)
        vn = (d * lax.rsqrt(var + LN_EPS) * gng_ref[:, gs]).astype(BF16)
        wg = jnp.where(tril, ws_ref[g], 0.0).astype(BF16)
        parts = [jnp.dot(wg, vn[c * GMLP_CHUNK:(c + 1) * GMLP_CHUNK], preferred_element_type=F32)
                 for c in range(tm // GMLP_CHUNK)]
        mixed_groups.append(jnp.concatenate(parts, axis=0))
    bias = jnp.concatenate([bias_ref[...]] * (tm // GMLP_CHUNK), axis=0)
    y_a = u * (jnp.concatenate(mixed_groups, axis=1) + bias)

    o0 = 2 * MIX_HALF
    zq = z[:, o0:o0 + MIX_HALF]
    zf = z[:, o0 + MIX_HALF:o0 + 2 * MIX_HALF]
    zi = z[:, o0 + 2 * MIX_HALF:o0 + 3 * MIX_HALF]
    zg = z[:, o0 + 3 * MIX_HALF:o0 + 4 * MIX_HALF]
    lb = lb_ref[...]
    q = _silu(zq)
    f = lb + (1.0 - lb) * _sigmoid(zf)
    k = 1.0 - f
    lf = jnp.log(f)
    rr, cc = _iota2((tm, tm))
    same = _same_block(rr, cc, HGRN_CHUNK)
    cum_m = jnp.where(same & (cc <= rr), 1.0, 0.0).astype(BF16)
    end_m = jnp.where(same, 1.0, 0.0).astype(BF16)
    lf_hi, lf_lo = _split(lf)
    b = (jnp.dot(cum_m, lf_hi, preferred_element_type=F32)
         + jnp.dot(cum_m, lf_lo, preferred_element_type=F32))
    b_end = (jnp.dot(end_m, lf_hi, preferred_element_type=F32)
             + jnp.dot(end_m, lf_lo, preferred_element_type=F32))
    q_dec = (q * jnp.exp(b)).astype(BF16)
    k_inv = (k * jnp.exp(-b)).astype(BF16)
    k_end = (k * jnp.exp(b_end - b)).astype(BF16)
    dec = jnp.exp(b_end)
    vb = zi.astype(BF16)

    r128, c128 = _iota2((128, 128))
    intra_mask = _same_block(r128, c128, HGRN_CHUNK) & (c128 <= r128)
    og = og_ref[...]
    o_heads = []
    for h in range(HGRN_HEADS):
        hs = slice(h * HGRN_DK, (h + 1) * HGRN_DK)
        intra = []
        for rb in range(tm // 128):
            rs = slice(rb * 128, (rb + 1) * 128)
            sc = _dot_nt(q_dec[rs, hs], k_inv[rs, hs])
            sc = jnp.where(intra_mask, sc, 0.0)
            intra.append(_dot(sc, vb[rs, hs]))
        o_intra = jnp.concatenate(intra, axis=0)
        st = st_ref[h]
        inter = []
        for c in range(tm // HGRN_CHUNK):
            cs = slice(c * HGRN_CHUNK, (c + 1) * HGRN_CHUNK)
            inter.append(_dot_nt(q_dec[cs, hs], st))
            inc_t = _dot_tn(vb[cs, hs], k_end[cs, hs])
            st = st * dec[c * HGRN_CHUNK:c * HGRN_CHUNK + 1, hs] + inc_t
        st_ref[h] = st
        o_h = o_intra + jnp.concatenate(inter, axis=0)
        o_h = o_h * lax.rsqrt(jnp.mean(o_h * o_h, axis=-1, keepdims=True) + RMS_EPS) * og
        o_heads.append(o_h * _silu(zg[:, hs]))

    y = jnp.concatenate([y_a] + o_heads, axis=1)
    o_ref[0] = x + _dot(y, wout_ref[...])


def _mixer0(x, g, w_in, w_out, gn_g, w_s, bias_b, lb, og):
    bsz, t, d = x.shape
    grid = (bsz, t // MIX_TM)
    const = lambda shape: pl.BlockSpec(shape, lambda b, i: (0,) * len(shape))
    return pl.pallas_call(
        _mixer0_kernel,
        grid=grid,
        in_specs=[
            pl.BlockSpec((1, MIX_TM, d), lambda b, i: (b, i, 0)),
            const((1, d)),
            const((d, MIX_IN)),
            const((2 * MIX_HALF, d)),
            const((1, MIX_HALF)),
            const((GMLP_GROUPS, GMLP_CHUNK, GMLP_CHUNK)),
            const((GMLP_CHUNK, MIX_HALF)),
            const((1, MIX_HALF)),
            const((1, HGRN_DK)),
        ],
        out_specs=pl.BlockSpec((1, MIX_TM, d), lambda b, i: (b, i, 0)),
        out_shape=jax.ShapeDtypeStruct((bsz, t, d), F32),
        scratch_shapes=[pltpu.VMEM((HGRN_HEADS, HGRN_DK, HGRN_DK), F32)],
        compiler_params=_cparams(("arbitrary", "arbitrary")),
        name="mixer0",
    )(x, g, w_in, w_out, gn_g, w_s, bias_b, lb, og)


def _swiglu(xb, wg, wu, wd):
    a = jnp.dot(xb, wg, preferred_element_type=F32)
    b = jnp.dot(xb, wu, preferred_element_type=F32)
    return _dot(_silu(a) * b, wd)


def _ffn_kernel(x_ref, g_ref, wg_ref, wu_ref, wd_ref, o_ref):
    x = x_ref[...]
    o_ref[...] = x + _swiglu(_rms(x, g_ref[...]).astype(BF16), wg_ref[...], wu_ref[...], wd_ref[...])


def _ffn(x, g, wg, wu, wd):
    n, d = x.shape
    tm = MLP_TM
    resident = lambda w: pl.BlockSpec(w.shape, lambda i: (0, 0), pipeline_mode=pl.Buffered(1))
    return pl.pallas_call(
        _ffn_kernel,
        grid=(n // tm,),
        in_specs=[pl.BlockSpec((tm, d), lambda i: (i, 0)),
                  pl.BlockSpec((1, d), lambda i: (0, 0)),
                  resident(wg), resident(wu), resident(wd)],
        out_specs=pl.BlockSpec((tm, d), lambda i: (i, 0)),
        out_shape=jax.ShapeDtypeStruct((n, d), F32),
        compiler_params=_cparams(("arbitrary",)),
        name="ffn_mlp",
    )(x, g, wg, wu, wd)


def _moe_router_kernel(x_ref, y_ref, wo_ref, g_ref, router_ref, rb_ref, tri_ref,
                       h_ref, hn_ref, posc_ref, posr_ref, cnt_ref):
    h = x_ref[...] + jnp.dot(y_ref[...], wo_ref[...], preferred_element_type=F32)
    h_ref[...] = h
    hn = _rms(h, g_ref[...])
    hn_ref[...] = hn.astype(BF16)
    logits = _dot3(hn, router_ref[...]) + rb_ref[...]
    lane = lax.broadcasted_iota(jnp.int32, logits.shape, 1)
    m1 = jnp.max(logits, axis=-1, keepdims=True)
    i1 = jnp.min(jnp.where(logits == m1, lane, LANES), axis=-1, keepdims=True)
    rest = jnp.where(lane == i1, -jnp.inf, logits)
    m2 = jnp.max(rest, axis=-1, keepdims=True)
    i2 = jnp.min(jnp.where(rest == m2, lane, LANES), axis=-1, keepdims=True)
    e2 = jnp.exp(m2 - m1)
    den = 1.0 + e2
    sel1, sel2 = lane == i1, lane == i2
    sel_f = jnp.where(sel1 | sel2, 1.0, 0.0)
    rank = jnp.dot(tri_ref[...], sel_f.astype(BF16), preferred_element_type=F32)
    cnt = jnp.sum(sel_f, axis=0, keepdims=True)
    seg = jnp.floor((cnt + (MOE_ALIGN - 1)) * (1.0 / MOE_ALIGN)) * MOE_ALIGN
    r2, c2 = _iota2((LANES, LANES))
    before = jnp.where(r2 < c2, 1.0, 0.0).astype(BF16)
    start = jnp.dot(jnp.broadcast_to(seg, (8, LANES)).astype(BF16), before,
                    preferred_element_type=F32)[0:1]
    pos = start + rank
    pos1 = jnp.sum(jnp.where(sel1, pos, 0.0), axis=-1, keepdims=True)
    pos2 = jnp.sum(jnp.where(sel2, pos, 0.0), axis=-1, keepdims=True)
    posc = (jnp.where(lane == 0, pos1, 0.0) + jnp.where(lane == 1, pos2, 0.0)
            + jnp.where(lane == 2, 1.0 / den, 0.0) + jnp.where(lane == 3, e2 / den, 0.0))
    posc_ref[...] = posc
    posr_ref[0] = posc.T[0:8]
    cnt_ref[0] = jnp.broadcast_to(cnt, cnt_ref.shape[1:])


def _moe_router(x, y, w_o, g, router, router_b, tri):
    n, d = x.shape
    tm = MOE_TT
    tile = lambda w: pl.BlockSpec((tm, w), lambda i: (i, 0))
    full = lambda arr: pl.BlockSpec(arr.shape, lambda i: (0,) * arr.ndim)
    small = lambda w: pl.BlockSpec((1, 8, w), lambda i: (i, 0, 0))
    return pl.pallas_call(
        _moe_router_kernel,
        grid=(n // tm,),
        in_specs=[tile(d), tile(d), full(w_o), full(g), full(router), full(router_b), full(tri)],
        out_specs=[tile(d), tile(d), tile(LANES), small(tm), small(LANES)],
        out_shape=[jax.ShapeDtypeStruct((n, d), F32), jax.ShapeDtypeStruct((n, d), BF16),
                   jax.ShapeDtypeStruct((n, LANES), F32), jax.ShapeDtypeStruct((n // tm, 8, tm), F32),
                   jax.ShapeDtypeStruct((n // tm, 8, LANES), F32)],
        compiler_params=_cparams(("arbitrary",)),
        name="moe_router",
    )(x, y, w_o, g, router, router_b, tri)


def _moe_expert_kernel(start_ref, cnt_ref, x_ref, hn_ref, posc_ref, posr_ref, wg_ref, wu_ref, wd_ref, gout_ref,
                       o_ref, xs_ref, ys_ref, *, n_e):
    i, e = pl.program_id(0), pl.program_id(1)
    tt, ch, blk = MOE_TT, MOE_CH, MOE_BLK
    n_sorted = xs_ref.shape[0]

    @pl.when(e == 0)
    def _():
        pos1, pos2 = posr_ref[0, 0:1, :], posr_ref[0, 1:2, :]
        hn = hn_ref[...]
        for rb in range(n_sorted // blk):
            srow = (lax.broadcasted_iota(jnp.int32, (blk, tt), 0) + rb * blk).astype(F32)
            p = jnp.where((srow == pos1) | (srow == pos2), 1.0, 0.0).astype(BF16)
            xs_ref[rb * blk:(rb + 1) * blk, :] = jnp.dot(p, hn, preferred_element_type=F32).astype(BF16)
        ys_ref[...] = jnp.zeros_like(ys_ref)

    start, count = start_ref[i, e], cnt_ref[i, e]
    row_in_chunk = lax.broadcasted_iota(jnp.int32, (ch, 1), 0)

    def chunk(k, carry):
        r0 = pl.multiple_of(start + k * ch, MOE_ALIGN)
        y = _swiglu(xs_ref[pl.ds(r0, ch), :], wg_ref[0], wu_ref[0], wd_ref[0])
        ys_ref[pl.ds(r0, ch), :] = jnp.where(row_in_chunk < count - k * ch, y, 0.0).astype(BF16)
        return carry

    lax.fori_loop(0, (count + (ch - 1)) // ch, chunk, 0)

    @pl.when(e == n_e - 1)
    def _():
        ys = ys_ref[...]
        for tb in range(tt // blk):
            ts = slice(tb * blk, (tb + 1) * blk)
            pc = posc_ref[ts, :]
            scol = lax.broadcasted_iota(jnp.int32, (blk, n_sorted), 1).astype(F32)
            pw = (jnp.where(scol == pc[:, 0:1], pc[:, 2:3], 0.0)
                  + jnp.where(scol == pc[:, 1:2], pc[:, 3:4], 0.0)).astype(BF16)
            moe = jnp.dot(pw, ys, preferred_element_type=F32)
            o_ref[ts, :] = _rms(x_ref[ts, :] + moe, gout_ref[...])


def _moe_experts(start, count, x, hn, posc, posr, wg, wu, wd, g_out):
    n, d = x.shape
    n_e, _, dff = wg.shape
    tt = MOE_TT
    tile = lambda w: pl.BlockSpec((tt, w), lambda i, e, *_: (i, 0))
    grid_spec = pltpu.PrefetchScalarGridSpec(
        num_scalar_prefetch=2,
        grid=(n // tt, n_e),
        in_specs=[tile(d), tile(d), tile(LANES), pl.BlockSpec((1, 8, tt), lambda i, e, *_: (i, 0, 0)),
                  pl.BlockSpec((1, d, dff), lambda i, e, *_: (e, 0, 0)),
                  pl.BlockSpec((1, d, dff), lambda i, e, *_: (e, 0, 0)),
                  pl.BlockSpec((1, dff, d), lambda i, e, *_: (e, 0, 0)),
                  pl.BlockSpec((1, d), lambda i, e, *_: (0, 0))],
        out_specs=tile(d),
        scratch_shapes=[pltpu.VMEM((MOE_SORTED, d), BF16), pltpu.VMEM((MOE_SORTED, d), BF16)],
    )
    return pl.pallas_call(
        functools.partial(_moe_expert_kernel, n_e=n_e),
        grid_spec=grid_spec,
        out_shape=jax.ShapeDtypeStruct((n, d), F32),
        compiler_params=_cparams(("arbitrary", "arbitrary")),
        name="moe_experts",
    )(start, count, x, hn, posc, posr, wg, wu, wd, g_out)


def _rwkv_proj_kernel(h_ref, hp_ref, g_ref, mix_ref, wr_ref, wk_ref, wv_ref, w1_ref, w2_ref,
                      a1_ref, a2_ref, g1_ref, g2_ref, w0_ref, a0_ref, kk_ref, ka_ref, rk_ref,
                      hsum_ref,
                      r_out, k_out, v_out, kk_out, bb_out, lw_out, g_out, bonus_out,
                      *, tiles_per_seq):
    i = pl.program_id(0)
    g = g_ref[...]
    hn = _rms(h_ref[...], g)
    prev = _rms(hp_ref[7:8, :], g)
    prev = jnp.where(i % tiles_per_seq == 0, 0.0, prev)
    row = lax.broadcasted_iota(jnp.int32, hn.shape, 0)
    shifted = jnp.where(row == 0, prev, pltpu.roll(hn, 1, axis=0))
    dx = shifted - hn
    xr, xw, xk, xv, xa, xg = (hn + dx * mix_ref[m:m + 1, :] for m in range(6))
    r = _dot(xr, wr_ref[...])
    k = _dot(xk, wk_ref[...])
    v = _dot(xv, wv_ref[...])
    wl = w0_ref[...] + _dot(jnp.tanh(_dot(xw, w1_ref[...])), w2_ref[...])
    nwl = -wl
    w_log = -(jnp.maximum(nwl, 0.0) + jnp.log(1.0 + jnp.exp(-jnp.abs(nwl)))) - 0.5
    a = _sigmoid(a0_ref[...] + _dot(_dot(xa, a1_ref[...]), a2_ref[...]))
    gate = _dot(_sigmoid(_dot(xg, g1_ref[...])), g2_ref[...])
    hsum = hsum_ref[...]

    def head_sum(z):
        w = hsum.shape[0]
        return jnp.concatenate([_dot(z[:, s:s + w], hsum) for s in range(0, z.shape[1], w)], axis=1)

    kk = k * kk_ref[...]
    kk = kk / jnp.maximum(jnp.sqrt(head_sum(kk * kk)), 1e-12)
    k2 = k * (1.0 + (a - 1.0) * ka_ref[...])
    r_out[...] = r.astype(r_out.dtype)
    k_out[...] = k2.astype(k_out.dtype)
    v_out[...] = v.astype(v_out.dtype)
    kk_out[...] = kk.astype(kk_out.dtype)
    bb_out[...] = (kk * a).astype(bb_out.dtype)
    lw_out[...] = -jnp.exp(w_log)
    g_out[...] = gate.astype(g_out.dtype)
    bonus_out[...] = (head_sum(r * k2 * rk_ref[...]) * v).astype(bonus_out.dtype)


def _rwkv_proj(h, t, g, mix8, wr, wk, wv, w1, w2, a1, a2, g1, g2, w0, a0, k_k, k_a, r_k, hsum):
    n, d = h.shape
    tm = PROJ_TM
    full = lambda arr: pl.BlockSpec(arr.shape, lambda i: (0,) * arr.ndim)
    tile = pl.BlockSpec((tm, d), lambda i: (i, 0))
    prev = pl.BlockSpec((8, d), lambda i: (jnp.maximum(i * (tm // 8) - 1, 0), 0))
    consts = [g, mix8, wr, wk, wv, w1, w2, a1, a2, g1, g2, w0, a0, k_k, k_a, r_k, hsum]
    return pl.pallas_call(
        functools.partial(_rwkv_proj_kernel, tiles_per_seq=t // tm),
        grid=(n // tm,),
        in_specs=[tile, prev] + [full(c) for c in consts],
        out_specs=[tile] * 8,
        out_shape=[jax.ShapeDtypeStruct((n, d), F32 if idx == 5 else BF16) for idx in range(8)],
        compiler_params=_cparams(("arbitrary",)),
        name="rwkv_proj",
    )(h, h, *consts)


def _rwkv_rec_kernel(r_ref, k_ref, v_ref, kk_ref, bb_ref, lw_ref, bonus_ref, gate_ref,
                     lng_ref, lnb_ref, o_ref, st_ref):
    c, hd, w = REC_C, RWKV_HEAD, REC_W
    bsz, _, d = r_ref.shape
    units = [(b, u) for b in range(bsz) for u in range(d // w)]

    @pl.when(pl.program_id(0) == 0)
    def _():
        st_ref[...] = jnp.zeros_like(st_ref)

    rr, cc = _iota2((c, c))
    cum_m = jnp.where(cc <= rr, 1.0, 0.0).astype(BF16)
    lane_c = lax.broadcasted_iota(jnp.int32, (c, w), 1)
    row_c = lax.broadcasted_iota(jnp.int32, (c, w), 0)
    head_c = lane_c >> (hd.bit_length() - 1)
    colm = lane_c & (hd - 1)
    strict = colm < row_c
    incl = colm <= row_c
    eye_cat = jnp.where(colm == row_c, 1.0, 0.0)
    r2, c2 = _iota2((w, w))
    bd = _same_block(r2, c2, hd)
    hmean = jnp.where(bd, 1.0 / hd, 0.0).astype(BF16)

    def stack(xm):
        return jnp.concatenate([jnp.where(head_c == h, xm, 0.0) for h in range(w // hd)], axis=0)

    prep = []
    for b in range(bsz):
        lw = lw_ref[b]
        lp = _dot_exact_lhs(cum_m, lw)
        pend = lp[c - 1:c, :]
        e_neg = jnp.exp(-lp)
        e_end = jnp.exp(pend - lp)
        kb, bbv = k_ref[b].astype(F32), bb_ref[b].astype(F32)
        prep.append(dict(rt=r_ref[b].astype(F32) * jnp.exp(lp), kt=kk_ref[b].astype(F32) * jnp.exp(lp - lw),
                         kh=kb * e_neg, bh=bbv * e_neg, khp=kb * e_end, bhp=bbv * e_end,
                         p_end=jnp.exp(pend), v=v_ref[b]))

    def usl(name, b, u):
        return prep[b][name][:, u * w:(u + 1) * w]

    m_kk, m_kb, a_rk, a_rb = [], [], [], []
    for (b, u) in units:
        lhs = jnp.concatenate([usl('kt', b, u), usl('rt', b, u)], axis=0)
        rhs = jnp.concatenate([stack(usl('kh', b, u)), stack(usl('bh', b, u))], axis=0)
        gram = _dot_nt(lhs, rhs)
        m_kk.append(jnp.where(strict, gram[0:c, 0:w], 0.0))
        m_kb.append(jnp.where(strict, gram[0:c, w:2 * w], 0.0))
        a_rk.append(jnp.where(incl, gram[c:2 * c, 0:w], 0.0))
        a_rb.append(jnp.where(incl, gram[c:2 * c, w:2 * w], 0.0))
    tinv = [eye_cat - m for m in m_kb]
    pw = [_dot(m, stack(m)) for m in m_kb]
    mv = [_dot(jnp.concatenate([m_kk[i], a_rk[i]], axis=0), stack(usl('v', b, u)))
          for i, (b, u) in enumerate(units)]
    n_lvl = (c - 1).bit_length() - 1
    for lvl in range(n_lvl):
        last = lvl == n_lvl - 1
        for i in range(len(units)):
            spw = stack(pw[i])
            if last:
                tinv[i] = tinv[i] + _dot(tinv[i], spw)
            else:
                res = _dot(jnp.concatenate([tinv[i], pw[i]], axis=0), spw)
                tinv[i] = tinv[i] + res[0:c]
                pw[i] = res[c:2 * c]
    ka = [_dot_nt(jnp.concatenate([usl('kt', b, u), usl('rt', b, u)], axis=0), st_ref[b, u])
          for (b, u) in units]
    uu = [_dot(tinv[i], stack(ka[i][0:c] + mv[i][0:c])) for i in range(len(units))]
    ys = []
    for i, (b, u) in enumerate(units):
        ys.append(ka[i][c:2 * c] + mv[i][c:2 * c] - _dot(a_rb[i], stack(uu[i])))
        inc = _dot_tn(jnp.concatenate([usl('v', b, u).astype(F32), uu[i]], axis=0),
                      jnp.concatenate([usl('khp', b, u), -usl('bhp', b, u)], axis=0))
        st_ref[b, u] = st_ref[b, u] * usl('p_end', b, u) + jnp.where(bd, inc, 0.0)
    y_all = jnp.concatenate(ys, axis=0)
    dlt = y_all - _dot(y_all, hmean)
    yn_all = dlt * lax.rsqrt(_dot(dlt * dlt, hmean) + GN_EPS)
    for i, (b, u) in enumerate(units):
        ls = slice(u * w, (u + 1) * w)
        yn = yn_all[i * c:(i + 1) * c] * lng_ref[:, ls] + lnb_ref[:, ls]
        o_ref[b, :, ls] = ((yn + bonus_ref[b, :, ls].astype(F32))
                           * gate_ref[b, :, ls].astype(F32)).astype(o_ref.dtype)


def _rwkv_rec(r, k, v, kk, bb, lw, bonus, gate, ln_g, ln_b):
    bsz, t, d = r.shape
    blk = pl.BlockSpec((bsz, REC_C, d), lambda i: (0, i, 0))
    vec = pl.BlockSpec((1, d), lambda i: (0, 0))
    return pl.pallas_call(
        _rwkv_rec_kernel,
        grid=(t // REC_C,),
        in_specs=[blk] * 8 + [vec, vec],
        out_specs=blk,
        out_shape=jax.ShapeDtypeStruct((bsz, t, d), BF16),
        scratch_shapes=[pltpu.VMEM((bsz, d // REC_W, REC_W, REC_W), F32)],
        compiler_params=_cparams(("arbitrary",)),
        name="rwkv_rec",
    )(r, k, v, kk, bb, lw, bonus, gate, ln_g, ln_b)


def kernel(x, norm_mix_g, norm_ffn_g, norm_out_g, mix_w_in, mix_w_out, gmlp_norm_g, gmlp_w_s, gmlp_b_s, hgrn_lb_logits, hgrn_onorm_g, ffn_w_gate, ffn_w_up, ffn_w_down, rwkv_mix, rwkv_w_r, rwkv_w_k, rwkv_w_v, rwkv_w_o, rwkv_w0, rwkv_w1, rwkv_w2, rwkv_a0, rwkv_a1, rwkv_a2, rwkv_g1, rwkv_g2, rwkv_k_k, rwkv_k_a, rwkv_r_k, rwkv_ln_g, rwkv_ln_b, moe_router, moe_router_b, moe_w_gate, moe_w_up, moe_w_down):
    bsz, t, d = x.shape
    n = bsz * t
    row = lambda vec: vec.reshape(1, -1).astype(F32)
    bf = lambda w: w.astype(BF16)

    lower_bounds = jnp.cumsum(jax.nn.softmax(hgrn_lb_logits.astype(F32), axis=0), axis=0)
    bias_b = jnp.repeat(gmlp_b_s[0].astype(F32).T, GMLP_DIM, axis=1)
    h = _mixer0(x, row(norm_mix_g[0]), bf(mix_w_in[0]), bf(mix_w_out[0]), row(gmlp_norm_g[0]),
                gmlp_w_s[0].astype(F32), bias_b, row(lower_bounds[0]), row(hgrn_onorm_g[0]))
    h = h.reshape(n, d)
    h = _ffn(h, row(norm_ffn_g[0]), bf(ffn_w_gate[0]), bf(ffn_w_up[0]), bf(ffn_w_down[0]))

    mix8 = jnp.concatenate([rwkv_mix[0].astype(F32), jnp.zeros((2, d), F32)], axis=0)
    head_id = jnp.arange(REC_W) // RWKV_HEAD
    hsum = (head_id[:, None] == head_id[None, :]).astype(BF16)
    r, k2, v, kk, bb, lw, gate, bonus = _rwkv_proj(
        h, t, row(norm_mix_g[1]), mix8, bf(rwkv_w_r[0]), bf(rwkv_w_k[0]), bf(rwkv_w_v[0]),
        bf(rwkv_w1[0]), bf(rwkv_w2[0]), bf(rwkv_a1[0]), bf(rwkv_a2[0]), bf(rwkv_g1[0]), bf(rwkv_g2[0]),
        row(rwkv_w0[0]), row(rwkv_a0[0]), row(rwkv_k_k[0]), row(rwkv_k_a[0]), row(rwkv_r_k[0]), hsum)
    sh = lambda z: z.reshape(bsz, t, d)
    yg = _rwkv_rec(sh(r), sh(k2), sh(v), sh(kk), sh(bb), sh(lw), sh(bonus), sh(gate),
                   row(rwkv_ln_g[0]), row(rwkv_ln_b[0]))
    router = jnp.zeros((d, LANES), F32).at[:, :N_EXPERTS].set(moe_router[0].astype(F32))
    router_b = jnp.full((1, LANES), -1e30, F32).at[0, :N_EXPERTS].set(moe_router_b[0].astype(F32))
    tok = jnp.arange(MOE_TT)
    tri = (tok[None, :] < tok[:, None]).astype(BF16)
    h, hn, posc, posr, counts = _moe_router(h, yg.reshape(n, d), bf(rwkv_w_o[0]), row(norm_ffn_g[1]),
                                            router, router_b, tri)
    count = counts[:, 0, :N_EXPERTS].astype(jnp.int32)
    seg = (count + (MOE_ALIGN - 1)) // MOE_ALIGN * MOE_ALIGN
    start = jnp.cumsum(seg, axis=1) - seg
    out = _moe_experts(start, count, h, hn, posc, posr, bf(moe_w_gate[0]), bf(moe_w_up[0]), bf(moe_w_down[0]),
                       row(norm_out_g))
    return out.reshape(bsz, t, d)
```

```python
import functools

import jax
import jax.numpy as jnp
from jax import lax
from jax.experimental import pallas as pl
from jax.experimental.pallas import tpu as pltpu

F32 = jnp.float32
BF16 = jnp.bfloat16

D_MODEL = 1024
GMLP_GROUPS = 4
GMLP_DIM = 128
GMLP_CHUNK = 128
HGRN_HEADS = 4
HGRN_DK = 128
HGRN_CHUNK = 32
MIX_HALF = 512
MIX_IN = 3072
RWKV_HEAD = 64
N_EXPERTS = 8
D_FF_EXPERT = 1408
RMS_EPS = 1e-6
LN_EPS = 1e-5
GN_EPS = 64e-5

LANES = 128
VMEM_LIMIT_BYTES = 56 * 1024 * 1024

MIX_TM = 256
MIX_CUM_BLK = 256
MLP_TM = 512
MOE_TT = 1024
MOE_CH = 128
MOE_ALIGN = 16
MOE_BLK = 256
MOE_SORTED = -(-(2 * MOE_TT + N_EXPERTS * (MOE_ALIGN - 1) + MOE_CH - 1) // MOE_BLK) * MOE_BLK
PROJ_TM = 512
REC_C = 64
REC_HPU = 4
REC_W = REC_HPU * RWKV_HEAD


def _cparams(sem):
    return pltpu.CompilerParams(dimension_semantics=sem, vmem_limit_bytes=VMEM_LIMIT_BYTES)


def _dot(a, b):
    return jnp.dot(a.astype(BF16), b.astype(BF16), preferred_element_type=F32)


def _dot_nt(a, b):
    return lax.dot_general(a.astype(BF16), b.astype(BF16), (((1,), (1,)), ((), ())),
                           preferred_element_type=F32)


def _dot_tn(a, b):
    return lax.dot_general(a.astype(BF16), b.astype(BF16), (((0,), (0,)), ((), ())),
                           preferred_element_type=F32)


def _split(a):
    hi = a.astype(BF16)
    lo = (a - hi.astype(F32)).astype(BF16)
    return hi, lo


def _dot_exact_lhs(m_bf16, a):
    hi, lo = _split(a)
    return (jnp.dot(m_bf16, hi, preferred_element_type=F32)
            + jnp.dot(m_bf16, lo, preferred_element_type=F32))


def _dot3(a, b):
    a_hi, a_lo = _split(a)
    b_hi, b_lo = _split(b)
    return (jnp.dot(a_hi, b_hi, preferred_element_type=F32)
            + jnp.dot(a_hi, b_lo, preferred_element_type=F32)
            + jnp.dot(a_lo, b_hi, preferred_element_type=F32))


def _rms(x, g):
    return x * lax.rsqrt(jnp.mean(x * x, axis=-1, keepdims=True) + RMS_EPS) * g


def _sigmoid(x):
    return jax.nn.sigmoid(x)


def _silu(x):
    return x * jax.nn.sigmoid(x)


def _iota2(shape):
    return (lax.broadcasted_iota(jnp.int32, shape, 0), lax.broadcasted_iota(jnp.int32, shape, 1))


def _same_block(a, b, size):
    shift = size.bit_length() - 1
    assert 1 << shift == size
    return (a >> shift) == (b >> shift)


def _mixer0_kernel(x_ref, g_ref, win_ref, wout_ref, gng_ref, ws_ref, bias_ref, lb_ref, og_ref,
                   o_ref, st_ref):
    tm = MIX_TM

    @pl.when(pl.program_id(1) == 0)
    def _():
        st_ref[...] = jnp.zeros_like(st_ref)

    x = x_ref[0]
    z = _dot(_rms(x, g_ref[...]), win_ref[...])

    u = jax.nn.gelu(z[:, 0:MIX_HALF])
    v = jax.nn.gelu(z[:, MIX_HALF:2 * MIX_HALF])
    row, col = _iota2((GMLP_CHUNK, GMLP_CHUNK))
    tril = col <= row
    mixed_groups = []
    for g in range(GMLP_GROUPS):
        gs = slice(g * GMLP_DIM, (g + 1) * GMLP_DIM)
        vg = v[:, gs]
        mu = jnp.mean(vg, axis=-1, keepdims=True)
        d = vg - mu
        var = jnp.mean(d * d, axis=-1, keepdims=True)
        vn = (d * lax.rsqrt(var + LN_EPS) * gng_ref[:, gs]).astype(BF16)
        wg = jnp.where(tril, ws_ref[g], 0.0).astype(BF16)
        parts = [jnp.dot(wg, vn[c * GMLP_CHUNK:(c + 1) * GMLP_CHUNK], preferred_element_type=F32)
                 for c in range(tm // GMLP_CHUNK)]
        mixed_groups.append(jnp.concatenate(parts, axis=0))
    bias = jnp.concatenate([bias_ref[...]] * (tm // GMLP_CHUNK), axis=0)
    y_a = u * (jnp.concatenate(mixed_groups, axis=1) + bias)

    o0 = 2 * MIX_HALF
    zq = z[:, o0:o0 + MIX_HALF]
    zf = z[:, o0 + MIX_HALF:o0 + 2 * MIX_HALF]
    zi = z[:, o0 + 2 * MIX_HALF:o0 + 3 * MIX_HALF]
    zg = z[:, o0 + 3 * MIX_HALF:o0 + 4 * MIX_HALF]
    lb = lb_ref[...]
    q = _silu(zq)
    f = lb + (1.0 - lb) * _sigmoid(zf)
    k = 1.0 - f
    lf = jnp.log(f)
    cb = MIX_CUM_BLK
    rr, cc = _iota2((cb, cb))
    same = _same_block(rr, cc, HGRN_CHUNK)
    cum_m = jnp.where(same & (cc <= rr), 1.0, 0.0).astype(BF16)
    end_m = jnp.where(same, 1.0, 0.0).astype(BF16)
    b = jnp.concatenate([_dot_exact_lhs(cum_m, lf[s:s + cb]) for s in range(0, tm, cb)], axis=0)
    b_end = jnp.concatenate([_dot_exact_lhs(end_m, lf[s:s + cb]) for s in range(0, tm, cb)], axis=0)
    q_dec = (q * jnp.exp(b)).astype(BF16)
    k_inv = (k * jnp.exp(-b)).astype(BF16)
    k_end = (k * jnp.exp(b_end - b)).astype(BF16)
    dec = jnp.exp(b_end)
    vb = zi.astype(BF16)

    r128, c128 = _iota2((128, 128))
    intra_mask = _same_block(r128, c128, HGRN_CHUNK) & (c128 <= r128)
    og = og_ref[...]
    heads = [slice(h * HGRN_DK, (h + 1) * HGRN_DK) for h in range(HGRN_HEADS)]
    chunks = [slice(c * HGRN_CHUNK, (c + 1) * HGRN_CHUNK) for c in range(tm // HGRN_CHUNK)]
    inc_t = [[_dot_tn(vb[cs, hs], k_end[cs, hs]) for cs in chunks] for hs in heads]
    scores = [[jnp.where(intra_mask, _dot_nt(q_dec[rs, hs], k_inv[rs, hs]), 0.0)
               for rs in (slice(rb * 128, (rb + 1) * 128) for rb in range(tm // 128))] for hs in heads]
    st_in = []
    for h, hs in enumerate(heads):
        st, per_chunk = st_ref[h], []
        for c, cs in enumerate(chunks):
            per_chunk.append(st)
            st = st * dec[cs.start:cs.start + 1, hs] + inc_t[h][c]
        st_ref[h] = st
        st_in.append(per_chunk)
    o_heads = []
    for h, hs in enumerate(heads):
        o_intra = jnp.concatenate([_dot(sc, vb[rb * 128:(rb + 1) * 128, hs])
                                   for rb, sc in enumerate(scores[h])], axis=0)
        o_inter = jnp.concatenate([_dot_nt(q_dec[cs, hs], st_in[h][c]) for c, cs in enumerate(chunks)], axis=0)
        o_h = o_intra + o_inter
        o_h = o_h * lax.rsqrt(jnp.mean(o_h * o_h, axis=-1, keepdims=True) + RMS_EPS) * og
        o_heads.append(o_h * _silu(zg[:, hs]))

    y = jnp.concatenate([y_a] + o_heads, axis=1)
    o_ref[0] = x + _dot(y, wout_ref[...])


def _mixer0(x, g, w_in, w_out, gn_g, w_s, bias_b, lb, og):
    bsz, t, d = x.shape
    grid = (bsz, t // MIX_TM)
    const = lambda shape: pl.BlockSpec(shape, lambda b, i: (0,) * len(shape))
    return pl.pallas_call(
        _mixer0_kernel,
        grid=grid,
        in_specs=[
            pl.BlockSpec((1, MIX_TM, d), lambda b, i: (b, i, 0)),
            const((1, d)),
            const((d, MIX_IN)),
            const((2 * MIX_HALF, d)),
            const((1, MIX_HALF)),
            const((GMLP_GROUPS, GMLP_CHUNK, GMLP_CHUNK)),
            const((GMLP_CHUNK, MIX_HALF)),
            const((1, MIX_HALF)),
            const((1, HGRN_DK)),
        ],
        out_specs=pl.BlockSpec((1, MIX_TM, d), lambda b, i: (b, i, 0)),
        out_shape=jax.ShapeDtypeStruct((bsz, t, d), F32),
        scratch_shapes=[pltpu.VMEM((HGRN_HEADS, HGRN_DK, HGRN_DK), F32)],
        compiler_params=_cparams(("arbitrary", "arbitrary")),
        name="mixer0",
    )(x, g, w_in, w_out, gn_g, w_s, bias_b, lb, og)


def _swiglu(xb, wg, wu, wd):
    a = jnp.dot(xb, wg, preferred_element_type=F32)
    b = jnp.dot(xb, wu, preferred_element_type=F32)
    return _dot(_silu(a) * b, wd)


def _ffn_kernel(x_ref, g_ref, wg_ref, wu_ref, wd_ref, o_ref):
    x = x_ref[...]
    o_ref[...] = x + _swiglu(_rms(x, g_ref[...]).astype(BF16), wg_ref[...], wu_ref[...], wd_ref[...])


def _ffn(x, g, wg, wu, wd):
    n, d = x.shape
    tm = MLP_TM
    resident = lambda w: pl.BlockSpec(w.shape, lambda i: (0, 0), pipeline_mode=pl.Buffered(1))
    return pl.pallas_call(
        _ffn_kernel,
        grid=(n // tm,),
        in_specs=[pl.BlockSpec((tm, d), lambda i: (i, 0)),
                  pl.BlockSpec((1, d), lambda i: (0, 0)),
                  resident(wg), resident(wu), resident(wd)],
        out_specs=pl.BlockSpec((tm, d), lambda i: (i, 0)),
        out_shape=jax.ShapeDtypeStruct((n, d), F32),
        compiler_params=_cparams(("arbitrary",)),
        name="ffn_mlp",
    )(x, g, wg, wu, wd)


def _moe_router_kernel(x_ref, y_ref, wo_ref, g_ref, router_ref, rb_ref, tri_ref,
                       h_ref, hn_ref, posc_ref, posr_ref, cnt_ref):
    h = x_ref[...] + jnp.dot(y_ref[...], wo_ref[...], preferred_element_type=F32)
    h_ref[...] = h
    hn = _rms(h, g_ref[...])
    hn_ref[...] = hn.astype(BF16)
    logits = _dot3(hn, router_ref[...]) + rb_ref[...]
    lane = lax.broadcasted_iota(jnp.int32, logits.shape, 1)
    m1 = jnp.max(logits, axis=-1, keepdims=True)
    i1 = jnp.min(jnp.where(logits == m1, lane, LANES), axis=-1, keepdims=True)
    rest = jnp.where(lane == i1, -jnp.inf, logits)
    m2 = jnp.max(rest, axis=-1, keepdims=True)
    i2 = jnp.min(jnp.where(rest == m2, lane, LANES), axis=-1, keepdims=True)
    e2 = jnp.exp(m2 - m1)
    den = 1.0 + e2
    sel1, sel2 = lane == i1, lane == i2
    sel_f = jnp.where(sel1 | sel2, 1.0, 0.0)
    rank = jnp.dot(tri_ref[...], sel_f.astype(BF16), preferred_element_type=F32)
    cnt = jnp.sum(sel_f, axis=0, keepdims=True)
    seg = jnp.floor((cnt + (MOE_ALIGN - 1)) * (1.0 / MOE_ALIGN)) * MOE_ALIGN
    r2, c2 = _iota2((LANES, LANES))
    before = jnp.where(r2 < c2, 1.0, 0.0).astype(BF16)
    start = jnp.dot(jnp.broadcast_to(seg, (8, LANES)).astype(BF16), before,
                    preferred_element_type=F32)[0:1]
    pos = start + rank
    pos1 = jnp.sum(jnp.where(sel1, pos, 0.0), axis=-1, keepdims=True)
    pos2 = jnp.sum(jnp.where(sel2, pos, 0.0), axis=-1, keepdims=True)
    posc = (jnp.where(lane == 0, pos1, 0.0) + jnp.where(lane == 1, pos2, 0.0)
            + jnp.where(lane == 2, 1.0 / den, 0.0) + jnp.where(lane == 3, e2 / den, 0.0))
    posc_ref[...] = posc
    posr_ref[0] = posc.T[0:8]
    cnt_ref[0] = jnp.broadcast_to(cnt, cnt_ref.shape[1:])


def _moe_router(x, y, w_o, g, router, router_b, tri):
    n, d = x.shape
    tm = MOE_TT
    tile = lambda w: pl.BlockSpec((tm, w), lambda i: (i, 0))
    full = lambda arr: pl.BlockSpec(arr.shape, lambda i: (0,) * arr.ndim)
    small = lambda w: pl.BlockSpec((1, 8, w), lambda i: (i, 0, 0))
    return pl.pallas_call(
        _moe_router_kernel,
        grid=(n // tm,),
        in_specs=[tile(d), tile(d), full(w_o), full(g), full(router), full(router_b), full(tri)],
        out_specs=[tile(d), tile(d), tile(LANES), small(tm), small(LANES)],
        out_shape=[jax.ShapeDtypeStruct((n, d), F32), jax.ShapeDtypeStruct((n, d), BF16),
                   jax.ShapeDtypeStruct((n, LANES), F32), jax.ShapeDtypeStruct((n // tm, 8, tm), F32),
                   jax.ShapeDtypeStruct((n // tm, 8, LANES), F32)],
        compiler_params=_cparams(("arbitrary",)),
        name="moe_router",
    )(x, y, w_o, g, router, router_b, tri)


def _moe_expert_kernel(start_ref, cnt_ref, x_ref, hn_ref, posc_ref, posr_ref, wg_ref, wu_ref, wd_ref, gout_ref,
                       o_ref, xs_ref, ys_ref, *, n_e):
    i, e = pl.program_id(0), pl.program_id(1)
    tt, ch, blk = MOE_TT, MOE_CH, MOE_BLK
    n_sorted = xs_ref.shape[0]

    @pl.when(e == 0)
    def _():
        pos1, pos2 = posr_ref[0, 0:1, :], posr_ref[0, 1:2, :]
        hn = hn_ref[...]
        for rb in range(n_sorted // blk):
            srow = (lax.broadcasted_iota(jnp.int32, (blk, tt), 0) + rb * blk).astype(F32)
            p = jnp.where((srow == pos1) | (srow == pos2), 1.0, 0.0).astype(BF16)
            xs_ref[rb * blk:(rb + 1) * blk, :] = jnp.dot(p, hn, preferred_element_type=F32).astype(BF16)
        ys_ref[...] = jnp.zeros_like(ys_ref)

    start, count = start_ref[i, e], cnt_ref[i, e]
    row_in_chunk = lax.broadcasted_iota(jnp.int32, (ch, 1), 0)

    def chunk(k, carry):
        r0 = pl.multiple_of(start + k * ch, MOE_ALIGN)
        y = _swiglu(xs_ref[pl.ds(r0, ch), :], wg_ref[0], wu_ref[0], wd_ref[0])
        ys_ref[pl.ds(r0, ch), :] = jnp.where(row_in_chunk < count - k * ch, y, 0.0).astype(BF16)
        return carry

    lax.fori_loop(0, (count + (ch - 1)) // ch, chunk, 0)

    @pl.when(e == n_e - 1)
    def _():
        ys = ys_ref[...]
        for tb in range(tt // blk):
            ts = slice(tb * blk, (tb + 1) * blk)
            pc = posc_ref[ts, :]
            scol = lax.broadcasted_iota(jnp.int32, (blk, n_sorted), 1).astype(F32)
            pw = (jnp.where(scol == pc[:, 0:1], pc[:, 2:3], 0.0)
                  + jnp.where(scol == pc[:, 1:2], pc[:, 3:4], 0.0)).astype(BF16)
            moe = jnp.dot(pw, ys, preferred_element_type=F32)
            o_ref[ts, :] = _rms(x_ref[ts, :] + moe, gout_ref[...])


def _moe_experts(start, count, x, hn, posc, posr, wg, wu, wd, g_out):
    n, d = x.shape
    n_e, _, dff = wg.shape
    tt = MOE_TT
    tile = lambda w: pl.BlockSpec((tt, w), lambda i, e, *_: (i, 0))
    grid_spec = pltpu.PrefetchScalarGridSpec(
        num_scalar_prefetch=2,
        grid=(n // tt, n_e),
        in_specs=[tile(d), tile(d), tile(LANES), pl.BlockSpec((1, 8, tt), lambda i, e, *_: (i, 0, 0)),
                  pl.BlockSpec((1, d, dff), lambda i, e, *_: (e, 0, 0)),
                  pl.BlockSpec((1, d, dff), lambda i, e, *_: (e, 0, 0)),
                  pl.BlockSpec((1, dff, d), lambda i, e, *_: (e, 0, 0)),
                  pl.BlockSpec((1, d), lambda i, e, *_: (0, 0))],
        out_specs=tile(d),
        scratch_shapes=[pltpu.VMEM((MOE_SORTED, d), BF16), pltpu.VMEM((MOE_SORTED, d), BF16)],
    )
    return pl.pallas_call(
        functools.partial(_moe_expert_kernel, n_e=n_e),
        grid_spec=grid_spec,
        out_shape=jax.ShapeDtypeStruct((n, d), F32),
        compiler_params=_cparams(("arbitrary", "arbitrary")),
        name="moe_experts",
    )(start, count, x, hn, posc, posr, wg, wu, wd, g_out)


def _rwkv_proj_kernel(h_ref, hp_ref, g_ref, mix_ref, wr_ref, wk_ref, wv_ref, w1_ref, w2_ref,
                      a1_ref, a2_ref, g1_ref, g2_ref, w0_ref, a0_ref, kk_ref, ka_ref, rk_ref,
                      hsum_ref,
                      r_out, k_out, v_out, kk_out, bb_out, lw_out, g_out, bonus_out,
                      *, tiles_per_seq):
    i = pl.program_id(0)
    g = g_ref[...]
    hn = _rms(h_ref[...], g)
    prev = _rms(hp_ref[7:8, :], g)
    prev = jnp.where(i % tiles_per_seq == 0, 0.0, prev)
    row = lax.broadcasted_iota(jnp.int32, hn.shape, 0)
    shifted = jnp.where(row == 0, prev, pltpu.roll(hn, 1, axis=0))
    dx = shifted - hn
    xr, xw, xk, xv, xa, xg = (hn + dx * mix_ref[m:m + 1, :] for m in range(6))
    r = _dot(xr, wr_ref[...])
    k = _dot(xk, wk_ref[...])
    v = _dot(xv, wv_ref[...])
    wl = w0_ref[...] + _dot(jnp.tanh(_dot(xw, w1_ref[...])), w2_ref[...])
    nwl = -wl
    w_log = -(jnp.maximum(nwl, 0.0) + jnp.log(1.0 + jnp.exp(-jnp.abs(nwl)))) - 0.5
    a = _sigmoid(a0_ref[...] + _dot(_dot(xa, a1_ref[...]), a2_ref[...]))
    gate = _dot(_sigmoid(_dot(xg, g1_ref[...])), g2_ref[...])
    hsum = hsum_ref[...]

    def head_sum(z):
        w = hsum.shape[0]
        return jnp.concatenate([_dot(z[:, s:s + w], hsum) for s in range(0, z.shape[1], w)], axis=1)

    kk = k * kk_ref[...]
    kk = kk / jnp.maximum(jnp.sqrt(head_sum(kk * kk)), 1e-12)
    k2 = k * (1.0 + (a - 1.0) * ka_ref[...])
    r_out[...] = r.astype(r_out.dtype)
    k_out[...] = k2.astype(k_out.dtype)
    v_out[...] = v.astype(v_out.dtype)
    kk_out[...] = kk.astype(kk_out.dtype)
    bb_out[...] = (kk * a).astype(bb_out.dtype)
    lw_out[...] = -jnp.exp(w_log)
    g_out[...] = gate.astype(g_out.dtype)
    bonus_out[...] = (head_sum(r * k2 * rk_ref[...]) * v).astype(bonus_out.dtype)


def _rwkv_proj(h, t, g, mix8, wr, wk, wv, w1, w2, a1, a2, g1, g2, w0, a0, k_k, k_a, r_k, hsum):
    n, d = h.shape
    tm = PROJ_TM
    full = lambda arr: pl.BlockSpec(arr.shape, lambda i: (0,) * arr.ndim)
    tile = pl.BlockSpec((tm, d), lambda i: (i, 0))
    prev = pl.BlockSpec((8, d), lambda i: (jnp.maximum(i * (tm // 8) - 1, 0), 0))
    consts = [g, mix8, wr, wk, wv, w1, w2, a1, a2, g1, g2, w0, a0, k_k, k_a, r_k, hsum]
    return pl.pallas_call(
        functools.partial(_rwkv_proj_kernel, tiles_per_seq=t // tm),
        grid=(n // tm,),
        in_specs=[tile, prev] + [full(c) for c in consts],
        out_specs=[tile] * 8,
        out_shape=[jax.ShapeDtypeStruct((n, d), F32 if idx == 5 else BF16) for idx in range(8)],
        compiler_params=_cparams(("arbitrary",)),
        name="rwkv_proj",
    )(h, h, *consts)


def _rwkv_rec_kernel(r_ref, k_ref, v_ref, kk_ref, bb_ref, lw_ref, bonus_ref, gate_ref,
                     lng_ref, lnb_ref, o_ref, st_ref):
    c, hd, w = REC_C, RWKV_HEAD, REC_W
    bsz, _, d = r_ref.shape
    units = [(b, u) for b in range(bsz) for u in range(d // w)]

    @pl.when(pl.program_id(0) == 0)
    def _():
        st_ref[...] = jnp.zeros_like(st_ref)

    rr, cc = _iota2((c, c))
    cum_m = jnp.where(cc <= rr, 1.0, 0.0).astype(BF16)
    lane_c = lax.broadcasted_iota(jnp.int32, (c, w), 1)
    row_c = lax.broadcasted_iota(jnp.int32, (c, w), 0)
    head_c = lane_c >> (hd.bit_length() - 1)
    colm = lane_c & (hd - 1)
    strict = colm < row_c
    incl = colm <= row_c
    eye_cat = jnp.where(colm == row_c, 1.0, 0.0)
    r2, c2 = _iota2((w, w))
    bd = _same_block(r2, c2, hd)
    hmean = jnp.where(bd, 1.0 / hd, 0.0).astype(BF16)

    def stack(xm):
        return jnp.concatenate([jnp.where(head_c == h, xm, 0.0) for h in range(w // hd)], axis=0)

    prep = []
    for b in range(bsz):
        lw = lw_ref[b]
        lp = _dot_exact_lhs(cum_m, lw)
        pend = lp[c - 1:c, :]
        e_neg = jnp.exp(-lp)
        e_end = jnp.exp(pend - lp)
        kb, bbv = k_ref[b].astype(F32), bb_ref[b].astype(F32)
        prep.append(dict(rt=r_ref[b].astype(F32) * jnp.exp(lp), kt=kk_ref[b].astype(F32) * jnp.exp(lp - lw),
                         kh=kb * e_neg, bh=bbv * e_neg, khp=kb * e_end, bhp=bbv * e_end,
                         p_end=jnp.exp(pend), v=v_ref[b]))

    def usl(name, b, u):
        return prep[b][name][:, u * w:(u + 1) * w]

    m_kk, m_kb, a_rk, a_rb = [], [], [], []
    for (b, u) in units:
        lhs = jnp.concatenate([usl('kt', b, u), usl('rt', b, u)], axis=0)
        rhs = jnp.concatenate([stack(usl('kh', b, u)), stack(usl('bh', b, u))], axis=0)
        gram = _dot_nt(lhs, rhs)
        m_kk.append(jnp.where(strict, gram[0:c, 0:w], 0.0))
        m_kb.append(jnp.where(strict, gram[0:c, w:2 * w], 0.0))
        a_rk.append(jnp.where(incl, gram[c:2 * c, 0:w], 0.0))
        a_rb.append(jnp.where(incl, gram[c:2 * c, w:2 * w], 0.0))
    tinv = [eye_cat - m for m in m_kb]
    pw = [_dot(m, stack(m)) for m in m_kb]
    mv = [_dot(jnp.concatenate([m_kk[i], a_rk[i]], axis=0), stack(usl('v', b, u)))
          for i, (b, u) in enumerate(units)]
    n_lvl = (c - 1).bit_length() - 1
    for lvl in range(n_lvl):
        last = lvl == n_lvl - 1
        for i in range(len(units)):
            spw = stack(pw[i])
            if last:
                tinv[i] = tinv[i] + _dot(tinv[i], spw)
            else:
                res = _dot(jnp.concatenate([tinv[i], pw[i]], axis=0), spw)
                tinv[i] = tinv[i] + res[0:c]
                pw[i] = res[c:2 * c]
    ka = [_dot_nt(jnp.concatenate([usl('kt', b, u), usl('rt', b, u)], axis=0), st_ref[b, u])
          for (b, u) in units]
    uu = [_dot(tinv[i], stack(ka[i][0:c] + mv[i][0:c])) for i in range(len(units))]
    ys = []
    for i, (b, u) in enumerate(units):
        ys.append(ka[i][c:2 * c] + mv[i][c:2 * c] - _dot(a_rb[i], stack(uu[i])))
        inc = _dot_tn(jnp.concatenate([usl('v', b, u).astype(F32), uu[i]], axis=0),
                      jnp.concatenate([usl('khp', b, u), -usl('bhp', b, u)], axis=0))
        st_ref[b, u] = st_ref[b, u] * usl('p_end', b, u) + jnp.where(bd, inc, 0.0)
    y_all = jnp.concatenate(ys, axis=0)
    dlt = y_all - _dot(y_all, hmean)
    yn_all = dlt * lax.rsqrt(_dot(dlt * dlt, hmean) + GN_EPS)
    for i, (b, u) in enumerate(units):
        ls = slice(u * w, (u + 1) * w)
        yn = yn_all[i * c:(i + 1) * c] * lng_ref[:, ls] + lnb_ref[:, ls]
        o_ref[b, :, ls] = ((yn + bonus_ref[b, :, ls].astype(F32))
                           * gate_ref[b, :, ls].astype(F32)).astype(o_ref.dtype)


def _rwkv_rec(r, k, v, kk, bb, lw, bonus, gate, ln_g, ln_b):
    bsz, t, d = r.shape
    blk = pl.BlockSpec((bsz, REC_C, d), lambda i: (0, i, 0))
    vec = pl.BlockSpec((1, d), lambda i: (0, 0))
    return pl.pallas_call(
        _rwkv_rec_kernel,
        grid=(t // REC_C,),
        in_specs=[blk] * 8 + [vec, vec],
        out_specs=blk,
        out_shape=jax.ShapeDtypeStruct((bsz, t, d), BF16),
        scratch_shapes=[pltpu.VMEM((bsz, d // REC_W, REC_W, REC_W), F32)],
        compiler_params=_cparams(("arbitrary",)),
        name="rwkv_rec",
    )(r, k, v, kk, bb, lw, bonus, gate, ln_g, ln_b)


def kernel(x, norm_mix_g, norm_ffn_g, norm_out_g, mix_w_in, mix_w_out, gmlp_norm_g, gmlp_w_s, gmlp_b_s, hgrn_lb_logits, hgrn_onorm_g, ffn_w_gate, ffn_w_up, ffn_w_down, rwkv_mix, rwkv_w_r, rwkv_w_k, rwkv_w_v, rwkv_w_o, rwkv_w0, rwkv_w1, rwkv_w2, rwkv_a0, rwkv_a1, rwkv_a2, rwkv_g1, rwkv_g2, rwkv_k_k, rwkv_k_a, rwkv_r_k, rwkv_ln_g, rwkv_ln_b, moe_router, moe_router_b, moe_w_gate, moe_w_up, moe_w_down):
    bsz, t, d = x.shape
    n = bsz * t
    row = lambda vec: vec.reshape(1, -1).astype(F32)
    bf = lambda w: w.astype(BF16)

    lower_bounds = jnp.cumsum(jax.nn.softmax(hgrn_lb_logits.astype(F32), axis=0), axis=0)
    bias_b = jnp.repeat(gmlp_b_s[0].astype(F32).T, GMLP_DIM, axis=1)
    h = _mixer0(x, row(norm_mix_g[0]), bf(mix_w_in[0]), bf(mix_w_out[0]), row(gmlp_norm_g[0]),
                gmlp_w_s[0].astype(F32), bias_b, row(lower_bounds[0]), row(hgrn_onorm_g[0]))
    h = h.reshape(n, d)
    h = _ffn(h, row(norm_ffn_g[0]), bf(ffn_w_gate[0]), bf(ffn_w_up[0]), bf(ffn_w_down[0]))

    mix8 = jnp.concatenate([rwkv_mix[0].astype(F32), jnp.zeros((2, d), F32)], axis=0)
    head_id = jnp.arange(REC_W) // RWKV_HEAD
    hsum = (head_id[:, None] == head_id[None, :]).astype(BF16)
    r, k2, v, kk, bb, lw, gate, bonus = _rwkv_proj(
        h, t, row(norm_mix_g[1]), mix8, bf(rwkv_w_r[0]), bf(rwkv_w_k[0]), bf(rwkv_w_v[0]),
        bf(rwkv_w1[0]), bf(rwkv_w2[0]), bf(rwkv_a1[0]), bf(rwkv_a2[0]), bf(rwkv_g1[0]), bf(rwkv_g2[0]),
        row(rwkv_w0[0]), row(rwkv_a0[0]), row(rwkv_k_k[0]), row(rwkv_k_a[0]), row(rwkv_r_k[0]), hsum)
    sh = lambda z: z.reshape(bsz, t, d)
    yg = _rwkv_rec(sh(r), sh(k2), sh(v), sh(kk), sh(bb), sh(lw), sh(bonus), sh(gate),
                   row(rwkv_ln_g[0]), row(rwkv_ln_b[0]))
    router = jnp.zeros((d, LANES), F32).at[:, :N_EXPERTS].set(moe_router[0].astype(F32))
    router_b = jnp.full((1, LANES), -1e30, F32).at[0, :N_EXPERTS].set(moe_router_b[0].astype(F32))
    tok = jnp.arange(MOE_TT)
    tri = (tok[None, :] < tok[:, None]).astype(BF16)
    h, hn, posc, posr, counts = _moe_router(h, yg.reshape(n, d), bf(rwkv_w_o[0]), row(norm_ffn_g[1]),
                                            router, router_b, tri)
    count = counts[:, 0, :N_EXPERTS].astype(jnp.int32)
    seg = (count + (MOE_ALIGN - 1)) // MOE_ALIGN * MOE_ALIGN
    start = jnp.cumsum(seg, axis=1) - seg
    out = _moe_experts(start, count, h, hn, posc, posr, bf(moe_w_gate[0]), bf(moe_w_up[0]), bf(moe_w_down[0]),
                       row(norm_out_g))
    return out.reshape(bsz, t, d)
```

```python
import functools

import jax
import jax.numpy as jnp
from jax import lax
from jax.experimental import pallas as pl
from jax.experimental.pallas import tpu as pltpu

F32 = jnp.float32
BF16 = jnp.bfloat16

D_MODEL = 1024
GMLP_GROUPS = 4
GMLP_DIM = 128
GMLP_CHUNK = 128
HGRN_HEADS = 4
HGRN_DK = 128
HGRN_CHUNK = 32
MIX_HALF = 512
MIX_IN = 3072
RWKV_HEAD = 64
N_EXPERTS = 8
D_FF_EXPERT = 1408
RMS_EPS = 1e-6
LN_EPS = 1e-5
GN_EPS = 64e-5

LANES = 128
VMEM_LIMIT_BYTES = 56 * 1024 * 1024

MIX_TM = 256
MIX_CUM_BLK = 256
MLP_TM = 512
MOE_TT = 1024
MOE_CH = 128
MOE_ALIGN = 16
MOE_ROWS = 16
MOE_BLK = 256
MOE_SORTED = -(-(2 * MOE_TT + N_EXPERTS * (MOE_ALIGN - 1) + MOE_CH - 1) // MOE_BLK) * MOE_BLK
PROJ_TM = 512
REC_C = 64
REC_HPU = 4
REC_W = REC_HPU * RWKV_HEAD


def _cparams(sem):
    return pltpu.CompilerParams(dimension_semantics=sem, vmem_limit_bytes=VMEM_LIMIT_BYTES)


def _dot(a, b):
    return jnp.dot(a.astype(BF16), b.astype(BF16), preferred_element_type=F32)


def _dot_nt(a, b):
    return lax.dot_general(a.astype(BF16), b.astype(BF16), (((1,), (1,)), ((), ())),
                           preferred_element_type=F32)


def _dot_tn(a, b):
    return lax.dot_general(a.astype(BF16), b.astype(BF16), (((0,), (0,)), ((), ())),
                           preferred_element_type=F32)


def _split(a):
    hi = a.astype(BF16)
    lo = (a - hi.astype(F32)).astype(BF16)
    return hi, lo


def _dot_exact_lhs(m_bf16, a):
    hi, lo = _split(a)
    return (jnp.dot(m_bf16, hi, preferred_element_type=F32)
            + jnp.dot(m_bf16, lo, preferred_element_type=F32))


def _dot3(a, b):
    a_hi, a_lo = _split(a)
    b_hi, b_lo = _split(b)
    return (jnp.dot(a_hi, b_hi, preferred_element_type=F32)
            + jnp.dot(a_hi, b_lo, preferred_element_type=F32)
            + jnp.dot(a_lo, b_hi, preferred_element_type=F32))


def _rms(x, g):
    return x * lax.rsqrt(jnp.mean(x * x, axis=-1, keepdims=True) + RMS_EPS) * g


def _sigmoid(x):
    return jax.nn.sigmoid(x)


def _silu(x):
    return x * jax.nn.sigmoid(x)


def _iota2(shape):
    return (lax.broadcasted_iota(jnp.int32, shape, 0), lax.broadcasted_iota(jnp.int32, shape, 1))


def _same_block(a, b, size):
    shift = size.bit_length() - 1
    assert 1 << shift == size
    return (a >> shift) == (b >> shift)


def _mixer0_kernel(x_ref, g_ref, win_ref, wout_ref, gng_ref, ws_ref, bias_ref, lb_ref, og_ref,
                   o_ref, st_ref):
    tm = MIX_TM

    @pl.when(pl.program_id(1) == 0)
    def _():
        st_ref[...] = jnp.zeros_like(st_ref)

    x = x_ref[0]
    z = _dot(_rms(x, g_ref[...]), win_ref[...])

    u = jax.nn.gelu(z[:, 0:MIX_HALF])
    v = jax.nn.gelu(z[:, MIX_HALF:2 * MIX_HALF])
    row, col = _iota2((GMLP_CHUNK, GMLP_CHUNK))
    tril = col <= row
    mixed_groups = []
    for g in range(GMLP_GROUPS):
        gs = slice(g * GMLP_DIM, (g + 1) * GMLP_DIM)
        vg = v[:, gs]
        mu = jnp.mean(vg, axis=-1, keepdims=True)
        d = vg - mu
        var = jnp.mean(d * d, axis=-1, keepdims=True)
        vn = (d * lax.rsqrt(var + LN_EPS) * gng_ref[:, gs]).astype(BF16)
        wg = jnp.where(tril, ws_ref[g], 0.0).astype(BF16)
        parts = [jnp.dot(wg, vn[c * GMLP_CHUNK:(c + 1) * GMLP_CHUNK], preferred_element_type=F32)
                 for c in range(tm // GMLP_CHUNK)]
        mixed_groups.append(jnp.concatenate(parts, axis=0))
    bias = jnp.concatenate([bias_ref[...]] * (tm // GMLP_CHUNK), axis=0)
    y_a = u * (jnp.concatenate(mixed_groups, axis=1) + bias)

    o0 = 2 * MIX_HALF
    zq = z[:, o0:o0 + MIX_HALF]
    zf = z[:, o0 + MIX_HALF:o0 + 2 * MIX_HALF]
    zi = z[:, o0 + 2 * MIX_HALF:o0 + 3 * MIX_HALF]
    zg = z[:, o0 + 3 * MIX_HALF:o0 + 4 * MIX_HALF]
    lb = lb_ref[...]
    q = _silu(zq)
    f = lb + (1.0 - lb) * _sigmoid(zf)
    k = 1.0 - f
    lf = jnp.log(f)
    cb = MIX_CUM_BLK
    rr, cc = _iota2((cb, cb))
    same = _same_block(rr, cc, HGRN_CHUNK)
    cum_m = jnp.where(same & (cc <= rr), 1.0, 0.0).astype(BF16)
    end_m = jnp.where(same, 1.0, 0.0).astype(BF16)
    b = jnp.concatenate([_dot_exact_lhs(cum_m, lf[s:s + cb]) for s in range(0, tm, cb)], axis=0)
    b_end = jnp.concatenate([_dot_exact_lhs(end_m, lf[s:s + cb]) for s in range(0, tm, cb)], axis=0)
    q_dec = (q * jnp.exp(b)).astype(BF16)
    k_inv = (k * jnp.exp(-b)).astype(BF16)
    k_end = (k * jnp.exp(b_end - b)).astype(BF16)
    dec = jnp.exp(b_end)
    vb = zi.astype(BF16)

    r128, c128 = _iota2((128, 128))
    intra_mask = _same_block(r128, c128, HGRN_CHUNK) & (c128 <= r128)
    og = og_ref[...]
    heads = [slice(h * HGRN_DK, (h + 1) * HGRN_DK) for h in range(HGRN_HEADS)]
    chunks = [slice(c * HGRN_CHUNK, (c + 1) * HGRN_CHUNK) for c in range(tm // HGRN_CHUNK)]
    inc_t = [[_dot_tn(vb[cs, hs], k_end[cs, hs]) for cs in chunks] for hs in heads]
    scores = [[jnp.where(intra_mask, _dot_nt(q_dec[rs, hs], k_inv[rs, hs]), 0.0)
               for rs in (slice(rb * 128, (rb + 1) * 128) for rb in range(tm // 128))] for hs in heads]
    st_in = []
    for h, hs in enumerate(heads):
        st, per_chunk = st_ref[h], []
        for c, cs in enumerate(chunks):
            per_chunk.append(st)
            st = st * dec[cs.start:cs.start + 1, hs] + inc_t[h][c]
        st_ref[h] = st
        st_in.append(per_chunk)
    o_heads = []
    for h, hs in enumerate(heads):
        o_intra = jnp.concatenate([_dot(sc, vb[rb * 128:(rb + 1) * 128, hs])
                                   for rb, sc in enumerate(scores[h])], axis=0)
        o_inter = jnp.concatenate([_dot_nt(q_dec[cs, hs], st_in[h][c]) for c, cs in enumerate(chunks)], axis=0)
        o_h = o_intra + o_inter
        o_h = o_h * lax.rsqrt(jnp.mean(o_h * o_h, axis=-1, keepdims=True) + RMS_EPS) * og
        o_heads.append(o_h * _silu(zg[:, hs]))

    y = jnp.concatenate([y_a] + o_heads, axis=1)
    o_ref[0] = x + _dot(y, wout_ref[...])


def _mixer0(x, g, w_in, w_out, gn_g, w_s, bias_b, lb, og):
    bsz, t, d = x.shape
    grid = (bsz, t // MIX_TM)
    const = lambda shape: pl.BlockSpec(shape, lambda b, i: (0,) * len(shape))
    return pl.pallas_call(
        _mixer0_kernel,
        grid=grid,
        in_specs=[
            pl.BlockSpec((1, MIX_TM, d), lambda b, i: (b, i, 0)),
            const((1, d)),
            const((d, MIX_IN)),
            const((2 * MIX_HALF, d)),
            const((1, MIX_HALF)),
            const((GMLP_GROUPS, GMLP_CHUNK, GMLP_CHUNK)),
            const((GMLP_CHUNK, MIX_HALF)),
            const((1, MIX_HALF)),
            const((1, HGRN_DK)),
        ],
        out_specs=pl.BlockSpec((1, MIX_TM, d), lambda b, i: (b, i, 0)),
        out_shape=jax.ShapeDtypeStruct((bsz, t, d), F32),
        scratch_shapes=[pltpu.VMEM((HGRN_HEADS, HGRN_DK, HGRN_DK), F32)],
        compiler_params=_cparams(("arbitrary", "arbitrary")),
        name="mixer0",
    )(x, g, w_in, w_out, gn_g, w_s, bias_b, lb, og)


def _swiglu(xb, wg, wu, wd):
    a = jnp.dot(xb, wg, preferred_element_type=F32)
    b = jnp.dot(xb, wu, preferred_element_type=F32)
    return _dot(_silu(a) * b, wd)


def _ffn_kernel(x_ref, g_ref, wg_ref, wu_ref, wd_ref, o_ref):
    x = x_ref[...]
    o_ref[...] = x + _swiglu(_rms(x, g_ref[...]).astype(BF16), wg_ref[...], wu_ref[...], wd_ref[...])


def _ffn(x, g, wg, wu, wd):
    n, d = x.shape
    tm = MLP_TM
    resident = lambda w: pl.BlockSpec(w.shape, lambda i: (0, 0), pipeline_mode=pl.Buffered(1))
    return pl.pallas_call(
        _ffn_kernel,
        grid=(n // tm,),
        in_specs=[pl.BlockSpec((tm, d), lambda i: (i, 0)),
                  pl.BlockSpec((1, d), lambda i: (0, 0)),
                  resident(wg), resident(wu), resident(wd)],
        out_specs=pl.BlockSpec((tm, d), lambda i: (i, 0)),
        out_shape=jax.ShapeDtypeStruct((n, d), F32),
        compiler_params=_cparams(("arbitrary",)),
        name="ffn_mlp",
    )(x, g, wg, wu, wd)


def _moe_router_kernel(x_ref, y_ref, wo_ref, g_ref, router_ref, rb_ref, tri_ref,
                       h_ref, hn_ref, posc_ref, posr_ref, cnt_ref):
    h = x_ref[...] + jnp.dot(y_ref[...], wo_ref[...], preferred_element_type=F32)
    h_ref[...] = h
    hn = _rms(h, g_ref[...])
    hn_ref[...] = hn.astype(BF16)
    hn_hi, hn_lo = _split(hn)
    r_hi, r_lo = _split(router_ref[...])
    nt = (((1,), (1,)), ((), ()))
    logits = (lax.dot_general(r_hi, hn_hi, nt, preferred_element_type=F32)
              + lax.dot_general(r_hi, hn_lo, nt, preferred_element_type=F32)
              + lax.dot_general(r_lo, hn_hi, nt, preferred_element_type=F32)) + rb_ref[...]
    n_rows = logits.shape[0]
    sub = lax.broadcasted_iota(jnp.int32, logits.shape, 0)
    m1 = jnp.max(logits, axis=0, keepdims=True)
    i1 = jnp.min(jnp.where(logits == m1, sub, n_rows), axis=0, keepdims=True)
    rest = jnp.where(sub == i1, -jnp.inf, logits)
    m2 = jnp.max(rest, axis=0, keepdims=True)
    i2 = jnp.min(jnp.where(rest == m2, sub, n_rows), axis=0, keepdims=True)
    e2 = jnp.exp(m2 - m1)
    den = 1.0 + e2
    sel1, sel2 = sub == i1, sub == i2
    sel_f = jnp.where(sel1 | sel2, 1.0, 0.0)
    rank = jnp.dot(sel_f.astype(BF16), tri_ref[...], preferred_element_type=F32)
    cnt = jnp.sum(sel_f, axis=1, keepdims=True)
    seg = jnp.floor((cnt + (MOE_ALIGN - 1)) * (1.0 / MOE_ALIGN)) * MOE_ALIGN
    starts, run = [], jnp.zeros((1, 1), F32)
    for e in range(n_rows):
        starts.append(run)
        run = run + seg[e:e + 1]
    pos = jnp.concatenate(starts, axis=0) + rank
    pos1 = jnp.sum(jnp.where(sel1, pos, 0.0), axis=0, keepdims=True)
    pos2 = jnp.sum(jnp.where(sel2, pos, 0.0), axis=0, keepdims=True)
    info = jnp.concatenate([pos1, pos2, 1.0 / den, e2 / den], axis=0)
    posr_ref[0] = jnp.concatenate([info, jnp.zeros((posr_ref.shape[1] - 4, info.shape[1]), F32)], axis=0)
    posc_ref[...] = jnp.concatenate([info, jnp.zeros((LANES - 4, info.shape[1]), F32)], axis=0).T
    cnt_ref[0] = jnp.broadcast_to(cnt[0:cnt_ref.shape[1]], cnt_ref.shape[1:])


def _moe_router(x, y, w_o, g, router, router_b, tri):
    n, d = x.shape
    tm = MOE_TT
    tile = lambda w: pl.BlockSpec((tm, w), lambda i: (i, 0))
    full = lambda arr: pl.BlockSpec(arr.shape, lambda i: (0,) * arr.ndim)
    small = lambda w: pl.BlockSpec((1, 8, w), lambda i: (i, 0, 0))
    return pl.pallas_call(
        _moe_router_kernel,
        grid=(n // tm,),
        in_specs=[tile(d), tile(d), full(w_o), full(g), full(router), full(router_b), full(tri)],
        out_specs=[tile(d), tile(d), tile(LANES), small(tm), small(LANES)],
        out_shape=[jax.ShapeDtypeStruct((n, d), F32), jax.ShapeDtypeStruct((n, d), BF16),
                   jax.ShapeDtypeStruct((n, LANES), F32), jax.ShapeDtypeStruct((n // tm, 8, tm), F32),
                   jax.ShapeDtypeStruct((n // tm, 8, LANES), F32)],
        compiler_params=_cparams(("arbitrary",)),
        name="moe_router",
    )(x, y, w_o, g, router, router_b, tri)


def _moe_expert_kernel(start_ref, cnt_ref, x_ref, hn_ref, posc_ref, posr_ref, wg_ref, wu_ref, wd_ref, gout_ref,
                       o_ref, xs_ref, ys_ref, *, n_e):
    i, e = pl.program_id(0), pl.program_id(1)
    tt, ch, blk = MOE_TT, MOE_CH, MOE_BLK
    n_sorted = xs_ref.shape[0]

    @pl.when(e == 0)
    def _():
        pos1, pos2 = posr_ref[0, 0:1, :], posr_ref[0, 1:2, :]
        hn = hn_ref[...]
        for rb in range(n_sorted // blk):
            srow = (lax.broadcasted_iota(jnp.int32, (blk, tt), 0) + rb * blk).astype(F32)
            p = jnp.where((srow == pos1) | (srow == pos2), 1.0, 0.0).astype(BF16)
            xs_ref[rb * blk:(rb + 1) * blk, :] = jnp.dot(p, hn, preferred_element_type=F32).astype(BF16)
        ys_ref[...] = jnp.zeros_like(ys_ref)

    start, count = start_ref[i, e], cnt_ref[i, e]
    row_in_chunk = lax.broadcasted_iota(jnp.int32, (ch, 1), 0)

    def chunk(k, carry):
        r0 = pl.multiple_of(start + k * ch, MOE_ALIGN)
        y = _swiglu(xs_ref[pl.ds(r0, ch), :], wg_ref[0], wu_ref[0], wd_ref[0])
        ys_ref[pl.ds(r0, ch), :] = jnp.where(row_in_chunk < count - k * ch, y, 0.0).astype(BF16)
        return carry

    lax.fori_loop(0, (count + (ch - 1)) // ch, chunk, 0)

    @pl.when(e == n_e - 1)
    def _():
        ys = ys_ref[...]
        for tb in range(tt // blk):
            ts = slice(tb * blk, (tb + 1) * blk)
            pc = posc_ref[ts, :]
            scol = lax.broadcasted_iota(jnp.int32, (blk, n_sorted), 1).astype(F32)
            pw = (jnp.where(scol == pc[:, 0:1], pc[:, 2:3], 0.0)
                  + jnp.where(scol == pc[:, 1:2], pc[:, 3:4], 0.0)).astype(BF16)
            moe = jnp.dot(pw, ys, preferred_element_type=F32)
            o_ref[ts, :] = _rms(x_ref[ts, :] + moe, gout_ref[...])


def _moe_experts(start, count, x, hn, posc, posr, wg, wu, wd, g_out):
    n, d = x.shape
    n_e, _, dff = wg.shape
    tt = MOE_TT
    tile = lambda w: pl.BlockSpec((tt, w), lambda i, e, *_: (i, 0))
    grid_spec = pltpu.PrefetchScalarGridSpec(
        num_scalar_prefetch=2,
        grid=(n // tt, n_e),
        in_specs=[tile(d), tile(d), tile(LANES), pl.BlockSpec((1, 8, tt), lambda i, e, *_: (i, 0, 0)),
                  pl.BlockSpec((1, d, dff), lambda i, e, *_: (e, 0, 0)),
                  pl.BlockSpec((1, d, dff), lambda i, e, *_: (e, 0, 0)),
                  pl.BlockSpec((1, dff, d), lambda i, e, *_: (e, 0, 0)),
                  pl.BlockSpec((1, d), lambda i, e, *_: (0, 0))],
        out_specs=tile(d),
        scratch_shapes=[pltpu.VMEM((MOE_SORTED, d), BF16), pltpu.VMEM((MOE_SORTED, d), BF16)],
    )
    return pl.pallas_call(
        functools.partial(_moe_expert_kernel, n_e=n_e),
        grid_spec=grid_spec,
        out_shape=jax.ShapeDtypeStruct((n, d), F32),
        compiler_params=_cparams(("arbitrary", "arbitrary")),
        name="moe_experts",
    )(start, count, x, hn, posc, posr, wg, wu, wd, g_out)


def _rwkv_proj_kernel(h_ref, hp_ref, g_ref, mix_ref, wr_ref, wk_ref, wv_ref, w1_ref, w2_ref,
                      a1_ref, a2_ref, g1_ref, g2_ref, w0_ref, a0_ref, kk_ref, ka_ref, rk_ref,
                      hsum_ref,
                      r_out, k_out, v_out, kk_out, bb_out, lw_out, g_out, bonus_out,
                      *, tiles_per_seq):
    i = pl.program_id(0)
    g = g_ref[...]
    hn = _rms(h_ref[...], g)
    prev = _rms(hp_ref[7:8, :], g)
    prev = jnp.where(i % tiles_per_seq == 0, 0.0, prev)
    row = lax.broadcasted_iota(jnp.int32, hn.shape, 0)
    shifted = jnp.where(row == 0, prev, pltpu.roll(hn, 1, axis=0))
    dx = shifted - hn
    xr, xw, xk, xv, xa, xg = (hn + dx * mix_ref[m:m + 1, :] for m in range(6))
    r = _dot(xr, wr_ref[...])
    k = _dot(xk, wk_ref[...])
    v = _dot(xv, wv_ref[...])
    wl = w0_ref[...] + _dot(jnp.tanh(_dot(xw, w1_ref[...])), w2_ref[...])
    nwl = -wl
    w_log = -(jnp.maximum(nwl, 0.0) + jnp.log(1.0 + jnp.exp(-jnp.abs(nwl)))) - 0.5
    a = _sigmoid(a0_ref[...] + _dot(_dot(xa, a1_ref[...]), a2_ref[...]))
    gate = _dot(_sigmoid(_dot(xg, g1_ref[...])), g2_ref[...])
    hsum = hsum_ref[...]

    def head_sum(z):
        w = hsum.shape[0]
        return jnp.concatenate([_dot(z[:, s:s + w], hsum) for s in range(0, z.shape[1], w)], axis=1)

    kk = k * kk_ref[...]
    kk = kk / jnp.maximum(jnp.sqrt(head_sum(kk * kk)), 1e-12)
    k2 = k * (1.0 + (a - 1.0) * ka_ref[...])
    r_out[...] = r.astype(r_out.dtype)
    k_out[...] = k2.astype(k_out.dtype)
    v_out[...] = v.astype(v_out.dtype)
    kk_out[...] = kk.astype(kk_out.dtype)
    bb_out[...] = (kk * a).astype(bb_out.dtype)
    lw_out[...] = -jnp.exp(w_log)
    g_out[...] = gate.astype(g_out.dtype)
    bonus_out[...] = (head_sum(r * k2 * rk_ref[...]) * v).astype(bonus_out.dtype)


def _rwkv_proj(h, t, g, mix8, wr, wk, wv, w1, w2, a1, a2, g1, g2, w0, a0, k_k, k_a, r_k, hsum):
    n, d = h.shape
    tm = PROJ_TM
    full = lambda arr: pl.BlockSpec(arr.shape, lambda i: (0,) * arr.ndim)
    tile = pl.BlockSpec((tm, d), lambda i: (i, 0))
    prev = pl.BlockSpec((8, d), lambda i: (jnp.maximum(i * (tm // 8) - 1, 0), 0))
    consts = [g, mix8, wr, wk, wv, w1, w2, a1, a2, g1, g2, w0, a0, k_k, k_a, r_k, hsum]
    return pl.pallas_call(
        functools.partial(_rwkv_proj_kernel, tiles_per_seq=t // tm),
        grid=(n // tm,),
        in_specs=[tile, prev] + [full(c) for c in consts],
        out_specs=[tile] * 8,
        out_shape=[jax.ShapeDtypeStruct((n, d), F32 if idx == 5 else BF16) for idx in range(8)],
        compiler_params=_cparams(("arbitrary",)),
        name="rwkv_proj",
    )(h, h, *consts)


def _rwkv_rec_kernel(r_ref, k_ref, v_ref, kk_ref, bb_ref, lw_ref, bonus_ref, gate_ref,
                     lng_ref, lnb_ref, o_ref, st_ref):
    c, hd, w = REC_C, RWKV_HEAD, REC_W
    bsz, _, d = r_ref.shape
    units = [(b, u) for b in range(bsz) for u in range(d // w)]

    @pl.when(pl.program_id(0) == 0)
    def _():
        st_ref[...] = jnp.zeros_like(st_ref)

    rr, cc = _iota2((c, c))
    cum_m = jnp.where(cc <= rr, 1.0, 0.0).astype(BF16)
    lane_c = lax.broadcasted_iota(jnp.int32, (c, w), 1)
    row_c = lax.broadcasted_iota(jnp.int32, (c, w), 0)
    head_c = lane_c >> (hd.bit_length() - 1)
    colm = lane_c & (hd - 1)
    strict = colm < row_c
    incl = colm <= row_c
    eye_cat = jnp.where(colm == row_c, 1.0, 0.0)
    r2, c2 = _iota2((w, w))
    bd = _same_block(r2, c2, hd)
    hmean = jnp.where(bd, 1.0 / hd, 0.0).astype(BF16)

    def stack(xm):
        return jnp.concatenate([jnp.where(head_c == h, xm, 0.0) for h in range(w // hd)], axis=0)

    prep = []
    for b in range(bsz):
        lw = lw_ref[b]
        lp = _dot_exact_lhs(cum_m, lw)
        pend = lp[c - 1:c, :]
        e_neg = jnp.exp(-lp)
        e_end = jnp.exp(pend - lp)
        kb, bbv = k_ref[b].astype(F32), bb_ref[b].astype(F32)
        prep.append(dict(rt=r_ref[b].astype(F32) * jnp.exp(lp), kt=kk_ref[b].astype(F32) * jnp.exp(lp - lw),
                         kh=kb * e_neg, bh=bbv * e_neg, khp=kb * e_end, bhp=bbv * e_end,
                         p_end=jnp.exp(pend), v=v_ref[b]))

    def usl(name, b, u):
        return prep[b][name][:, u * w:(u + 1) * w]

    m_kk, m_kb, a_rk, a_rb = [], [], [], []
    for (b, u) in units:
        lhs = jnp.concatenate([usl('kt', b, u), usl('rt', b, u)], axis=0)
        rhs = jnp.concatenate([stack(usl('kh', b, u)), stack(usl('bh', b, u))], axis=0)
        gram = _dot_nt(lhs, rhs)
        m_kk.append(jnp.where(strict, gram[0:c, 0:w], 0.0))
        m_kb.append(jnp.where(strict, gram[0:c, w:2 * w], 0.0))
        a_rk.append(jnp.where(incl, gram[c:2 * c, 0:w], 0.0))
        a_rb.append(jnp.where(incl, gram[c:2 * c, w:2 * w], 0.0))
    tinv = [eye_cat - m for m in m_kb]
    pw = [_dot(m, stack(m)) for m in m_kb]
    mv = [_dot(jnp.concatenate([m_kk[i], a_rk[i]], axis=0), stack(usl('v', b, u)))
          for i, (b, u) in enumerate(units)]
    n_lvl = (c - 1).bit_length() - 1
    for lvl in range(n_lvl):
        last = lvl == n_lvl - 1
        for i in range(len(units)):
            spw = stack(pw[i])
            if last:
                tinv[i] = tinv[i] + _dot(tinv[i], spw)
            else:
                res = _dot(jnp.concatenate([tinv[i], pw[i]], axis=0), spw)
                tinv[i] = tinv[i] + res[0:c]
                pw[i] = res[c:2 * c]
    ka = [_dot_nt(jnp.concatenate([usl('kt', b, u), usl('rt', b, u)], axis=0), st_ref[b, u])
          for (b, u) in units]
    uu = [_dot(tinv[i], stack(ka[i][0:c] + mv[i][0:c])) for i in range(len(units))]
    ys = []
    for i, (b, u) in enumerate(units):
        ys.append(ka[i][c:2 * c] + mv[i][c:2 * c] - _dot(a_rb[i], stack(uu[i])))
        inc = _dot_tn(jnp.concatenate([usl('v', b, u).astype(F32), uu[i]], axis=0),
                      jnp.concatenate([usl('khp', b, u), -usl('bhp', b, u)], axis=0))
        st_ref[b, u] = st_ref[b, u] * usl('p_end', b, u) + jnp.where(bd, inc, 0.0)
    y_all = jnp.concatenate(ys, axis=0)
    dlt = y_all - _dot(y_all, hmean)
    yn_all = dlt * lax.rsqrt(_dot(dlt * dlt, hmean) + GN_EPS)
    for i, (b, u) in enumerate(units):
        ls = slice(u * w, (u + 1) * w)
        yn = yn_all[i * c:(i + 1) * c] * lng_ref[:, ls] + lnb_ref[:, ls]
        o_ref[b, :, ls] = ((yn + bonus_ref[b, :, ls].astype(F32))
                           * gate_ref[b, :, ls].astype(F32)).astype(o_ref.dtype)


def _rwkv_rec(r, k, v, kk, bb, lw, bonus, gate, ln_g, ln_b):
    bsz, t, d = r.shape
    blk = pl.BlockSpec((bsz, REC_C, d), lambda i: (0, i, 0))
    vec = pl.BlockSpec((1, d), lambda i: (0, 0))
    return pl.pallas_call(
        _rwkv_rec_kernel,
        grid=(t // REC_C,),
        in_specs=[blk] * 8 + [vec, vec],
        out_specs=blk,
        out_shape=jax.ShapeDtypeStruct((bsz, t, d), BF16),
        scratch_shapes=[pltpu.VMEM((bsz, d // REC_W, REC_W, REC_W), F32)],
        compiler_params=_cparams(("arbitrary",)),
        name="rwkv_rec",
    )(r, k, v, kk, bb, lw, bonus, gate, ln_g, ln_b)


def kernel(x, norm_mix_g, norm_ffn_g, norm_out_g, mix_w_in, mix_w_out, gmlp_norm_g, gmlp_w_s, gmlp_b_s, hgrn_lb_logits, hgrn_onorm_g, ffn_w_gate, ffn_w_up, ffn_w_down, rwkv_mix, rwkv_w_r, rwkv_w_k, rwkv_w_v, rwkv_w_o, rwkv_w0, rwkv_w1, rwkv_w2, rwkv_a0, rwkv_a1, rwkv_a2, rwkv_g1, rwkv_g2, rwkv_k_k, rwkv_k_a, rwkv_r_k, rwkv_ln_g, rwkv_ln_b, moe_router, moe_router_b, moe_w_gate, moe_w_up, moe_w_down):
    bsz, t, d = x.shape
    n = bsz * t
    row = lambda vec: vec.reshape(1, -1).astype(F32)
    bf = lambda w: w.astype(BF16)

    lower_bounds = jnp.cumsum(jax.nn.softmax(hgrn_lb_logits.astype(F32), axis=0), axis=0)
    bias_b = jnp.repeat(gmlp_b_s[0].astype(F32).T, GMLP_DIM, axis=1)
    h = _mixer0(x, row(norm_mix_g[0]), bf(mix_w_in[0]), bf(mix_w_out[0]), row(gmlp_norm_g[0]),
                gmlp_w_s[0].astype(F32), bias_b, row(lower_bounds[0]), row(hgrn_onorm_g[0]))
    h = h.reshape(n, d)
    h = _ffn(h, row(norm_ffn_g[0]), bf(ffn_w_gate[0]), bf(ffn_w_up[0]), bf(ffn_w_down[0]))

    mix8 = jnp.concatenate([rwkv_mix[0].astype(F32), jnp.zeros((2, d), F32)], axis=0)
    head_id = jnp.arange(REC_W) // RWKV_HEAD
    hsum = (head_id[:, None] == head_id[None, :]).astype(BF16)
    r, k2, v, kk, bb, lw, gate, bonus = _rwkv_proj(
        h, t, row(norm_mix_g[1]), mix8, bf(rwkv_w_r[0]), bf(rwkv_w_k[0]), bf(rwkv_w_v[0]),
        bf(rwkv_w1[0]), bf(rwkv_w2[0]), bf(rwkv_a1[0]), bf(rwkv_a2[0]), bf(rwkv_g1[0]), bf(rwkv_g2[0]),
        row(rwkv_w0[0]), row(rwkv_a0[0]), row(rwkv_k_k[0]), row(rwkv_k_a[0]), row(rwkv_r_k[0]), hsum)
    sh = lambda z: z.reshape(bsz, t, d)
    yg = _rwkv_rec(sh(r), sh(k2), sh(v), sh(kk), sh(bb), sh(lw), sh(bonus), sh(gate),
                   row(rwkv_ln_g[0]), row(rwkv_ln_b[0]))
    router = jnp.zeros((MOE_ROWS, d), F32).at[:N_EXPERTS].set(moe_router[0].astype(F32).T)
    router_b = jnp.full((MOE_ROWS,), -1e30, F32).at[:N_EXPERTS].set(moe_router_b[0].astype(F32))
    router_b = jnp.broadcast_to(router_b[:, None], (MOE_ROWS, MOE_TT))
    tok = jnp.arange(MOE_TT)
    tri = (tok[:, None] < tok[None, :]).astype(BF16)
    h, hn, posc, posr, counts = _moe_router(h, yg.reshape(n, d), bf(rwkv_w_o[0]), row(norm_ffn_g[1]),
                                            router, router_b, tri)
    count = counts[:, :N_EXPERTS, 0].astype(jnp.int32)
    seg = (count + (MOE_ALIGN - 1)) // MOE_ALIGN * MOE_ALIGN
    start = jnp.cumsum(seg, axis=1) - seg
    out = _moe_experts(start, count, h, hn, posc, posr, bf(moe_w_gate[0]), bf(moe_w_up[0]), bf(moe_w_down[0]),
                       row(norm_out_g))
    return out.reshape(bsz, t, d)
```

```python
import functools

import jax
import jax.numpy as jnp
from jax import lax
from jax.experimental import pallas as pl
from jax.experimental.pallas import tpu as pltpu

F32 = jnp.float32
BF16 = jnp.bfloat16

D_MODEL = 1024
GMLP_GROUPS = 4
GMLP_DIM = 128
GMLP_CHUNK = 128
HGRN_HEADS = 4
HGRN_DK = 128
HGRN_CHUNK = 32
MIX_HALF = 512
MIX_IN = 3072
RWKV_HEAD = 64
N_EXPERTS = 8
D_FF_EXPERT = 1408
RMS_EPS = 1e-6
LN_EPS = 1e-5
GN_EPS = 64e-5

LANES = 128
VMEM_LIMIT_BYTES = 56 * 1024 * 1024

MIX_TM = 256
MIX_CUM_BLK = 256
MLP_TM = 512
MOE_TT = 1024
MOE_CH = 128
MOE_ALIGN = 16
MOE_ROWS = 16
MOE_BLK = 256
MOE_SORTED = -(-(2 * MOE_TT + N_EXPERTS * (MOE_ALIGN - 1) + MOE_CH - 1) // MOE_BLK) * MOE_BLK
PROJ_TM = 512
REC_C = 64
REC_NCH = 4
REC_HPU = 4
REC_W = REC_HPU * RWKV_HEAD


def _cparams(sem):
    return pltpu.CompilerParams(dimension_semantics=sem, vmem_limit_bytes=VMEM_LIMIT_BYTES)


def _dot(a, b):
    return jnp.dot(a.astype(BF16), b.astype(BF16), preferred_element_type=F32)


def _dot_nt(a, b):
    return lax.dot_general(a.astype(BF16), b.astype(BF16), (((1,), (1,)), ((), ())),
                           preferred_element_type=F32)


def _dot_tn(a, b):
    return lax.dot_general(a.astype(BF16), b.astype(BF16), (((0,), (0,)), ((), ())),
                           preferred_element_type=F32)


def _split(a):
    hi = a.astype(BF16)
    lo = (a - hi.astype(F32)).astype(BF16)
    return hi, lo


def _dot_exact_lhs(m_bf16, a):
    hi, lo = _split(a)
    return (jnp.dot(m_bf16, hi, preferred_element_type=F32)
            + jnp.dot(m_bf16, lo, preferred_element_type=F32))


def _dot3(a, b):
    a_hi, a_lo = _split(a)
    b_hi, b_lo = _split(b)
    return (jnp.dot(a_hi, b_hi, preferred_element_type=F32)
            + jnp.dot(a_hi, b_lo, preferred_element_type=F32)
            + jnp.dot(a_lo, b_hi, preferred_element_type=F32))


def _rms(x, g):
    return x * lax.rsqrt(jnp.mean(x * x, axis=-1, keepdims=True) + RMS_EPS) * g


def _sigmoid(x):
    return jax.nn.sigmoid(x)


def _silu(x):
    return x * jax.nn.sigmoid(x)


def _iota2(shape):
    return (lax.broadcasted_iota(jnp.int32, shape, 0), lax.broadcasted_iota(jnp.int32, shape, 1))


def _same_block(a, b, size):
    shift = size.bit_length() - 1
    assert 1 << shift == size
    return (a >> shift) == (b >> shift)


def _mixer0_kernel(x_ref, g_ref, win_ref, wout_ref, gng_ref, ws_ref, bias_ref, lb_ref, og_ref,
                   o_ref, st_ref):
    tm = MIX_TM

    @pl.when(pl.program_id(1) == 0)
    def _():
        st_ref[...] = jnp.zeros_like(st_ref)

    x = x_ref[0]
    z = _dot(_rms(x, g_ref[...]), win_ref[...])

    u = jax.nn.gelu(z[:, 0:MIX_HALF])
    v = jax.nn.gelu(z[:, MIX_HALF:2 * MIX_HALF])
    row, col = _iota2((GMLP_CHUNK, GMLP_CHUNK))
    tril = col <= row
    mixed_groups = []
    for g in range(GMLP_GROUPS):
        gs = slice(g * GMLP_DIM, (g + 1) * GMLP_DIM)
        vg = v[:, gs]
        mu = jnp.mean(vg, axis=-1, keepdims=True)
        d = vg - mu
        var = jnp.mean(d * d, axis=-1, keepdims=True)
        vn = (d * lax.rsqrt(var + LN_EPS) * gng_ref[:, gs]).astype(BF16)
        wg = jnp.where(tril, ws_ref[g], 0.0).astype(BF16)
        parts = [jnp.dot(wg, vn[c * GMLP_CHUNK:(c + 1) * GMLP_CHUNK], preferred_element_type=F32)
                 for c in range(tm // GMLP_CHUNK)]
        mixed_groups.append(jnp.concatenate(parts, axis=0))
    bias = jnp.concatenate([bias_ref[...]] * (tm // GMLP_CHUNK), axis=0)
    y_a = u * (jnp.concatenate(mixed_groups, axis=1) + bias)

    o0 = 2 * MIX_HALF
    zq = z[:, o0:o0 + MIX_HALF]
    zf = z[:, o0 + MIX_HALF:o0 + 2 * MIX_HALF]
    zi = z[:, o0 + 2 * MIX_HALF:o0 + 3 * MIX_HALF]
    zg = z[:, o0 + 3 * MIX_HALF:o0 + 4 * MIX_HALF]
    lb = lb_ref[...]
    q = _silu(zq)
    f = lb + (1.0 - lb) * _sigmoid(zf)
    k = 1.0 - f
    lf = jnp.log(f)
    cb = MIX_CUM_BLK
    rr, cc = _iota2((cb, cb))
    same = _same_block(rr, cc, HGRN_CHUNK)
    cum_m = jnp.where(same & (cc <= rr), 1.0, 0.0).astype(BF16)
    end_m = jnp.where(same, 1.0, 0.0).astype(BF16)
    b = jnp.concatenate([_dot_exact_lhs(cum_m, lf[s:s + cb]) for s in range(0, tm, cb)], axis=0)
    b_end = jnp.concatenate([_dot_exact_lhs(end_m, lf[s:s + cb]) for s in range(0, tm, cb)], axis=0)
    q_dec = (q * jnp.exp(b)).astype(BF16)
    k_inv = (k * jnp.exp(-b)).astype(BF16)
    k_end = (k * jnp.exp(b_end - b)).astype(BF16)
    dec = jnp.exp(b_end)
    vb = zi.astype(BF16)

    r128, c128 = _iota2((128, 128))
    intra_mask = _same_block(r128, c128, HGRN_CHUNK) & (c128 <= r128)
    og = og_ref[...]
    heads = [slice(h * HGRN_DK, (h + 1) * HGRN_DK) for h in range(HGRN_HEADS)]
    chunks = [slice(c * HGRN_CHUNK, (c + 1) * HGRN_CHUNK) for c in range(tm // HGRN_CHUNK)]
    inc_t = [[_dot_tn(vb[cs, hs], k_end[cs, hs]) for cs in chunks] for hs in heads]
    scores = [[jnp.where(intra_mask, _dot_nt(q_dec[rs, hs], k_inv[rs, hs]), 0.0)
               for rs in (slice(rb * 128, (rb + 1) * 128) for rb in range(tm // 128))] for hs in heads]
    st_in = []
    for h, hs in enumerate(heads):
        st, per_chunk = st_ref[h], []
        for c, cs in enumerate(chunks):
            per_chunk.append(st)
            st = st * dec[cs.start:cs.start + 1, hs] + inc_t[h][c]
        st_ref[h] = st
        st_in.append(per_chunk)
    o_heads = []
    for h, hs in enumerate(heads):
        o_intra = jnp.concatenate([_dot(sc, vb[rb * 128:(rb + 1) * 128, hs])
                                   for rb, sc in enumerate(scores[h])], axis=0)
        o_inter = jnp.concatenate([_dot_nt(q_dec[cs, hs], st_in[h][c]) for c, cs in enumerate(chunks)], axis=0)
        o_h = o_intra + o_inter
        o_h = o_h * lax.rsqrt(jnp.mean(o_h * o_h, axis=-1, keepdims=True) + RMS_EPS) * og
        o_heads.append(o_h * _silu(zg[:, hs]))

    y = jnp.concatenate([y_a] + o_heads, axis=1)
    o_ref[0] = x + _dot(y, wout_ref[...])


def _mixer0(x, g, w_in, w_out, gn_g, w_s, bias_b, lb, og):
    bsz, t, d = x.shape
    grid = (bsz, t // MIX_TM)
    const = lambda shape: pl.BlockSpec(shape, lambda b, i: (0,) * len(shape))
    return pl.pallas_call(
        _mixer0_kernel,
        grid=grid,
        in_specs=[
            pl.BlockSpec((1, MIX_TM, d), lambda b, i: (b, i, 0)),
            const((1, d)),
            const((d, MIX_IN)),
            const((2 * MIX_HALF, d)),
            const((1, MIX_HALF)),
            const((GMLP_GROUPS, GMLP_CHUNK, GMLP_CHUNK)),
            const((GMLP_CHUNK, MIX_HALF)),
            const((1, MIX_HALF)),
            const((1, HGRN_DK)),
        ],
        out_specs=pl.BlockSpec((1, MIX_TM, d), lambda b, i: (b, i, 0)),
        out_shape=jax.ShapeDtypeStruct((bsz, t, d), F32),
        scratch_shapes=[pltpu.VMEM((HGRN_HEADS, HGRN_DK, HGRN_DK), F32)],
        compiler_params=_cparams(("arbitrary", "arbitrary")),
        name="mixer0",
    )(x, g, w_in, w_out, gn_g, w_s, bias_b, lb, og)


def _swiglu(xb, wg, wu, wd):
    a = jnp.dot(xb, wg, preferred_element_type=F32)
    b = jnp.dot(xb, wu, preferred_element_type=F32)
    return _dot(_silu(a) * b, wd)


def _ffn_kernel(x_ref, g_ref, wg_ref, wu_ref, wd_ref, o_ref):
    x = x_ref[...]
    o_ref[...] = x + _swiglu(_rms(x, g_ref[...]).astype(BF16), wg_ref[...], wu_ref[...], wd_ref[...])


def _ffn(x, g, wg, wu, wd):
    n, d = x.shape
    tm = MLP_TM
    resident = lambda w: pl.BlockSpec(w.shape, lambda i: (0, 0), pipeline_mode=pl.Buffered(1))
    return pl.pallas_call(
        _ffn_kernel,
        grid=(n // tm,),
        in_specs=[pl.BlockSpec((tm, d), lambda i: (i, 0)),
                  pl.BlockSpec((1, d), lambda i: (0, 0)),
                  resident(wg), resident(wu), resident(wd)],
        out_specs=pl.BlockSpec((tm, d), lambda i: (i, 0)),
        out_shape=jax.ShapeDtypeStruct((n, d), F32),
        compiler_params=_cparams(("arbitrary",)),
        name="ffn_mlp",
    )(x, g, wg, wu, wd)


def _moe_router_kernel(x_ref, y_ref, wo_ref, g_ref, router_ref, rb_ref, tri_ref,
                       h_ref, hn_ref, posc_ref, posr_ref, cnt_ref):
    h = x_ref[...] + jnp.dot(y_ref[...], wo_ref[...], preferred_element_type=F32)
    h_ref[...] = h
    hn = _rms(h, g_ref[...])
    hn_ref[...] = hn.astype(BF16)
    hn_hi, hn_lo = _split(hn)
    r_hi, r_lo = _split(router_ref[...])
    nt = (((1,), (1,)), ((), ()))
    logits = (lax.dot_general(r_hi, hn_hi, nt, preferred_element_type=F32)
              + lax.dot_general(r_hi, hn_lo, nt, preferred_element_type=F32)
              + lax.dot_general(r_lo, hn_hi, nt, preferred_element_type=F32)) + rb_ref[...]
    n_rows = logits.shape[0]
    sub = lax.broadcasted_iota(jnp.int32, logits.shape, 0)
    m1 = jnp.max(logits, axis=0, keepdims=True)
    i1 = jnp.min(jnp.where(logits == m1, sub, n_rows), axis=0, keepdims=True)
    rest = jnp.where(sub == i1, -jnp.inf, logits)
    m2 = jnp.max(rest, axis=0, keepdims=True)
    i2 = jnp.min(jnp.where(rest == m2, sub, n_rows), axis=0, keepdims=True)
    e2 = jnp.exp(m2 - m1)
    den = 1.0 + e2
    sel1, sel2 = sub == i1, sub == i2
    sel_f = jnp.where(sel1 | sel2, 1.0, 0.0)
    rank = jnp.dot(sel_f.astype(BF16), tri_ref[...], preferred_element_type=F32)
    cnt = jnp.sum(sel_f, axis=1, keepdims=True)
    seg = jnp.floor((cnt + (MOE_ALIGN - 1)) * (1.0 / MOE_ALIGN)) * MOE_ALIGN
    starts, run = [], jnp.zeros((1, 1), F32)
    for e in range(n_rows):
        starts.append(run)
        run = run + seg[e:e + 1]
    pos = jnp.concatenate(starts, axis=0) + rank
    pos1 = jnp.sum(jnp.where(sel1, pos, 0.0), axis=0, keepdims=True)
    pos2 = jnp.sum(jnp.where(sel2, pos, 0.0), axis=0, keepdims=True)
    info = jnp.concatenate([pos1, pos2, 1.0 / den, e2 / den], axis=0)
    posr_ref[0] = jnp.concatenate([info, jnp.zeros((posr_ref.shape[1] - 4, info.shape[1]), F32)], axis=0)
    posc_ref[...] = jnp.concatenate([info, jnp.zeros((LANES - 4, info.shape[1]), F32)], axis=0).T
    cnt_ref[0] = jnp.broadcast_to(cnt[0:cnt_ref.shape[1]], cnt_ref.shape[1:])


def _moe_router(x, y, w_o, g, router, router_b, tri):
    n, d = x.shape
    tm = MOE_TT
    tile = lambda w: pl.BlockSpec((tm, w), lambda i: (i, 0))
    full = lambda arr: pl.BlockSpec(arr.shape, lambda i: (0,) * arr.ndim)
    small = lambda w: pl.BlockSpec((1, 8, w), lambda i: (i, 0, 0))
    return pl.pallas_call(
        _moe_router_kernel,
        grid=(n // tm,),
        in_specs=[tile(d), tile(d), full(w_o), full(g), full(router), full(router_b), full(tri)],
        out_specs=[tile(d), tile(d), tile(LANES), small(tm), small(LANES)],
        out_shape=[jax.ShapeDtypeStruct((n, d), F32), jax.ShapeDtypeStruct((n, d), BF16),
                   jax.ShapeDtypeStruct((n, LANES), F32), jax.ShapeDtypeStruct((n // tm, 8, tm), F32),
                   jax.ShapeDtypeStruct((n // tm, 8, LANES), F32)],
        compiler_params=_cparams(("arbitrary",)),
        name="moe_router",
    )(x, y, w_o, g, router, router_b, tri)


def _moe_expert_kernel(start_ref, cnt_ref, x_ref, hn_ref, posc_ref, posr_ref, wg_ref, wu_ref, wd_ref, gout_ref,
                       o_ref, xs_ref, ys_ref, *, n_e):
    i, e = pl.program_id(0), pl.program_id(1)
    tt, ch, blk = MOE_TT, MOE_CH, MOE_BLK
    n_sorted = xs_ref.shape[0]

    @pl.when(e == 0)
    def _():
        pos1, pos2 = posr_ref[0, 0:1, :], posr_ref[0, 1:2, :]
        hn = hn_ref[...]
        for rb in range(n_sorted // blk):
            srow = (lax.broadcasted_iota(jnp.int32, (blk, tt), 0) + rb * blk).astype(F32)
            p = jnp.where((srow == pos1) | (srow == pos2), 1.0, 0.0).astype(BF16)
            xs_ref[rb * blk:(rb + 1) * blk, :] = jnp.dot(p, hn, preferred_element_type=F32).astype(BF16)
        ys_ref[...] = jnp.zeros_like(ys_ref)

    start, count = start_ref[i, e], cnt_ref[i, e]
    row_in_chunk = lax.broadcasted_iota(jnp.int32, (ch, 1), 0)

    def chunk(k, carry):
        r0 = pl.multiple_of(start + k * ch, MOE_ALIGN)
        y = _swiglu(xs_ref[pl.ds(r0, ch), :], wg_ref[0], wu_ref[0], wd_ref[0])
        ys_ref[pl.ds(r0, ch), :] = jnp.where(row_in_chunk < count - k * ch, y, 0.0).astype(BF16)
        return carry

    lax.fori_loop(0, (count + (ch - 1)) // ch, chunk, 0)

    @pl.when(e == n_e - 1)
    def _():
        ys = ys_ref[...]
        for tb in range(tt // blk):
            ts = slice(tb * blk, (tb + 1) * blk)
            pc = posc_ref[ts, :]
            scol = lax.broadcasted_iota(jnp.int32, (blk, n_sorted), 1).astype(F32)
            pw = (jnp.where(scol == pc[:, 0:1], pc[:, 2:3], 0.0)
                  + jnp.where(scol == pc[:, 1:2], pc[:, 3:4], 0.0)).astype(BF16)
            moe = jnp.dot(pw, ys, preferred_element_type=F32)
            o_ref[ts, :] = _rms(x_ref[ts, :] + moe, gout_ref[...])


def _moe_experts(start, count, x, hn, posc, posr, wg, wu, wd, g_out):
    n, d = x.shape
    n_e, _, dff = wg.shape
    tt = MOE_TT
    tile = lambda w: pl.BlockSpec((tt, w), lambda i, e, *_: (i, 0))
    grid_spec = pltpu.PrefetchScalarGridSpec(
        num_scalar_prefetch=2,
        grid=(n // tt, n_e),
        in_specs=[tile(d), tile(d), tile(LANES), pl.BlockSpec((1, 8, tt), lambda i, e, *_: (i, 0, 0)),
                  pl.BlockSpec((1, d, dff), lambda i, e, *_: (e, 0, 0)),
                  pl.BlockSpec((1, d, dff), lambda i, e, *_: (e, 0, 0)),
                  pl.BlockSpec((1, dff, d), lambda i, e, *_: (e, 0, 0)),
                  pl.BlockSpec((1, d), lambda i, e, *_: (0, 0))],
        out_specs=tile(d),
        scratch_shapes=[pltpu.VMEM((MOE_SORTED, d), BF16), pltpu.VMEM((MOE_SORTED, d), BF16)],
    )
    return pl.pallas_call(
        functools.partial(_moe_expert_kernel, n_e=n_e),
        grid_spec=grid_spec,
        out_shape=jax.ShapeDtypeStruct((n, d), F32),
        compiler_params=_cparams(("arbitrary", "arbitrary")),
        name="moe_experts",
    )(start, count, x, hn, posc, posr, wg, wu, wd, g_out)


def _rwkv_proj_kernel(h_ref, hp_ref, g_ref, mix_ref, wr_ref, wk_ref, wv_ref, w1_ref, w2_ref,
                      a1_ref, a2_ref, g1_ref, g2_ref, w0_ref, a0_ref, kk_ref, ka_ref, rk_ref,
                      hsum_ref,
                      r_out, k_out, v_out, kk_out, bb_out, lw_out, g_out, bonus_out,
                      *, tiles_per_seq):
    i = pl.program_id(0)
    g = g_ref[...]
    hn = _rms(h_ref[...], g)
    prev = _rms(hp_ref[7:8, :], g)
    prev = jnp.where(i % tiles_per_seq == 0, 0.0, prev)
    row = lax.broadcasted_iota(jnp.int32, hn.shape, 0)
    shifted = jnp.where(row == 0, prev, pltpu.roll(hn, 1, axis=0))
    dx = shifted - hn
    xr, xw, xk, xv, xa, xg = (hn + dx * mix_ref[m:m + 1, :] for m in range(6))
    r = _dot(xr, wr_ref[...])
    k = _dot(xk, wk_ref[...])
    v = _dot(xv, wv_ref[...])
    wl = w0_ref[...] + _dot(jnp.tanh(_dot(xw, w1_ref[...])), w2_ref[...])
    nwl = -wl
    w_log = -(jnp.maximum(nwl, 0.0) + jnp.log(1.0 + jnp.exp(-jnp.abs(nwl)))) - 0.5
    a = _sigmoid(a0_ref[...] + _dot(_dot(xa, a1_ref[...]), a2_ref[...]))
    gate = _dot(_sigmoid(_dot(xg, g1_ref[...])), g2_ref[...])
    hsum = hsum_ref[...]

    def head_sum(z):
        w = hsum.shape[0]
        return jnp.concatenate([_dot(z[:, s:s + w], hsum) for s in range(0, z.shape[1], w)], axis=1)

    kk = k * kk_ref[...]
    kk = kk / jnp.maximum(jnp.sqrt(head_sum(kk * kk)), 1e-12)
    k2 = k * (1.0 + (a - 1.0) * ka_ref[...])
    r_out[...] = r.astype(r_out.dtype)
    k_out[...] = k2.astype(k_out.dtype)
    v_out[...] = v.astype(v_out.dtype)
    kk_out[...] = kk.astype(kk_out.dtype)
    bb_out[...] = (kk * a).astype(bb_out.dtype)
    lw_out[...] = -jnp.exp(w_log)
    g_out[...] = gate.astype(g_out.dtype)
    bonus_out[...] = (head_sum(r * k2 * rk_ref[...]) * v).astype(bonus_out.dtype)


def _rwkv_proj(h, t, g, mix8, wr, wk, wv, w1, w2, a1, a2, g1, g2, w0, a0, k_k, k_a, r_k, hsum):
    n, d = h.shape
    tm = PROJ_TM
    full = lambda arr: pl.BlockSpec(arr.shape, lambda i: (0,) * arr.ndim)
    tile = pl.BlockSpec((tm, d), lambda i: (i, 0))
    prev = pl.BlockSpec((8, d), lambda i: (jnp.maximum(i * (tm // 8) - 1, 0), 0))
    consts = [g, mix8, wr, wk, wv, w1, w2, a1, a2, g1, g2, w0, a0, k_k, k_a, r_k, hsum]
    return pl.pallas_call(
        functools.partial(_rwkv_proj_kernel, tiles_per_seq=t // tm),
        grid=(n // tm,),
        in_specs=[tile, prev] + [full(c) for c in consts],
        out_specs=[tile] * 8,
        out_shape=[jax.ShapeDtypeStruct((n, d), F32 if idx == 5 else BF16) for idx in range(8)],
        compiler_params=_cparams(("arbitrary",)),
        name="rwkv_proj",
    )(h, h, *consts)


def _rwkv_rec_kernel(r_ref, k_ref, v_ref, kk_ref, bb_ref, lw_ref, bonus_ref, gate_ref,
                     lng_ref, lnb_ref, o_ref, st_ref):
    c, hd, w = REC_C, RWKV_HEAD, REC_W
    bsz, rows, d = r_ref.shape
    n_u = d // w
    chunks = [slice(ci * c, (ci + 1) * c) for ci in range(rows // c)]
    units = [(ci, b, u) for ci in range(len(chunks)) for b in range(bsz) for u in range(n_u)]

    @pl.when(pl.program_id(0) == 0)
    def _():
        st_ref[...] = jnp.zeros_like(st_ref)

    rr, cc = _iota2((c, c))
    cum_m = jnp.where(cc <= rr, 1.0, 0.0).astype(BF16)
    lane_c = lax.broadcasted_iota(jnp.int32, (c, w), 1)
    row_c = lax.broadcasted_iota(jnp.int32, (c, w), 0)
    head_c = lane_c >> (hd.bit_length() - 1)
    colm = lane_c & (hd - 1)
    strict = colm < row_c
    incl = colm <= row_c
    eye_cat = jnp.where(colm == row_c, 1.0, 0.0)
    r2, c2 = _iota2((w, w))
    bd = _same_block(r2, c2, hd)
    hmean = jnp.where(bd, 1.0 / hd, 0.0).astype(BF16)

    def stack(xm):
        return jnp.concatenate([jnp.where(head_c == h, xm, 0.0) for h in range(w // hd)], axis=0)

    prep = {}
    for ci, cs in enumerate(chunks):
        for b in range(bsz):
            lw = lw_ref[b, cs, :]
            lp = _dot_exact_lhs(cum_m, lw)
            pend = lp[c - 1:c, :]
            e_neg = jnp.exp(-lp)
            e_end = jnp.exp(pend - lp)
            kb, bbv = k_ref[b, cs, :].astype(F32), bb_ref[b, cs, :].astype(F32)
            prep[ci, b] = dict(rt=r_ref[b, cs, :].astype(F32) * jnp.exp(lp),
                               kt=kk_ref[b, cs, :].astype(F32) * jnp.exp(lp - lw),
                               kh=kb * e_neg, bh=bbv * e_neg, khp=kb * e_end, bhp=bbv * e_end,
                               p_end=jnp.exp(pend), v=v_ref[b, cs, :])

    def usl(name, ci, b, u):
        return prep[ci, b][name][:, u * w:(u + 1) * w]

    m_kk, m_kb, a_rk, a_rb = [], [], [], []
    for unit in units:
        lhs = jnp.concatenate([usl('kt', *unit), usl('rt', *unit)], axis=0)
        rhs = jnp.concatenate([stack(usl('kh', *unit)), stack(usl('bh', *unit))], axis=0)
        gram = _dot_nt(lhs, rhs)
        m_kk.append(jnp.where(strict, gram[0:c, 0:w], 0.0))
        m_kb.append(jnp.where(strict, gram[0:c, w:2 * w], 0.0))
        a_rk.append(jnp.where(incl, gram[c:2 * c, 0:w], 0.0))
        a_rb.append(jnp.where(incl, gram[c:2 * c, w:2 * w], 0.0))
    tinv = [eye_cat - m for m in m_kb]
    pw = [_dot(m, stack(m)) for m in m_kb]
    mv = [_dot(jnp.concatenate([m_kk[i], a_rk[i]], axis=0), stack(usl('v', *unit)))
          for i, unit in enumerate(units)]
    n_lvl = (c - 1).bit_length() - 1
    for lvl in range(n_lvl):
        last = lvl == n_lvl - 1
        for i in range(len(units)):
            spw = stack(pw[i])
            if last:
                tinv[i] = tinv[i] + _dot(tinv[i], spw)
            else:
                res = _dot(jnp.concatenate([tinv[i], pw[i]], axis=0), spw)
                tinv[i] = tinv[i] + res[0:c]
                pw[i] = res[c:2 * c]
    ys = []
    for ci in range(len(chunks)):
        mine = [(i, unit) for i, unit in enumerate(units) if unit[0] == ci]
        ka = {i: _dot_nt(jnp.concatenate([usl('kt', *unit), usl('rt', *unit)], axis=0), st_ref[unit[1], unit[2]])
              for i, unit in mine}
        uu = {i: _dot(tinv[i], stack(ka[i][0:c] + mv[i][0:c])) for i, _ in mine}
        for i, unit in mine:
            _, b, u = unit
            ys.append(ka[i][c:2 * c] + mv[i][c:2 * c] - _dot(a_rb[i], stack(uu[i])))
            inc = _dot_tn(jnp.concatenate([usl('v', *unit).astype(F32), uu[i]], axis=0),
                          jnp.concatenate([usl('khp', *unit), -usl('bhp', *unit)], axis=0))
            st_ref[b, u] = st_ref[b, u] * usl('p_end', *unit) + jnp.where(bd, inc, 0.0)
    y_all = jnp.concatenate(ys, axis=0)
    dlt = y_all - _dot(y_all, hmean)
    yn_all = dlt * lax.rsqrt(_dot(dlt * dlt, hmean) + GN_EPS)
    for i, (ci, b, u) in enumerate(units):
        ls, cs = slice(u * w, (u + 1) * w), chunks[ci]
        yn = yn_all[i * c:(i + 1) * c] * lng_ref[:, ls] + lnb_ref[:, ls]
        o_ref[b, cs, ls] = ((yn + bonus_ref[b, cs, ls].astype(F32))
                            * gate_ref[b, cs, ls].astype(F32)).astype(o_ref.dtype)


def _rwkv_rec(r, k, v, kk, bb, lw, bonus, gate, ln_g, ln_b):
    bsz, t, d = r.shape
    rows = REC_C * REC_NCH
    blk = pl.BlockSpec((bsz, rows, d), lambda i: (0, i, 0))
    vec = pl.BlockSpec((1, d), lambda i: (0, 0))
    return pl.pallas_call(
        _rwkv_rec_kernel,
        grid=(t // rows,),
        in_specs=[blk] * 8 + [vec, vec],
        out_specs=blk,
        out_shape=jax.ShapeDtypeStruct((bsz, t, d), BF16),
        scratch_shapes=[pltpu.VMEM((bsz, d // REC_W, REC_W, REC_W), F32)],
        compiler_params=_cparams(("arbitrary",)),
        name="rwkv_rec",
    )(r, k, v, kk, bb, lw, bonus, gate, ln_g, ln_b)


def kernel(x, norm_mix_g, norm_ffn_g, norm_out_g, mix_w_in, mix_w_out, gmlp_norm_g, gmlp_w_s, gmlp_b_s, hgrn_lb_logits, hgrn_onorm_g, ffn_w_gate, ffn_w_up, ffn_w_down, rwkv_mix, rwkv_w_r, rwkv_w_k, rwkv_w_v, rwkv_w_o, rwkv_w0, rwkv_w1, rwkv_w2, rwkv_a0, rwkv_a1, rwkv_a2, rwkv_g1, rwkv_g2, rwkv_k_k, rwkv_k_a, rwkv_r_k, rwkv_ln_g, rwkv_ln_b, moe_router, moe_router_b, moe_w_gate, moe_w_up, moe_w_down):
    bsz, t, d = x.shape
    n = bsz * t
    row = lambda vec: vec.reshape(1, -1).astype(F32)
    bf = lambda w: w.astype(BF16)

    lower_bounds = jnp.cumsum(jax.nn.softmax(hgrn_lb_logits.astype(F32), axis=0), axis=0)
    bias_b = jnp.repeat(gmlp_b_s[0].astype(F32).T, GMLP_DIM, axis=1)
    h = _mixer0(x, row(norm_mix_g[0]), bf(mix_w_in[0]), bf(mix_w_out[0]), row(gmlp_norm_g[0]),
                gmlp_w_s[0].astype(F32), bias_b, row(lower_bounds[0]), row(hgrn_onorm_g[0]))
    h = h.reshape(n, d)
    h = _ffn(h, row(norm_ffn_g[0]), bf(ffn_w_gate[0]), bf(ffn_w_up[0]), bf(ffn_w_down[0]))

    mix8 = jnp.concatenate([rwkv_mix[0].astype(F32), jnp.zeros((2, d), F32)], axis=0)
    head_id = jnp.arange(REC_W) // RWKV_HEAD
    hsum = (head_id[:, None] == head_id[None, :]).astype(BF16)
    r, k2, v, kk, bb, lw, gate, bonus = _rwkv_proj(
        h, t, row(norm_mix_g[1]), mix8, bf(rwkv_w_r[0]), bf(rwkv_w_k[0]), bf(rwkv_w_v[0]),
        bf(rwkv_w1[0]), bf(rwkv_w2[0]), bf(rwkv_a1[0]), bf(rwkv_a2[0]), bf(rwkv_g1[0]), bf(rwkv_g2[0]),
        row(rwkv_w0[0]), row(rwkv_a0[0]), row(rwkv_k_k[0]), row(rwkv_k_a[0]), row(rwkv_r_k[0]), hsum)
    sh = lambda z: z.reshape(bsz, t, d)
    yg = _rwkv_rec(sh(r), sh(k2), sh(v), sh(kk), sh(bb), sh(lw), sh(bonus), sh(gate),
                   row(rwkv_ln_g[0]), row(rwkv_ln_b[0]))
    router = jnp.zeros((MOE_ROWS, d), F32).at[:N_EXPERTS].set(moe_router[0].astype(F32).T)
    router_b = jnp.full((MOE_ROWS,), -1e30, F32).at[:N_EXPERTS].set(moe_router_b[0].astype(F32))
    router_b = jnp.broadcast_to(router_b[:, None], (MOE_ROWS, MOE_TT))
    tok = jnp.arange(MOE_TT)
    tri = (tok[:, None] < tok[None, :]).astype(BF16)
    h, hn, posc, posr, counts = _moe_router(h, yg.reshape(n, d), bf(rwkv_w_o[0]), row(norm_ffn_g[1]),
                                            router, router_b, tri)
    count = counts[:, :N_EXPERTS, 0].astype(jnp.int32)
    seg = (count + (MOE_ALIGN - 1)) // MOE_ALIGN * MOE_ALIGN
    start = jnp.cumsum(seg, axis=1) - seg
    out = _moe_experts(start, count, h, hn, posc, posr, bf(moe_w_gate[0]), bf(moe_w_up[0]), bf(moe_w_down[0]),
                       row(norm_out_g))
    return out.reshape(bsz, t, d)
```

```python
import functools

import jax
import jax.numpy as jnp
from jax import lax
from jax.experimental import pallas as pl
from jax.experimental.pallas import tpu as pltpu

F32 = jnp.float32
BF16 = jnp.bfloat16

D_MODEL = 1024
GMLP_GROUPS = 4
GMLP_DIM = 128
GMLP_CHUNK = 128
HGRN_HEADS = 4
HGRN_DK = 128
HGRN_CHUNK = 32
MIX_HALF = 512
MIX_IN = 3072
RWKV_HEAD = 64
N_EXPERTS = 8
D_FF_EXPERT = 1408
RMS_EPS = 1e-6
LN_EPS = 1e-5
GN_EPS = 64e-5

LANES = 128
VMEM_LIMIT_BYTES = 56 * 1024 * 1024

MIX_TM = 256
MIX_NT = 2
MIX_CUM_BLK = 256
MLP_TM = 512
MOE_TT = 1024
MOE_CH = 128
MOE_ALIGN = 16
MOE_ROWS = 16
MOE_BLK = 256
MOE_SORTED = -(-(2 * MOE_TT + N_EXPERTS * (MOE_ALIGN - 1) + MOE_CH - 1) // MOE_BLK) * MOE_BLK
PROJ_TM = 512
REC_C = 64
REC_NCH = 4
REC_HPU = 4
REC_W = REC_HPU * RWKV_HEAD


def _cparams(sem):
    return pltpu.CompilerParams(dimension_semantics=sem, vmem_limit_bytes=VMEM_LIMIT_BYTES)


def _dot(a, b):
    return jnp.dot(a.astype(BF16), b.astype(BF16), preferred_element_type=F32)


def _dot_nt(a, b):
    return lax.dot_general(a.astype(BF16), b.astype(BF16), (((1,), (1,)), ((), ())),
                           preferred_element_type=F32)


def _dot_tn(a, b):
    return lax.dot_general(a.astype(BF16), b.astype(BF16), (((0,), (0,)), ((), ())),
                           preferred_element_type=F32)


def _split(a):
    hi = a.astype(BF16)
    lo = (a - hi.astype(F32)).astype(BF16)
    return hi, lo


def _dot_exact_lhs(m_bf16, a):
    hi, lo = _split(a)
    return (jnp.dot(m_bf16, hi, preferred_element_type=F32)
            + jnp.dot(m_bf16, lo, preferred_element_type=F32))


def _dot3(a, b):
    a_hi, a_lo = _split(a)
    b_hi, b_lo = _split(b)
    return (jnp.dot(a_hi, b_hi, preferred_element_type=F32)
            + jnp.dot(a_hi, b_lo, preferred_element_type=F32)
            + jnp.dot(a_lo, b_hi, preferred_element_type=F32))


def _rms(x, g):
    return x * lax.rsqrt(jnp.mean(x * x, axis=-1, keepdims=True) + RMS_EPS) * g


def _sigmoid(x):
    return jax.nn.sigmoid(x)


def _silu(x):
    return x * jax.nn.sigmoid(x)


def _iota2(shape):
    return (lax.broadcasted_iota(jnp.int32, shape, 0), lax.broadcasted_iota(jnp.int32, shape, 1))


def _same_block(a, b, size):
    shift = size.bit_length() - 1
    assert 1 << shift == size
    return (a >> shift) == (b >> shift)


def _mixer0_row(row_idx, tile_idx, x_ref, g_ref, win_ref, wout_ref, gng_ref, ws_ref, bias_ref, lb_ref, og_ref,
                o_ref, st_ref):
    tm = MIX_TM
    rows = slice(tile_idx * tm, (tile_idx + 1) * tm)
    x = x_ref[row_idx, rows, :]
    z = _dot(_rms(x, g_ref[...]), win_ref[...])
    yield

    u = jax.nn.gelu(z[:, 0:MIX_HALF])
    v = jax.nn.gelu(z[:, MIX_HALF:2 * MIX_HALF])
    row, col = _iota2((GMLP_CHUNK, GMLP_CHUNK))
    tril = col <= row
    mixed_groups = []
    for g in range(GMLP_GROUPS):
        gs = slice(g * GMLP_DIM, (g + 1) * GMLP_DIM)
        vg = v[:, gs]
        mu = jnp.mean(vg, axis=-1, keepdims=True)
        d = vg - mu
        var = jnp.mean(d * d, axis=-1, keepdims=True)
        vn = (d * lax.rsqrt(var + LN_EPS) * gng_ref[:, gs]).astype(BF16)
        wg = jnp.where(tril, ws_ref[g], 0.0).astype(BF16)
        parts = [jnp.dot(wg, vn[c * GMLP_CHUNK:(c + 1) * GMLP_CHUNK], preferred_element_type=F32)
                 for c in range(tm // GMLP_CHUNK)]
        mixed_groups.append(jnp.concatenate(parts, axis=0))
    bias = jnp.concatenate([bias_ref[...]] * (tm // GMLP_CHUNK), axis=0)
    y_a = u * (jnp.concatenate(mixed_groups, axis=1) + bias)
    yield

    o0 = 2 * MIX_HALF
    zq = z[:, o0:o0 + MIX_HALF]
    zf = z[:, o0 + MIX_HALF:o0 + 2 * MIX_HALF]
    zi = z[:, o0 + 2 * MIX_HALF:o0 + 3 * MIX_HALF]
    zg = z[:, o0 + 3 * MIX_HALF:o0 + 4 * MIX_HALF]
    lb = lb_ref[...]
    q = _silu(zq)
    f = lb + (1.0 - lb) * _sigmoid(zf)
    k = 1.0 - f
    lf = jnp.log(f)
    cb = MIX_CUM_BLK
    rr, cc = _iota2((cb, cb))
    same = _same_block(rr, cc, HGRN_CHUNK)
    cum_m = jnp.where(same & (cc <= rr), 1.0, 0.0).astype(BF16)
    end_m = jnp.where(same, 1.0, 0.0).astype(BF16)
    b = jnp.concatenate([_dot_exact_lhs(cum_m, lf[s:s + cb]) for s in range(0, tm, cb)], axis=0)
    b_end = jnp.concatenate([_dot_exact_lhs(end_m, lf[s:s + cb]) for s in range(0, tm, cb)], axis=0)
    q_dec = (q * jnp.exp(b)).astype(BF16)
    k_inv = (k * jnp.exp(-b)).astype(BF16)
    k_end = (k * jnp.exp(b_end - b)).astype(BF16)
    dec = jnp.exp(b_end)
    vb = zi.astype(BF16)
    yield

    r128, c128 = _iota2((128, 128))
    intra_mask = _same_block(r128, c128, HGRN_CHUNK) & (c128 <= r128)
    og = og_ref[...]
    heads = [slice(h * HGRN_DK, (h + 1) * HGRN_DK) for h in range(HGRN_HEADS)]
    chunks = [slice(c * HGRN_CHUNK, (c + 1) * HGRN_CHUNK) for c in range(tm // HGRN_CHUNK)]
    inc_t = [[_dot_tn(vb[cs, hs], k_end[cs, hs]) for cs in chunks] for hs in heads]
    scores = [[jnp.where(intra_mask, _dot_nt(q_dec[rs, hs], k_inv[rs, hs]), 0.0)
               for rs in (slice(rb * 128, (rb + 1) * 128) for rb in range(tm // 128))] for hs in heads]
    yield
    st_in = []
    for h, hs in enumerate(heads):
        st, per_chunk = st_ref[row_idx, h], []
        for c, cs in enumerate(chunks):
            per_chunk.append(st)
            st = st * dec[cs.start:cs.start + 1, hs] + inc_t[h][c]
        st_ref[row_idx, h] = st
        st_in.append(per_chunk)
    yield
    o_heads = []
    for h, hs in enumerate(heads):
        o_intra = jnp.concatenate([_dot(sc, vb[rb * 128:(rb + 1) * 128, hs])
                                   for rb, sc in enumerate(scores[h])], axis=0)
        o_inter = jnp.concatenate([_dot_nt(q_dec[cs, hs], st_in[h][c]) for c, cs in enumerate(chunks)], axis=0)
        o_h = o_intra + o_inter
        o_h = o_h * lax.rsqrt(jnp.mean(o_h * o_h, axis=-1, keepdims=True) + RMS_EPS) * og
        o_heads.append(o_h * _silu(zg[:, hs]))

    yield
    y = jnp.concatenate([y_a] + o_heads, axis=1)
    o_ref[row_idx, rows, :] = x + _dot(y, wout_ref[...])


def _mixer0_kernel(x_ref, *refs):
    st_ref = refs[-1]

    @pl.when(pl.program_id(0) == 0)
    def _():
        st_ref[...] = jnp.zeros_like(st_ref)

    bsz, step_rows, _ = x_ref.shape
    tiles = [_mixer0_row(b, ti, x_ref, *refs) for b in range(bsz) for ti in range(step_rows // MIX_TM)]
    while tiles:
        tiles = [tile for tile in tiles if next(tile, True) is None]


def _mixer0(x, g, w_in, w_out, gn_g, w_s, bias_b, lb, og):
    bsz, t, d = x.shape
    const = lambda shape: pl.BlockSpec(shape, lambda i: (0,) * len(shape))
    return pl.pallas_call(
        _mixer0_kernel,
        grid=(t // (MIX_TM * MIX_NT),),
        in_specs=[
            pl.BlockSpec((bsz, MIX_TM * MIX_NT, d), lambda i: (0, i, 0)),
            const((1, d)),
            const((d, MIX_IN)),
            const((2 * MIX_HALF, d)),
            const((1, MIX_HALF)),
            const((GMLP_GROUPS, GMLP_CHUNK, GMLP_CHUNK)),
            const((GMLP_CHUNK, MIX_HALF)),
            const((1, MIX_HALF)),
            const((1, HGRN_DK)),
        ],
        out_specs=pl.BlockSpec((bsz, MIX_TM * MIX_NT, d), lambda i: (0, i, 0)),
        out_shape=jax.ShapeDtypeStruct((bsz, t, d), F32),
        scratch_shapes=[pltpu.VMEM((bsz, HGRN_HEADS, HGRN_DK, HGRN_DK), F32)],
        compiler_params=_cparams(("arbitrary",)),
        name="mixer0",
    )(x, g, w_in, w_out, gn_g, w_s, bias_b, lb, og)


def _swiglu(xb, wg, wu, wd):
    a = jnp.dot(xb, wg, preferred_element_type=F32)
    b = jnp.dot(xb, wu, preferred_element_type=F32)
    return _dot(_silu(a) * b, wd)


def _ffn_kernel(x_ref, g_ref, wg_ref, wu_ref, wd_ref, o_ref):
    x = x_ref[...]
    o_ref[...] = x + _swiglu(_rms(x, g_ref[...]).astype(BF16), wg_ref[...], wu_ref[...], wd_ref[...])


def _ffn(x, g, wg, wu, wd):
    n, d = x.shape
    tm = MLP_TM
    resident = lambda w: pl.BlockSpec(w.shape, lambda i: (0, 0), pipeline_mode=pl.Buffered(1))
    return pl.pallas_call(
        _ffn_kernel,
        grid=(n // tm,),
        in_specs=[pl.BlockSpec((tm, d), lambda i: (i, 0)),
                  pl.BlockSpec((1, d), lambda i: (0, 0)),
                  resident(wg), resident(wu), resident(wd)],
        out_specs=pl.BlockSpec((tm, d), lambda i: (i, 0)),
        out_shape=jax.ShapeDtypeStruct((n, d), F32),
        compiler_params=_cparams(("arbitrary",)),
        name="ffn_mlp",
    )(x, g, wg, wu, wd)


def _moe_router_kernel(x_ref, y_ref, wo_ref, g_ref, router_ref, rb_ref, tri_ref,
                       h_ref, hn_ref, posc_ref, posr_ref, cnt_ref):
    h = x_ref[...] + jnp.dot(y_ref[...], wo_ref[...], preferred_element_type=F32)
    h_ref[...] = h
    hn = _rms(h, g_ref[...])
    hn_ref[...] = hn.astype(BF16)
    hn_hi, hn_lo = _split(hn)
    r_hi, r_lo = _split(router_ref[...])
    nt = (((1,), (1,)), ((), ()))
    logits = (lax.dot_general(r_hi, hn_hi, nt, preferred_element_type=F32)
              + lax.dot_general(r_hi, hn_lo, nt, preferred_element_type=F32)
              + lax.dot_general(r_lo, hn_hi, nt, preferred_element_type=F32)) + rb_ref[...]
    n_rows = logits.shape[0]
    sub = lax.broadcasted_iota(jnp.int32, logits.shape, 0)
    m1 = jnp.max(logits, axis=0, keepdims=True)
    i1 = jnp.min(jnp.where(logits == m1, sub, n_rows), axis=0, keepdims=True)
    rest = jnp.where(sub == i1, -jnp.inf, logits)
    m2 = jnp.max(rest, axis=0, keepdims=True)
    i2 = jnp.min(jnp.where(rest == m2, sub, n_rows), axis=0, keepdims=True)
    e2 = jnp.exp(m2 - m1)
    den = 1.0 + e2
    sel1, sel2 = sub == i1, sub == i2
    sel_f = jnp.where(sel1 | sel2, 1.0, 0.0)
    rank = jnp.dot(sel_f.astype(BF16), tri_ref[...], preferred_element_type=F32)
    cnt = jnp.sum(sel_f, axis=1, keepdims=True)
    seg = jnp.floor((cnt + (MOE_ALIGN - 1)) * (1.0 / MOE_ALIGN)) * MOE_ALIGN
    starts, run = [], jnp.zeros((1, 1), F32)
    for e in range(n_rows):
        starts.append(run)
        run = run + seg[e:e + 1]
    pos = jnp.concatenate(starts, axis=0) + rank
    pos1 = jnp.sum(jnp.where(sel1, pos, 0.0), axis=0, keepdims=True)
    pos2 = jnp.sum(jnp.where(sel2, pos, 0.0), axis=0, keepdims=True)
    info = jnp.concatenate([pos1, pos2, 1.0 / den, e2 / den], axis=0)
    posr_ref[0] = jnp.concatenate([info, jnp.zeros((posr_ref.shape[1] - 4, info.shape[1]), F32)], axis=0)
    posc_ref[...] = jnp.concatenate([info, jnp.zeros((LANES - 4, info.shape[1]), F32)], axis=0).T
    cnt_ref[0] = jnp.broadcast_to(cnt[0:cnt_ref.shape[1]], cnt_ref.shape[1:])


def _moe_router(x, y, w_o, g, router, router_b, tri):
    n, d = x.shape
    tm = MOE_TT
    tile = lambda w: pl.BlockSpec((tm, w), lambda i: (i, 0))
    full = lambda arr: pl.BlockSpec(arr.shape, lambda i: (0,) * arr.ndim)
    small = lambda w: pl.BlockSpec((1, 8, w), lambda i: (i, 0, 0))
    return pl.pallas_call(
        _moe_router_kernel,
        grid=(n // tm,),
        in_specs=[tile(d), tile(d), full(w_o), full(g), full(router), full(router_b), full(tri)],
        out_specs=[tile(d), tile(d), tile(LANES), small(tm), small(LANES)],
        out_shape=[jax.ShapeDtypeStruct((n, d), F32), jax.ShapeDtypeStruct((n, d), BF16),
                   jax.ShapeDtypeStruct((n, LANES), F32), jax.ShapeDtypeStruct((n // tm, 8, tm), F32),
                   jax.ShapeDtypeStruct((n // tm, 8, LANES), F32)],
        compiler_params=_cparams(("arbitrary",)),
        name="moe_router",
    )(x, y, w_o, g, router, router_b, tri)


def _moe_expert_kernel(start_ref, cnt_ref, x_ref, hn_ref, posc_ref, posr_ref, wg_ref, wu_ref, wd_ref, gout_ref,
                       o_ref, xs_ref, ys_ref, *, n_e):
    i, e = pl.program_id(0), pl.program_id(1)
    tt, ch, blk = MOE_TT, MOE_CH, MOE_BLK
    n_sorted = xs_ref.shape[0]

    @pl.when(e == 0)
    def _():
        pos1, pos2 = posr_ref[0, 0:1, :], posr_ref[0, 1:2, :]
        hn = hn_ref[...]
        for rb in range(n_sorted // blk):
            srow = (lax.broadcasted_iota(jnp.int32, (blk, tt), 0) + rb * blk).astype(F32)
            p = jnp.where((srow == pos1) | (srow == pos2), 1.0, 0.0).astype(BF16)
            xs_ref[rb * blk:(rb + 1) * blk, :] = jnp.dot(p, hn, preferred_element_type=F32).astype(BF16)
        ys_ref[...] = jnp.zeros_like(ys_ref)

    start, count = start_ref[i, e], cnt_ref[i, e]
    row_in_chunk = lax.broadcasted_iota(jnp.int32, (ch, 1), 0)

    def chunk(k, carry):
        r0 = pl.multiple_of(start + k * ch, MOE_ALIGN)
        y = _swiglu(xs_ref[pl.ds(r0, ch), :], wg_ref[0], wu_ref[0], wd_ref[0])
        ys_ref[pl.ds(r0, ch), :] = jnp.where(row_in_chunk < count - k * ch, y, 0.0).astype(BF16)
        return carry

    lax.fori_loop(0, (count + (ch - 1)) // ch, chunk, 0)

    @pl.when(e == n_e - 1)
    def _():
        ys = ys_ref[...]
        for tb in range(tt // blk):
            ts = slice(tb * blk, (tb + 1) * blk)
            pc = posc_ref[ts, :]
            scol = lax.broadcasted_iota(jnp.int32, (blk, n_sorted), 1).astype(F32)
            pw = (jnp.where(scol == pc[:, 0:1], pc[:, 2:3], 0.0)
                  + jnp.where(scol == pc[:, 1:2], pc[:, 3:4], 0.0)).astype(BF16)
            moe = jnp.dot(pw, ys, preferred_element_type=F32)
            o_ref[ts, :] = _rms(x_ref[ts, :] + moe, gout_ref[...])


def _moe_experts(start, count, x, hn, posc, posr, wg, wu, wd, g_out):
    n, d = x.shape
    n_e, _, dff = wg.shape
    tt = MOE_TT
    tile = lambda w: pl.BlockSpec((tt, w), lambda i, e, *_: (i, 0))
    grid_spec = pltpu.PrefetchScalarGridSpec(
        num_scalar_prefetch=2,
        grid=(n // tt, n_e),
        in_specs=[tile(d), tile(d), tile(LANES), pl.BlockSpec((1, 8, tt), lambda i, e, *_: (i, 0, 0)),
                  pl.BlockSpec((1, d, dff), lambda i, e, *_: (e, 0, 0)),
                  pl.BlockSpec((1, d, dff), lambda i, e, *_: (e, 0, 0)),
                  pl.BlockSpec((1, dff, d), lambda i, e, *_: (e, 0, 0)),
                  pl.BlockSpec((1, d), lambda i, e, *_: (0, 0))],
        out_specs=tile(d),
        scratch_shapes=[pltpu.VMEM((MOE_SORTED, d), BF16), pltpu.VMEM((MOE_SORTED, d), BF16)],
    )
    return pl.pallas_call(
        functools.partial(_moe_expert_kernel, n_e=n_e),
        grid_spec=grid_spec,
        out_shape=jax.ShapeDtypeStruct((n, d), F32),
        compiler_params=_cparams(("arbitrary", "arbitrary")),
        name="moe_experts",
    )(start, count, x, hn, posc, posr, wg, wu, wd, g_out)


def _rwkv_proj_kernel(h_ref, hp_ref, g_ref, mix_ref, wr_ref, wk_ref, wv_ref, w1_ref, w2_ref,
                      a1_ref, a2_ref, g1_ref, g2_ref, w0_ref, a0_ref, kk_ref, ka_ref, rk_ref,
                      hsum_ref,
                      r_out, k_out, v_out, kk_out, bb_out, lw_out, g_out, bonus_out,
                      *, tiles_per_seq):
    i = pl.program_id(0)
    g = g_ref[...]
    hn = _rms(h_ref[...], g)
    prev = _rms(hp_ref[7:8, :], g)
    prev = jnp.where(i % tiles_per_seq == 0, 0.0, prev)
    row = lax.broadcasted_iota(jnp.int32, hn.shape, 0)
    shifted = jnp.where(row == 0, prev, pltpu.roll(hn, 1, axis=0))
    dx = shifted - hn
    xr, xw, xk, xv, xa, xg = (hn + dx * mix_ref[m:m + 1, :] for m in range(6))
    r = _dot(xr, wr_ref[...])
    k = _dot(xk, wk_ref[...])
    v = _dot(xv, wv_ref[...])
    wl = w0_ref[...] + _dot(jnp.tanh(_dot(xw, w1_ref[...])), w2_ref[...])
    nwl = -wl
    w_log = -(jnp.maximum(nwl, 0.0) + jnp.log(1.0 + jnp.exp(-jnp.abs(nwl)))) - 0.5
    a = _sigmoid(a0_ref[...] + _dot(_dot(xa, a1_ref[...]), a2_ref[...]))
    gate = _dot(_sigmoid(_dot(xg, g1_ref[...])), g2_ref[...])
    hsum = hsum_ref[...]

    def head_sum(z):
        w = hsum.shape[0]
        return jnp.concatenate([_dot(z[:, s:s + w], hsum) for s in range(0, z.shape[1], w)], axis=1)

    kk = k * kk_ref[...]
    kk = kk / jnp.maximum(jnp.sqrt(head_sum(kk * kk)), 1e-12)
    k2 = k * (1.0 + (a - 1.0) * ka_ref[...])
    r_out[...] = r.astype(r_out.dtype)
    k_out[...] = k2.astype(k_out.dtype)
    v_out[...] = v.astype(v_out.dtype)
    kk_out[...] = kk.astype(kk_out.dtype)
    bb_out[...] = (kk * a).astype(bb_out.dtype)
    lw_out[...] = -jnp.exp(w_log)
    g_out[...] = gate.astype(g_out.dtype)
    bonus_out[...] = (head_sum(r * k2 * rk_ref[...]) * v).astype(bonus_out.dtype)


def _rwkv_proj(h, t, g, mix8, wr, wk, wv, w1, w2, a1, a2, g1, g2, w0, a0, k_k, k_a, r_k, hsum):
    n, d = h.shape
    tm = PROJ_TM
    full = lambda arr: pl.BlockSpec(arr.shape, lambda i: (0,) * arr.ndim)
    tile = pl.BlockSpec((tm, d), lambda i: (i, 0))
    prev = pl.BlockSpec((8, d), lambda i: (jnp.maximum(i * (tm // 8) - 1, 0), 0))
    consts = [g, mix8, wr, wk, wv, w1, w2, a1, a2, g1, g2, w0, a0, k_k, k_a, r_k, hsum]
    return pl.pallas_call(
        functools.partial(_rwkv_proj_kernel, tiles_per_seq=t // tm),
        grid=(n // tm,),
        in_specs=[tile, prev] + [full(c) for c in consts],
        out_specs=[tile] * 8,
        out_shape=[jax.ShapeDtypeStruct((n, d), F32 if idx == 5 else BF16) for idx in range(8)],
        compiler_params=_cparams(("arbitrary",)),
        name="rwkv_proj",
    )(h, h, *consts)


def _rwkv_rec_kernel(r_ref, k_ref, v_ref, kk_ref, bb_ref, lw_ref, bonus_ref, gate_ref,
                     lng_ref, lnb_ref, o_ref, st_ref):
    c, hd, w = REC_C, RWKV_HEAD, REC_W
    bsz, rows, d = r_ref.shape
    n_u = d // w
    chunks = [slice(ci * c, (ci + 1) * c) for ci in range(rows // c)]
    units = [(ci, b, u) for ci in range(len(chunks)) for b in range(bsz) for u in range(n_u)]

    @pl.when(pl.program_id(0) == 0)
    def _():
        st_ref[...] = jnp.zeros_like(st_ref)

    rr, cc = _iota2((c, c))
    cum_m = jnp.where(cc <= rr, 1.0, 0.0).astype(BF16)
    lane_c = lax.broadcasted_iota(jnp.int32, (c, w), 1)
    row_c = lax.broadcasted_iota(jnp.int32, (c, w), 0)
    head_c = lane_c >> (hd.bit_length() - 1)
    colm = lane_c & (hd - 1)
    strict = colm < row_c
    incl = colm <= row_c
    eye_cat = jnp.where(colm == row_c, 1.0, 0.0)
    r2, c2 = _iota2((w, w))
    bd = _same_block(r2, c2, hd)
    hmean = jnp.where(bd, 1.0 / hd, 0.0).astype(BF16)

    def stack(xm):
        return jnp.concatenate([jnp.where(head_c == h, xm, 0.0) for h in range(w // hd)], axis=0)

    prep = {}
    for ci, cs in enumerate(chunks):
        for b in range(bsz):
            lw = lw_ref[b, cs, :]
            lp = _dot_exact_lhs(cum_m, lw)
            pend = lp[c - 1:c, :]
            e_neg = jnp.exp(-lp)
            e_end = jnp.exp(pend - lp)
            kb, bbv = k_ref[b, cs, :].astype(F32), bb_ref[b, cs, :].astype(F32)
            prep[ci, b] = dict(rt=r_ref[b, cs, :].astype(F32) * jnp.exp(lp),
                               kt=kk_ref[b, cs, :].astype(F32) * jnp.exp(lp - lw),
                               kh=kb * e_neg, bh=bbv * e_neg, khp=kb * e_end, bhp=bbv * e_end,
                               p_end=jnp.exp(pend), v=v_ref[b, cs, :])

    def usl(name, ci, b, u):
        return prep[ci, b][name][:, u * w:(u + 1) * w]

    m_kk, m_kb, a_rk, a_rb = [], [], [], []
    for unit in units:
        lhs = jnp.concatenate([usl('kt', *unit), usl('rt', *unit)], axis=0)
        rhs = jnp.concatenate([stack(usl('kh', *unit)), stack(usl('bh', *unit))], axis=0)
        gram = _dot_nt(lhs, rhs)
        m_kk.append(jnp.where(strict, gram[0:c, 0:w], 0.0))
        m_kb.append(jnp.where(strict, gram[0:c, w:2 * w], 0.0))
        a_rk.append(jnp.where(incl, gram[c:2 * c, 0:w], 0.0))
        a_rb.append(jnp.where(incl, gram[c:2 * c, w:2 * w], 0.0))
    tinv = [eye_cat - m for m in m_kb]
    pw = [_dot(m, stack(m)) for m in m_kb]
    mv = [_dot(jnp.concatenate([m_kk[i], a_rk[i]], axis=0), stack(usl('v', *unit)))
          for i, unit in enumerate(units)]
    n_lvl = (c - 1).bit_length() - 1
    for lvl in range(n_lvl):
        last = lvl == n_lvl - 1
        for i in range(len(units)):
            spw = stack(pw[i])
            if last:
                tinv[i] = tinv[i] + _dot(tinv[i], spw)
            else:
                res = _dot(jnp.concatenate([tinv[i], pw[i]], axis=0), spw)
                tinv[i] = tinv[i] + res[0:c]
                pw[i] = res[c:2 * c]
    ys = []
    for ci in range(len(chunks)):
        mine = [(i, unit) for i, unit in enumerate(units) if unit[0] == ci]
        ka = {i: _dot_nt(jnp.concatenate([usl('kt', *unit), usl('rt', *unit)], axis=0), st_ref[unit[1], unit[2]])
              for i, unit in mine}
        uu = {i: _dot(tinv[i], stack(ka[i][0:c] + mv[i][0:c])) for i, _ in mine}
        for i, unit in mine:
            _, b, u = unit
            ys.append(ka[i][c:2 * c] + mv[i][c:2 * c] - _dot(a_rb[i], stack(uu[i])))
            inc = _dot_tn(jnp.concatenate([usl('v', *unit).astype(F32), uu[i]], axis=0),
                          jnp.concatenate([usl('khp', *unit), -usl('bhp', *unit)], axis=0))
            st_ref[b, u] = st_ref[b, u] * usl('p_end', *unit) + jnp.where(bd, inc, 0.0)
    y_all = jnp.concatenate(ys, axis=0)
    dlt = y_all - _dot(y_all, hmean)
    yn_all = dlt * lax.rsqrt(_dot(dlt * dlt, hmean) + GN_EPS)
    for i, (ci, b, u) in enumerate(units):
        ls, cs = slice(u * w, (u + 1) * w), chunks[ci]
        yn = yn_all[i * c:(i + 1) * c] * lng_ref[:, ls] + lnb_ref[:, ls]
        o_ref[b, cs, ls] = ((yn + bonus_ref[b, cs, ls].astype(F32))
                            * gate_ref[b, cs, ls].astype(F32)).astype(o_ref.dtype)


def _rwkv_rec(r, k, v, kk, bb, lw, bonus, gate, ln_g, ln_b):
    bsz, t, d = r.shape
    rows = REC_C * REC_NCH
    blk = pl.BlockSpec((bsz, rows, d), lambda i: (0, i, 0))
    vec = pl.BlockSpec((1, d), lambda i: (0, 0))
    return pl.pallas_call(
        _rwkv_rec_kernel,
        grid=(t // rows,),
        in_specs=[blk] * 8 + [vec, vec],
        out_specs=blk,
        out_shape=jax.ShapeDtypeStruct((bsz, t, d), BF16),
        scratch_shapes=[pltpu.VMEM((bsz, d // REC_W, REC_W, REC_W), F32)],
        compiler_params=_cparams(("arbitrary",)),
        name="rwkv_rec",
    )(r, k, v, kk, bb, lw, bonus, gate, ln_g, ln_b)


def kernel(x, norm_mix_g, norm_ffn_g, norm_out_g, mix_w_in, mix_w_out, gmlp_norm_g, gmlp_w_s, gmlp_b_s, hgrn_lb_logits, hgrn_onorm_g, ffn_w_gate, ffn_w_up, ffn_w_down, rwkv_mix, rwkv_w_r, rwkv_w_k, rwkv_w_v, rwkv_w_o, rwkv_w0, rwkv_w1, rwkv_w2, rwkv_a0, rwkv_a1, rwkv_a2, rwkv_g1, rwkv_g2, rwkv_k_k, rwkv_k_a, rwkv_r_k, rwkv_ln_g, rwkv_ln_b, moe_router, moe_router_b, moe_w_gate, moe_w_up, moe_w_down):
    bsz, t, d = x.shape
    n = bsz * t
    row = lambda vec: vec.reshape(1, -1).astype(F32)
    bf = lambda w: w.astype(BF16)

    lower_bounds = jnp.cumsum(jax.nn.softmax(hgrn_lb_logits.astype(F32), axis=0), axis=0)
    bias_b = jnp.repeat(gmlp_b_s[0].astype(F32).T, GMLP_DIM, axis=1)
    h = _mixer0(x, row(norm_mix_g[0]), bf(mix_w_in[0]), bf(mix_w_out[0]), row(gmlp_norm_g[0]),
                gmlp_w_s[0].astype(F32), bias_b, row(lower_bounds[0]), row(hgrn_onorm_g[0]))
    h = h.reshape(n, d)
    h = _ffn(h, row(norm_ffn_g[0]), bf(ffn_w_gate[0]), bf(ffn_w_up[0]), bf(ffn_w_down[0]))

    mix8 = jnp.concatenate([rwkv_mix[0].astype(F32), jnp.zeros((2, d), F32)], axis=0)
    head_id = jnp.arange(REC_W) // RWKV_HEAD
    hsum = (head_id[:, None] == head_id[None, :]).astype(BF16)
    r, k2, v, kk, bb, lw, gate, bonus = _rwkv_proj(
        h, t, row(norm_mix_g[1]), mix8, bf(rwkv_w_r[0]), bf(rwkv_w_k[0]), bf(rwkv_w_v[0]),
        bf(rwkv_w1[0]), bf(rwkv_w2[0]), bf(rwkv_a1[0]), bf(rwkv_a2[0]), bf(rwkv_g1[0]), bf(rwkv_g2[0]),
        row(rwkv_w0[0]), row(rwkv_a0[0]), row(rwkv_k_k[0]), row(rwkv_k_a[0]), row(rwkv_r_k[0]), hsum)
    sh = lambda z: z.reshape(bsz, t, d)
    yg = _rwkv_rec(sh(r), sh(k2), sh(v), sh(kk), sh(bb), sh(lw), sh(bonus), sh(gate),
                   row(rwkv_ln_g[0]), row(rwkv_ln_b[0]))
    router = jnp.zeros((MOE_ROWS, d), F32).at[:N_EXPERTS].set(moe_router[0].astype(F32).T)
    router_b = jnp.full((MOE_ROWS,), -1e30, F32).at[:N_EXPERTS].set(moe_router_b[0].astype(F32))
    router_b = jnp.broadcast_to(router_b[:, None], (MOE_ROWS, MOE_TT))
    tok = jnp.arange(MOE_TT)
    tri = (tok[:, None] < tok[None, :]).astype(BF16)
    h, hn, posc, posr, counts = _moe_router(h, yg.reshape(n, d), bf(rwkv_w_o[0]), row(norm_ffn_g[1]),
                                            router, router_b, tri)
    count = counts[:, :N_EXPERTS, 0].astype(jnp.int32)
    seg = (count + (MOE_ALIGN - 1)) // MOE_ALIGN * MOE_ALIGN
    start = jnp.cumsum(seg, axis=1) - seg
    out = _moe_experts(start, count, h, hn, posc, posr, bf(moe_w_gate[0]), bf(moe_w_up[0]), bf(moe_w_down[0]),
                       row(norm_out_g))
    return out.reshape(bsz, t, d)
```

```python
import functools

import jax
import jax.numpy as jnp
from jax import lax
from jax.experimental import pallas as pl
from jax.experimental.pallas import tpu as pltpu

F32 = jnp.float32
BF16 = jnp.bfloat16

D_MODEL = 1024
GMLP_GROUPS = 4
GMLP_DIM = 128
GMLP_CHUNK = 128
HGRN_HEADS = 4
HGRN_DK = 128
HGRN_CHUNK = 32
MIX_HALF = 512
MIX_IN = 3072
RWKV_HEAD = 64
N_EXPERTS = 8
D_FF_EXPERT = 1408
RMS_EPS = 1e-6
LN_EPS = 1e-5
GN_EPS = 64e-5

LANES = 128
VMEM_LIMIT_BYTES = 56 * 1024 * 1024

MIX_TM = 256
MIX_NT = 2
MIX_CUM_BLK = 256
MLP_TM = 512
MOE_TT = 1024
MOE_CH = 128
MOE_ALIGN = 16
MOE_ROWS = 16
MOE_BLK = 256
MOE_SORTED = -(-(2 * MOE_TT + N_EXPERTS * (MOE_ALIGN - 1) + MOE_CH - 1) // MOE_BLK) * MOE_BLK
PROJ_TM = 512
REC_C = 64
REC_NCH = 4
REC_HPU = 4
REC_W = REC_HPU * RWKV_HEAD


def _cparams(sem):
    return pltpu.CompilerParams(dimension_semantics=sem, vmem_limit_bytes=VMEM_LIMIT_BYTES)


def _dot(a, b):
    return jnp.dot(a.astype(BF16), b.astype(BF16), preferred_element_type=F32)


def _dot_nt(a, b):
    return lax.dot_general(a.astype(BF16), b.astype(BF16), (((1,), (1,)), ((), ())),
                           preferred_element_type=F32)


def _dot_tn(a, b):
    return lax.dot_general(a.astype(BF16), b.astype(BF16), (((0,), (0,)), ((), ())),
                           preferred_element_type=F32)


def _split(a):
    hi = a.astype(BF16)
    lo = (a - hi.astype(F32)).astype(BF16)
    return hi, lo


def _dot_exact_lhs(m_bf16, a):
    hi, lo = _split(a)
    return (jnp.dot(m_bf16, hi, preferred_element_type=F32)
            + jnp.dot(m_bf16, lo, preferred_element_type=F32))


def _dot3(a, b):
    a_hi, a_lo = _split(a)
    b_hi, b_lo = _split(b)
    return (jnp.dot(a_hi, b_hi, preferred_element_type=F32)
            + jnp.dot(a_hi, b_lo, preferred_element_type=F32)
            + jnp.dot(a_lo, b_hi, preferred_element_type=F32))


def _rms(x, g):
    return x * lax.rsqrt(jnp.mean(x * x, axis=-1, keepdims=True) + RMS_EPS) * g


def _sigmoid(x):
    return jax.nn.sigmoid(x)


def _silu(x):
    return x * jax.nn.sigmoid(x)


def _iota2(shape):
    return (lax.broadcasted_iota(jnp.int32, shape, 0), lax.broadcasted_iota(jnp.int32, shape, 1))


def _same_block(a, b, size):
    shift = size.bit_length() - 1
    assert 1 << shift == size
    return (a >> shift) == (b >> shift)


def _mixer0_row(row_idx, tile_idx, x_ref, g_ref, win_ref, wout_ref, gng_ref, ws_ref, bias_ref, lb_ref, og_ref,
                o_ref, st_ref):
    tm = MIX_TM
    rows = slice(tile_idx * tm, (tile_idx + 1) * tm)
    x = x_ref[row_idx, rows, :]
    z = _dot(_rms(x, g_ref[...]), win_ref[...])
    yield

    u = jax.nn.gelu(z[:, 0:MIX_HALF])
    v = jax.nn.gelu(z[:, MIX_HALF:2 * MIX_HALF])
    row, col = _iota2((GMLP_CHUNK, GMLP_CHUNK))
    tril = col <= row
    mixed_groups = []
    for g in range(GMLP_GROUPS):
        gs = slice(g * GMLP_DIM, (g + 1) * GMLP_DIM)
        vg = v[:, gs]
        mu = jnp.mean(vg, axis=-1, keepdims=True)
        d = vg - mu
        var = jnp.mean(d * d, axis=-1, keepdims=True)
        vn = (d * lax.rsqrt(var + LN_EPS) * gng_ref[:, gs]).astype(BF16)
        wg = jnp.where(tril, ws_ref[g], 0.0).astype(BF16)
        parts = [jnp.dot(wg, vn[c * GMLP_CHUNK:(c + 1) * GMLP_CHUNK], preferred_element_type=F32)
                 for c in range(tm // GMLP_CHUNK)]
        mixed_groups.append(jnp.concatenate(parts, axis=0))
    bias = jnp.concatenate([bias_ref[...]] * (tm // GMLP_CHUNK), axis=0)
    y_a = u * (jnp.concatenate(mixed_groups, axis=1) + bias)
    yield

    o0 = 2 * MIX_HALF
    zq = z[:, o0:o0 + MIX_HALF]
    zf = z[:, o0 + MIX_HALF:o0 + 2 * MIX_HALF]
    zi = z[:, o0 + 2 * MIX_HALF:o0 + 3 * MIX_HALF]
    zg = z[:, o0 + 3 * MIX_HALF:o0 + 4 * MIX_HALF]
    lb = lb_ref[...]
    q = _silu(zq)
    f = lb + (1.0 - lb) * _sigmoid(zf)
    k = 1.0 - f
    lf = jnp.log(f)
    cb = MIX_CUM_BLK
    rr, cc = _iota2((cb, cb))
    same = _same_block(rr, cc, HGRN_CHUNK)
    cum_m = jnp.where(same & (cc <= rr), 1.0, 0.0).astype(BF16)
    end_m = jnp.where(same, 1.0, 0.0).astype(BF16)
    b = jnp.concatenate([_dot_exact_lhs(cum_m, lf[s:s + cb]) for s in range(0, tm, cb)], axis=0)
    b_end = jnp.concatenate([_dot_exact_lhs(end_m, lf[s:s + cb]) for s in range(0, tm, cb)], axis=0)
    q_dec = (q * jnp.exp(b)).astype(BF16)
    k_inv = (k * jnp.exp(-b)).astype(BF16)
    k_end = (k * jnp.exp(b_end - b)).astype(BF16)
    dec = jnp.exp(b_end)
    vb = zi.astype(BF16)
    yield

    r128, c128 = _iota2((128, 128))
    intra_mask = _same_block(r128, c128, HGRN_CHUNK) & (c128 <= r128)
    og = og_ref[...]
    heads = [slice(h * HGRN_DK, (h + 1) * HGRN_DK) for h in range(HGRN_HEADS)]
    chunks = [slice(c * HGRN_CHUNK, (c + 1) * HGRN_CHUNK) for c in range(tm // HGRN_CHUNK)]
    inc_t = [[_dot_tn(vb[cs, hs], k_end[cs, hs]) for cs in chunks] for hs in heads]
    scores = [[jnp.where(intra_mask, _dot_nt(q_dec[rs, hs], k_inv[rs, hs]), 0.0)
               for rs in (slice(rb * 128, (rb + 1) * 128) for rb in range(tm // 128))] for hs in heads]
    yield
    st_in = []
    for h, hs in enumerate(heads):
        st, per_chunk = st_ref[row_idx, h], []
        for c, cs in enumerate(chunks):
            per_chunk.append(st)
            st = st * dec[cs.start:cs.start + 1, hs] + inc_t[h][c]
        st_ref[row_idx, h] = st
        st_in.append(per_chunk)
    yield
    o_heads = []
    for h, hs in enumerate(heads):
        o_intra = jnp.concatenate([_dot(sc, vb[rb * 128:(rb + 1) * 128, hs])
                                   for rb, sc in enumerate(scores[h])], axis=0)
        o_inter = jnp.concatenate([_dot_nt(q_dec[cs, hs], st_in[h][c]) for c, cs in enumerate(chunks)], axis=0)
        o_h = o_intra + o_inter
        o_h = o_h * lax.rsqrt(jnp.mean(o_h * o_h, axis=-1, keepdims=True) + RMS_EPS) * og
        o_heads.append(o_h * _silu(zg[:, hs]))

    yield
    y = jnp.concatenate([y_a] + o_heads, axis=1)
    o_ref[row_idx, rows, :] = x + _dot(y, wout_ref[...])


def _mixer0_kernel(x_ref, *refs):
    st_ref = refs[-1]

    @pl.when(pl.program_id(0) == 0)
    def _():
        st_ref[...] = jnp.zeros_like(st_ref)

    bsz, step_rows, _ = x_ref.shape
    tiles = [_mixer0_row(b, ti, x_ref, *refs) for b in range(bsz) for ti in range(step_rows // MIX_TM)]
    while tiles:
        tiles = [tile for tile in tiles if next(tile, True) is None]


def _mixer0(x, g, w_in, w_out, gn_g, w_s, bias_b, lb, og):
    bsz, t, d = x.shape
    const = lambda shape: pl.BlockSpec(shape, lambda i: (0,) * len(shape))
    return pl.pallas_call(
        _mixer0_kernel,
        grid=(t // (MIX_TM * MIX_NT),),
        in_specs=[
            pl.BlockSpec((bsz, MIX_TM * MIX_NT, d), lambda i: (0, i, 0)),
            const((1, d)),
            const((d, MIX_IN)),
            const((2 * MIX_HALF, d)),
            const((1, MIX_HALF)),
            const((GMLP_GROUPS, GMLP_CHUNK, GMLP_CHUNK)),
            const((GMLP_CHUNK, MIX_HALF)),
            const((1, MIX_HALF)),
            const((1, HGRN_DK)),
        ],
        out_specs=pl.BlockSpec((bsz, MIX_TM * MIX_NT, d), lambda i: (0, i, 0)),
        out_shape=jax.ShapeDtypeStruct((bsz, t, d), F32),
        scratch_shapes=[pltpu.VMEM((bsz, HGRN_HEADS, HGRN_DK, HGRN_DK), F32)],
        compiler_params=_cparams(("arbitrary",)),
        name="mixer0",
    )(x, g, w_in, w_out, gn_g, w_s, bias_b, lb, og)


def _swiglu(xb, wg, wu, wd):
    a = jnp.dot(xb, wg, preferred_element_type=F32)
    b = jnp.dot(xb, wu, preferred_element_type=F32)
    return _dot(_silu(a) * b, wd)


def _ffn_kernel(x_ref, g_ref, wg_ref, wu_ref, wd_ref, o_ref):
    x = x_ref[...]
    o_ref[...] = x + _swiglu(_rms(x, g_ref[...]).astype(BF16), wg_ref[...], wu_ref[...], wd_ref[...])


def _ffn(x, g, wg, wu, wd):
    n, d = x.shape
    tm = MLP_TM
    resident = lambda w: pl.BlockSpec(w.shape, lambda i: (0, 0), pipeline_mode=pl.Buffered(1))
    return pl.pallas_call(
        _ffn_kernel,
        grid=(n // tm,),
        in_specs=[pl.BlockSpec((tm, d), lambda i: (i, 0)),
                  pl.BlockSpec((1, d), lambda i: (0, 0)),
                  resident(wg), resident(wu), resident(wd)],
        out_specs=pl.BlockSpec((tm, d), lambda i: (i, 0)),
        out_shape=jax.ShapeDtypeStruct((n, d), F32),
        compiler_params=_cparams(("arbitrary",)),
        name="ffn_mlp",
    )(x, g, wg, wu, wd)


def _moe_router_kernel(x_ref, y_ref, wo_ref, g_ref, router_ref, rb_ref, tri_ref,
                       h_ref, hn_ref, posc_ref, posr_ref, cnt_ref):
    h = x_ref[...] + jnp.dot(y_ref[...], wo_ref[...], preferred_element_type=F32)
    h_ref[...] = h
    hn = _rms(h, g_ref[...])
    hn_ref[...] = hn.astype(BF16)
    hn_hi, hn_lo = _split(hn)
    r_hi, r_lo = _split(router_ref[...])
    nt = (((1,), (1,)), ((), ()))
    logits = (lax.dot_general(r_hi, hn_hi, nt, preferred_element_type=F32)
              + lax.dot_general(r_hi, hn_lo, nt, preferred_element_type=F32)
              + lax.dot_general(r_lo, hn_hi, nt, preferred_element_type=F32)) + rb_ref[...]
    n_rows = logits.shape[0]
    sub = lax.broadcasted_iota(jnp.int32, logits.shape, 0)
    m1 = jnp.max(logits, axis=0, keepdims=True)
    i1 = jnp.min(jnp.where(logits == m1, sub, n_rows), axis=0, keepdims=True)
    rest = jnp.where(sub == i1, -jnp.inf, logits)
    m2 = jnp.max(rest, axis=0, keepdims=True)
    i2 = jnp.min(jnp.where(rest == m2, sub, n_rows), axis=0, keepdims=True)
    e2 = jnp.exp(m2 - m1)
    den = 1.0 + e2
    sel1, sel2 = sub == i1, sub == i2
    sel_f = jnp.where(sel1 | sel2, 1.0, 0.0)
    rank = jnp.dot(sel_f.astype(BF16), tri_ref[...], preferred_element_type=F32)
    cnt = jnp.sum(sel_f, axis=1, keepdims=True)
    seg = jnp.floor((cnt + (MOE_ALIGN - 1)) * (1.0 / MOE_ALIGN)) * MOE_ALIGN
    starts, run = [], jnp.zeros((1, 1), F32)
    for e in range(n_rows):
        starts.append(run)
        run = run + seg[e:e + 1]
    pos = jnp.concatenate(starts, axis=0) + rank
    pos1 = jnp.sum(jnp.where(sel1, pos, 0.0), axis=0, keepdims=True)
    pos2 = jnp.sum(jnp.where(sel2, pos, 0.0), axis=0, keepdims=True)
    info = jnp.concatenate([pos1, pos2, 1.0 / den, e2 / den], axis=0)
    posr_ref[0] = jnp.concatenate([info, jnp.zeros((posr_ref.shape[1] - 4, info.shape[1]), F32)], axis=0)
    posc_ref[...] = jnp.concatenate([info, jnp.zeros((LANES - 4, info.shape[1]), F32)], axis=0).T
    cnt_ref[0] = jnp.broadcast_to(cnt[0:cnt_ref.shape[1]], cnt_ref.shape[1:])


def _moe_router(x, y, w_o, g, router, router_b, tri):
    n, d = x.shape
    tm = MOE_TT
    tile = lambda w: pl.BlockSpec((tm, w), lambda i: (i, 0))
    full = lambda arr: pl.BlockSpec(arr.shape, lambda i: (0,) * arr.ndim)
    small = lambda w: pl.BlockSpec((1, 8, w), lambda i: (i, 0, 0))
    return pl.pallas_call(
        _moe_router_kernel,
        grid=(n // tm,),
        in_specs=[tile(d), tile(d), full(w_o), full(g), full(router), full(router_b), full(tri)],
        out_specs=[tile(d), tile(d), tile(LANES), small(tm), small(LANES)],
        out_shape=[jax.ShapeDtypeStruct((n, d), F32), jax.ShapeDtypeStruct((n, d), BF16),
                   jax.ShapeDtypeStruct((n, LANES), F32), jax.ShapeDtypeStruct((n // tm, 8, tm), F32),
                   jax.ShapeDtypeStruct((n // tm, 8, LANES), F32)],
        compiler_params=_cparams(("arbitrary",)),
        name="moe_router",
    )(x, y, w_o, g, router, router_b, tri)


def _moe_expert_kernel(start_ref, cnt_ref, x_ref, hn_ref, posc_ref, posr_ref, wg_ref, wu_ref, wd_ref, gout_ref,
                       o_ref, xs_ref, ys_ref, *, n_e):
    i, e = pl.program_id(0), pl.program_id(1)
    tt, ch, blk = MOE_TT, MOE_CH, MOE_BLK
    n_sorted = xs_ref.shape[0]

    @pl.when(e == 0)
    def _():
        pos1, pos2 = posr_ref[0, 0:1, :], posr_ref[0, 1:2, :]
        hn = hn_ref[...]
        for rb in range(n_sorted // blk):
            srow = (lax.broadcasted_iota(jnp.int32, (blk, tt), 0) + rb * blk).astype(F32)
            p = jnp.where((srow == pos1) | (srow == pos2), 1.0, 0.0).astype(BF16)
            xs_ref[rb * blk:(rb + 1) * blk, :] = jnp.dot(p, hn, preferred_element_type=F32).astype(BF16)
        ys_ref[...] = jnp.zeros_like(ys_ref)

    start, count = start_ref[i, e], cnt_ref[i, e]
    row_in_chunk = lax.broadcasted_iota(jnp.int32, (ch, 1), 0)

    def chunk(k):
        r0 = pl.multiple_of(start + k * ch, MOE_ALIGN)
        xs = xs_ref[pl.ds(r0, ch), :]
        a = jnp.dot(xs, wg_ref[0], preferred_element_type=F32)
        b = jnp.dot(xs, wu_ref[0], preferred_element_type=F32)
        yield
        y = _dot(_silu(a) * b, wd_ref[0])
        yield
        ys_ref[pl.ds(r0, ch), :] = jnp.where(row_in_chunk < count - k * ch, y, 0.0).astype(BF16)

    def run(progs):
        while progs:
            progs = [p for p in progs if next(p, True) is None]

    def pair(kk, carry):
        run([chunk(2 * kk), chunk(2 * kk + 1)])
        return carry

    n_chunks = lax.shift_right_logical(count + (ch - 1), ch.bit_length() - 1)
    lax.fori_loop(0, lax.shift_right_logical(n_chunks, 1), pair, 0)

    @pl.when((n_chunks & 1) == 1)
    def _():
        run([chunk(n_chunks - 1)])

    @pl.when(e == n_e - 1)
    def _():
        ys = ys_ref[...]
        for tb in range(tt // blk):
            ts = slice(tb * blk, (tb + 1) * blk)
            pc = posc_ref[ts, :]
            scol = lax.broadcasted_iota(jnp.int32, (blk, n_sorted), 1).astype(F32)
            pw = (jnp.where(scol == pc[:, 0:1], pc[:, 2:3], 0.0)
                  + jnp.where(scol == pc[:, 1:2], pc[:, 3:4], 0.0)).astype(BF16)
            moe = jnp.dot(pw, ys, preferred_element_type=F32)
            o_ref[ts, :] = _rms(x_ref[ts, :] + moe, gout_ref[...])


def _moe_experts(start, count, x, hn, posc, posr, wg, wu, wd, g_out):
    n, d = x.shape
    n_e, _, dff = wg.shape
    tt = MOE_TT
    tile = lambda w: pl.BlockSpec((tt, w), lambda i, e, *_: (i, 0))
    grid_spec = pltpu.PrefetchScalarGridSpec(
        num_scalar_prefetch=2,
        grid=(n // tt, n_e),
        in_specs=[tile(d), tile(d), tile(LANES), pl.BlockSpec((1, 8, tt), lambda i, e, *_: (i, 0, 0)),
                  pl.BlockSpec((1, d, dff), lambda i, e, *_: (e, 0, 0)),
                  pl.BlockSpec((1, d, dff), lambda i, e, *_: (e, 0, 0)),
                  pl.BlockSpec((1, dff, d), lambda i, e, *_: (e, 0, 0)),
                  pl.BlockSpec((1, d), lambda i, e, *_: (0, 0))],
        out_specs=tile(d),
        scratch_shapes=[pltpu.VMEM((MOE_SORTED, d), BF16), pltpu.VMEM((MOE_SORTED, d), BF16)],
    )
    return pl.pallas_call(
        functools.partial(_moe_expert_kernel, n_e=n_e),
        grid_spec=grid_spec,
        out_shape=jax.ShapeDtypeStruct((n, d), F32),
        compiler_params=_cparams(("arbitrary", "arbitrary")),
        name="moe_experts",
    )(start, count, x, hn, posc, posr, wg, wu, wd, g_out)


def _rwkv_proj_kernel(h_ref, hp_ref, g_ref, mix_ref, wr_ref, wk_ref, wv_ref, w1_ref, w2_ref,
                      a1_ref, a2_ref, g1_ref, g2_ref, w0_ref, a0_ref, kk_ref, ka_ref, rk_ref,
                      hsum_ref,
                      r_out, k_out, v_out, kk_out, bb_out, lw_out, g_out, bonus_out,
                      *, tiles_per_seq):
    i = pl.program_id(0)
    g = g_ref[...]
    hn = _rms(h_ref[...], g)
    prev = _rms(hp_ref[7:8, :], g)
    prev = jnp.where(i % tiles_per_seq == 0, 0.0, prev)
    row = lax.broadcasted_iota(jnp.int32, hn.shape, 0)
    shifted = jnp.where(row == 0, prev, pltpu.roll(hn, 1, axis=0))
    dx = shifted - hn
    xr, xw, xk, xv, xa, xg = (hn + dx * mix_ref[m:m + 1, :] for m in range(6))
    r = _dot(xr, wr_ref[...])
    k = _dot(xk, wk_ref[...])
    v = _dot(xv, wv_ref[...])
    wl = w0_ref[...] + _dot(jnp.tanh(_dot(xw, w1_ref[...])), w2_ref[...])
    nwl = -wl
    w_log = -(jnp.maximum(nwl, 0.0) + jnp.log(1.0 + jnp.exp(-jnp.abs(nwl)))) - 0.5
    a = _sigmoid(a0_ref[...] + _dot(_dot(xa, a1_ref[...]), a2_ref[...]))
    gate = _dot(_sigmoid(_dot(xg, g1_ref[...])), g2_ref[...])
    hsum = hsum_ref[...]

    def head_sum(z):
        w = hsum.shape[0]
        return jnp.concatenate([_dot(z[:, s:s + w], hsum) for s in range(0, z.shape[1], w)], axis=1)

    kk = k * kk_ref[...]
    kk = kk / jnp.maximum(jnp.sqrt(head_sum(kk * kk)), 1e-12)
    k2 = k * (1.0 + (a - 1.0) * ka_ref[...])
    r_out[...] = r.astype(r_out.dtype)
    k_out[...] = k2.astype(k_out.dtype)
    v_out[...] = v.astype(v_out.dtype)
    kk_out[...] = kk.astype(kk_out.dtype)
    bb_out[...] = (kk * a).astype(bb_out.dtype)
    lw_out[...] = -jnp.exp(w_log)
    g_out[...] = gate.astype(g_out.dtype)
    bonus_out[...] = (head_sum(r * k2 * rk_ref[...]) * v).astype(bonus_out.dtype)


def _rwkv_proj(h, t, g, mix8, wr, wk, wv, w1, w2, a1, a2, g1, g2, w0, a0, k_k, k_a, r_k, hsum):
    n, d = h.shape
    tm = PROJ_TM
    full = lambda arr: pl.BlockSpec(arr.shape, lambda i: (0,) * arr.ndim)
    tile = pl.BlockSpec((tm, d), lambda i: (i, 0))
    prev = pl.BlockSpec((8, d), lambda i: (jnp.maximum(i * (tm // 8) - 1, 0), 0))
    consts = [g, mix8, wr, wk, wv, w1, w2, a1, a2, g1, g2, w0, a0, k_k, k_a, r_k, hsum]
    return pl.pallas_call(
        functools.partial(_rwkv_proj_kernel, tiles_per_seq=t // tm),
        grid=(n // tm,),
        in_specs=[tile, prev] + [full(c) for c in consts],
        out_specs=[tile] * 8,
        out_shape=[jax.ShapeDtypeStruct((n, d), F32 if idx == 5 else BF16) for idx in range(8)],
        compiler_params=_cparams(("arbitrary",)),
        name="rwkv_proj",
    )(h, h, *consts)


def _rwkv_rec_kernel(r_ref, k_ref, v_ref, kk_ref, bb_ref, lw_ref, bonus_ref, gate_ref,
                     lng_ref, lnb_ref, o_ref, st_ref):
    c, hd, w = REC_C, RWKV_HEAD, REC_W
    bsz, rows, d = r_ref.shape
    n_u = d // w
    chunks = [slice(ci * c, (ci + 1) * c) for ci in range(rows // c)]
    units = [(ci, b, u) for ci in range(len(chunks)) for b in range(bsz) for u in range(n_u)]

    @pl.when(pl.program_id(0) == 0)
    def _():
        st_ref[...] = jnp.zeros_like(st_ref)

    rr, cc = _iota2((c, c))
    cum_m = jnp.where(cc <= rr, 1.0, 0.0).astype(BF16)
    lane_c = lax.broadcasted_iota(jnp.int32, (c, w), 1)
    row_c = lax.broadcasted_iota(jnp.int32, (c, w), 0)
    head_c = lane_c >> (hd.bit_length() - 1)
    colm = lane_c & (hd - 1)
    strict = colm < row_c
    incl = colm <= row_c
    eye_cat = jnp.where(colm == row_c, 1.0, 0.0)
    r2, c2 = _iota2((w, w))
    bd = _same_block(r2, c2, hd)
    hmean = jnp.where(bd, 1.0 / hd, 0.0).astype(BF16)

    def stack(xm):
        return jnp.concatenate([jnp.where(head_c == h, xm, 0.0) for h in range(w // hd)], axis=0)

    prep = {}
    for ci, cs in enumerate(chunks):
        for b in range(bsz):
            lw = lw_ref[b, cs, :]
            lp = _dot_exact_lhs(cum_m, lw)
            pend = lp[c - 1:c, :]
            e_neg = jnp.exp(-lp)
            e_end = jnp.exp(pend - lp)
            kb, bbv = k_ref[b, cs, :].astype(F32), bb_ref[b, cs, :].astype(F32)
            prep[ci, b] = dict(rt=r_ref[b, cs, :].astype(F32) * jnp.exp(lp),
                               kt=kk_ref[b, cs, :].astype(F32) * jnp.exp(lp - lw),
                               kh=kb * e_neg, bh=bbv * e_neg, khp=kb * e_end, bhp=bbv * e_end,
                               p_end=jnp.exp(pend), v=v_ref[b, cs, :])

    def usl(name, ci, b, u):
        return prep[ci, b][name][:, u * w:(u + 1) * w]

    m_kk, m_kb, a_rk, a_rb = [], [], [], []
    for unit in units:
        lhs = jnp.concatenate([usl('kt', *unit), usl('rt', *unit)], axis=0)
        rhs = jnp.concatenate([stack(usl('kh', *unit)), stack(usl('bh', *unit))], axis=0)
        gram = _dot_nt(lhs, rhs)
        m_kk.append(jnp.where(strict, gram[0:c, 0:w], 0.0))
        m_kb.append(jnp.where(strict, gram[0:c, w:2 * w], 0.0))
        a_rk.append(jnp.where(incl, gram[c:2 * c, 0:w], 0.0))
        a_rb.append(jnp.where(incl, gram[c:2 * c, w:2 * w], 0.0))
    tinv = [eye_cat - m for m in m_kb]
    pw = [_dot(m, stack(m)) for m in m_kb]
    mv = [_dot(jnp.concatenate([m_kk[i], a_rk[i]], axis=0), stack(usl('v', *unit)))
          for i, unit in enumerate(units)]
    n_lvl = (c - 1).bit_length() - 1
    for lvl in range(n_lvl):
        last = lvl == n_lvl - 1
        for i in range(len(units)):
            spw = stack(pw[i])
            if last:
                tinv[i] = tinv[i] + _dot(tinv[i], spw)
            else:
                res = _dot(jnp.concatenate([tinv[i], pw[i]], axis=0), spw)
                tinv[i] = tinv[i] + res[0:c]
                pw[i] = res[c:2 * c]
    ys = []
    for ci in range(len(chunks)):
        mine = [(i, unit) for i, unit in enumerate(units) if unit[0] == ci]
        ka = {i: _dot_nt(jnp.concatenate([usl('kt', *unit), usl('rt', *unit)], axis=0), st_ref[unit[1], unit[2]])
              for i, unit in mine}
        uu = {i: _dot(tinv[i], stack(ka[i][0:c] + mv[i][0:c])) for i, _ in mine}
        for i, unit in mine:
            _, b, u = unit
            ys.append(ka[i][c:2 * c] + mv[i][c:2 * c] - _dot(a_rb[i], stack(uu[i])))
            inc = _dot_tn(jnp.concatenate([usl('v', *unit).astype(F32), uu[i]], axis=0),
                          jnp.concatenate([usl('khp', *unit), -usl('bhp', *unit)], axis=0))
            st_ref[b, u] = st_ref[b, u] * usl('p_end', *unit) + jnp.where(bd, inc, 0.0)
    y_all = jnp.concatenate(ys, axis=0)
    dlt = y_all - _dot(y_all, hmean)
    yn_all = dlt * lax.rsqrt(_dot(dlt * dlt, hmean) + GN_EPS)
    for i, (ci, b, u) in enumerate(units):
        ls, cs = slice(u * w, (u + 1) * w), chunks[ci]
        yn = yn_all[i * c:(i + 1) * c] * lng_ref[:, ls] + lnb_ref[:, ls]
        o_ref[b, cs, ls] = ((yn + bonus_ref[b, cs, ls].astype(F32))
                            * gate_ref[b, cs, ls].astype(F32)).astype(o_ref.dtype)


def _rwkv_rec(r, k, v, kk, bb, lw, bonus, gate, ln_g, ln_b):
    bsz, t, d = r.shape
    rows = REC_C * REC_NCH
    blk = pl.BlockSpec((bsz, rows, d), lambda i: (0, i, 0))
    vec = pl.BlockSpec((1, d), lambda i: (0, 0))
    return pl.pallas_call(
        _rwkv_rec_kernel,
        grid=(t // rows,),
        in_specs=[blk] * 8 + [vec, vec],
        out_specs=blk,
        out_shape=jax.ShapeDtypeStruct((bsz, t, d), BF16),
        scratch_shapes=[pltpu.VMEM((bsz, d // REC_W, REC_W, REC_W), F32)],
        compiler_params=_cparams(("arbitrary",)),
        name="rwkv_rec",
    )(r, k, v, kk, bb, lw, bonus, gate, ln_g, ln_b)


def kernel(x, norm_mix_g, norm_ffn_g, norm_out_g, mix_w_in, mix_w_out, gmlp_norm_g, gmlp_w_s, gmlp_b_s, hgrn_lb_logits, hgrn_onorm_g, ffn_w_gate, ffn_w_up, ffn_w_down, rwkv_mix, rwkv_w_r, rwkv_w_k, rwkv_w_v, rwkv_w_o, rwkv_w0, rwkv_w1, rwkv_w2, rwkv_a0, rwkv_a1, rwkv_a2, rwkv_g1, rwkv_g2, rwkv_k_k, rwkv_k_a, rwkv_r_k, rwkv_ln_g, rwkv_ln_b, moe_router, moe_router_b, moe_w_gate, moe_w_up, moe_w_down):
    bsz, t, d = x.shape
    n = bsz * t
    row = lambda vec: vec.reshape(1, -1).astype(F32)
    bf = lambda w: w.astype(BF16)

    lower_bounds = jnp.cumsum(jax.nn.softmax(hgrn_lb_logits.astype(F32), axis=0), axis=0)
    bias_b = jnp.repeat(gmlp_b_s[0].astype(F32).T, GMLP_DIM, axis=1)
    h = _mixer0(x, row(norm_mix_g[0]), bf(mix_w_in[0]), bf(mix_w_out[0]), row(gmlp_norm_g[0]),
                gmlp_w_s[0].astype(F32), bias_b, row(lower_bounds[0]), row(hgrn_onorm_g[0]))
    h = h.reshape(n, d)
    h = _ffn(h, row(norm_ffn_g[0]), bf(ffn_w_gate[0]), bf(ffn_w_up[0]), bf(ffn_w_down[0]))

    mix8 = jnp.concatenate([rwkv_mix[0].astype(F32), jnp.zeros((2, d), F32)], axis=0)
    head_id = jnp.arange(REC_W) // RWKV_HEAD
    hsum = (head_id[:, None] == head_id[None, :]).astype(BF16)
    r, k2, v, kk, bb, lw, gate, bonus = _rwkv_proj(
        h, t, row(norm_mix_g[1]), mix8, bf(rwkv_w_r[0]), bf(rwkv_w_k[0]), bf(rwkv_w_v[0]),
        bf(rwkv_w1[0]), bf(rwkv_w2[0]), bf(rwkv_a1[0]), bf(rwkv_a2[0]), bf(rwkv_g1[0]), bf(rwkv_g2[0]),
        row(rwkv_w0[0]), row(rwkv_a0[0]), row(rwkv_k_k[0]), row(rwkv_k_a[0]), row(rwkv_r_k[0]), hsum)
    sh = lambda z: z.reshape(bsz, t, d)
    yg = _rwkv_rec(sh(r), sh(k2), sh(v), sh(kk), sh(bb), sh(lw), sh(bonus), sh(gate),
                   row(rwkv_ln_g[0]), row(rwkv_ln_b[0]))
    router = jnp.zeros((MOE_ROWS, d), F32).at[:N_EXPERTS].set(moe_router[0].astype(F32).T)
    router_b = jnp.full((MOE_ROWS,), -1e30, F32).at[:N_EXPERTS].set(moe_router_b[0].astype(F32))
    router_b = jnp.broadcast_to(router_b[:, None], (MOE_ROWS, MOE_TT))
    tok = jnp.arange(MOE_TT)
    tri = (tok[:, None] < tok[None, :]).astype(BF16)
    h, hn, posc, posr, counts = _moe_router(h, yg.reshape(n, d), bf(rwkv_w_o[0]), row(norm_ffn_g[1]),
                                            router, router_b, tri)
    count = counts[:, :N_EXPERTS, 0].astype(jnp.int32)
    seg = (count + (MOE_ALIGN - 1)) // MOE_ALIGN * MOE_ALIGN
    start = jnp.cumsum(seg, axis=1) - seg
    out = _moe_experts(start, count, h, hn, posc, posr, bf(moe_w_gate[0]), bf(moe_w_up[0]), bf(moe_w_down[0]),
                       row(norm_out_g))
    return out.reshape(bsz, t, d)
```

```python
import functools

import jax
import jax.numpy as jnp
from jax import lax
from jax.experimental import pallas as pl
from jax.experimental.pallas import tpu as pltpu

F32 = jnp.float32
BF16 = jnp.bfloat16

GMLP_GROUPS = 4
GMLP_DIM = 128
GMLP_CHUNK = 128
HGRN_HEADS = 4
HGRN_DK = 128
HGRN_CHUNK = 32
MIX_HALF = 512
MIX_IN = 3072
RWKV_HEAD = 64
N_EXPERTS = 8
RMS_EPS = 1e-6
LN_EPS = 1e-5
GN_EPS = 64e-5

LANES = 128
VMEM_LIMIT_BYTES = 56 * 1024 * 1024

MIX_TM = 256
MIX_NT = 2
MIX_CUM_BLK = 256
MLP_TM = 512
MOE_TT = 1024
MOE_CH = 128
MOE_ALIGN = 16
MOE_ROWS = 16
MOE_BLK = 256
MOE_SORTED = -(-(2 * MOE_TT + N_EXPERTS * (MOE_ALIGN - 1) + MOE_CH - 1) // MOE_BLK) * MOE_BLK
PROJ_TM = 512
REC_C = 64
REC_NCH = 4
REC_HPU = 4
REC_W = REC_HPU * RWKV_HEAD


def _cparams(sem):
    return pltpu.CompilerParams(dimension_semantics=sem, vmem_limit_bytes=VMEM_LIMIT_BYTES)


def _dot(a, b):
    return jnp.dot(a.astype(BF16), b.astype(BF16), preferred_element_type=F32)


def _dot_nt(a, b):
    return lax.dot_general(a.astype(BF16), b.astype(BF16), (((1,), (1,)), ((), ())),
                           preferred_element_type=F32)


def _dot_tn(a, b):
    return lax.dot_general(a.astype(BF16), b.astype(BF16), (((0,), (0,)), ((), ())),
                           preferred_element_type=F32)


def _split(a):
    hi = a.astype(BF16)
    lo = (a - hi.astype(F32)).astype(BF16)
    return hi, lo


def _dot_exact_lhs(m_bf16, a):
    hi, lo = _split(a)
    return (jnp.dot(m_bf16, hi, preferred_element_type=F32)
            + jnp.dot(m_bf16, lo, preferred_element_type=F32))


def _rms(x, g):
    return x * lax.rsqrt(jnp.mean(x * x, axis=-1, keepdims=True) + RMS_EPS) * g


def _sigmoid(x):
    return jax.nn.sigmoid(x)


def _silu(x):
    return x * jax.nn.sigmoid(x)


def _iota2(shape):
    return (lax.broadcasted_iota(jnp.int32, shape, 0), lax.broadcasted_iota(jnp.int32, shape, 1))


def _same_block(a, b, size):
    shift = size.bit_length() - 1
    assert 1 << shift == size
    return (a >> shift) == (b >> shift)


def _mixer0_row(row_idx, tile_idx, x_ref, g_ref, win_ref, wout_ref, gng_ref, ws_ref, bias_ref, lb_ref, og_ref,
                o_ref, st_ref):
    tm = MIX_TM
    rows = slice(tile_idx * tm, (tile_idx + 1) * tm)
    x = x_ref[row_idx, rows, :]
    z = _dot(_rms(x, g_ref[...]), win_ref[...])
    yield

    u = jax.nn.gelu(z[:, 0:MIX_HALF])
    v = jax.nn.gelu(z[:, MIX_HALF:2 * MIX_HALF])
    row, col = _iota2((GMLP_CHUNK, GMLP_CHUNK))
    tril = col <= row
    mixed_groups = []
    for g in range(GMLP_GROUPS):
        gs = slice(g * GMLP_DIM, (g + 1) * GMLP_DIM)
        vg = v[:, gs]
        mu = jnp.mean(vg, axis=-1, keepdims=True)
        d = vg - mu
        var = jnp.mean(d * d, axis=-1, keepdims=True)
        vn = (d * lax.rsqrt(var + LN_EPS) * gng_ref[:, gs]).astype(BF16)
        wg = jnp.where(tril, ws_ref[g], 0.0).astype(BF16)
        parts = [jnp.dot(wg, vn[c * GMLP_CHUNK:(c + 1) * GMLP_CHUNK], preferred_element_type=F32)
                 for c in range(tm // GMLP_CHUNK)]
        mixed_groups.append(jnp.concatenate(parts, axis=0))
    bias = jnp.concatenate([bias_ref[...]] * (tm // GMLP_CHUNK), axis=0)
    y_a = u * (jnp.concatenate(mixed_groups, axis=1) + bias)
    yield

    o0 = 2 * MIX_HALF
    zq = z[:, o0:o0 + MIX_HALF]
    zf = z[:, o0 + MIX_HALF:o0 + 2 * MIX_HALF]
    zi = z[:, o0 + 2 * MIX_HALF:o0 + 3 * MIX_HALF]
    zg = z[:, o0 + 3 * MIX_HALF:o0 + 4 * MIX_HALF]
    lb = lb_ref[...]
    q = _silu(zq)
    f = lb + (1.0 - lb) * _sigmoid(zf)
    k = 1.0 - f
    lf = jnp.log(f)
    cb = MIX_CUM_BLK
    rr, cc = _iota2((cb, cb))
    same = _same_block(rr, cc, HGRN_CHUNK)
    cum_m = jnp.where(same & (cc <= rr), 1.0, 0.0).astype(BF16)
    end_m = jnp.where(same, 1.0, 0.0).astype(BF16)
    b = jnp.concatenate([_dot_exact_lhs(cum_m, lf[s:s + cb]) for s in range(0, tm, cb)], axis=0)
    b_end = jnp.concatenate([_dot_exact_lhs(end_m, lf[s:s + cb]) for s in range(0, tm, cb)], axis=0)
    q_dec = (q * jnp.exp(b)).astype(BF16)
    k_inv = (k * jnp.exp(-b)).astype(BF16)
    k_end = (k * jnp.exp(b_end - b)).astype(BF16)
    dec = jnp.exp(b_end)
    vb = zi.astype(BF16)
    yield

    r128, c128 = _iota2((128, 128))
    intra_mask = _same_block(r128, c128, HGRN_CHUNK) & (c128 <= r128)
    og = og_ref[...]
    heads = [slice(h * HGRN_DK, (h + 1) * HGRN_DK) for h in range(HGRN_HEADS)]
    chunks = [slice(c * HGRN_CHUNK, (c + 1) * HGRN_CHUNK) for c in range(tm // HGRN_CHUNK)]
    inc_t = [[_dot_tn(vb[cs, hs], k_end[cs, hs]) for cs in chunks] for hs in heads]
    scores = [[jnp.where(intra_mask, _dot_nt(q_dec[rs, hs], k_inv[rs, hs]), 0.0)
               for rs in (slice(rb * 128, (rb + 1) * 128) for rb in range(tm // 128))] for hs in heads]
    yield
    st_in = []
    for h, hs in enumerate(heads):
        st, per_chunk = st_ref[row_idx, h], []
        for c, cs in enumerate(chunks):
            per_chunk.append(st)
            st = st * dec[cs.start:cs.start + 1, hs] + inc_t[h][c]
        st_ref[row_idx, h] = st
        st_in.append(per_chunk)
    yield
    o_heads = []
    for h, hs in enumerate(heads):
        o_intra = jnp.concatenate([_dot(sc, vb[rb * 128:(rb + 1) * 128, hs])
                                   for rb, sc in enumerate(scores[h])], axis=0)
        o_inter = jnp.concatenate([_dot_nt(q_dec[cs, hs], st_in[h][c]) for c, cs in enumerate(chunks)], axis=0)
        o_h = o_intra + o_inter
        o_h = o_h * lax.rsqrt(jnp.mean(o_h * o_h, axis=-1, keepdims=True) + RMS_EPS) * og
        o_heads.append(o_h * _silu(zg[:, hs]))

    yield
    y = jnp.concatenate([y_a] + o_heads, axis=1)
    o_ref[row_idx, rows, :] = x + _dot(y, wout_ref[...])


def _mixer0_kernel(x_ref, *refs):
    st_ref = refs[-1]

    @pl.when(pl.program_id(0) == 0)
    def _():
        st_ref[...] = jnp.zeros_like(st_ref)

    bsz, step_rows, _ = x_ref.shape
    tiles = [_mixer0_row(b, ti, x_ref, *refs) for b in range(bsz) for ti in range(step_rows // MIX_TM)]
    while tiles:
        tiles = [tile for tile in tiles if next(tile, True) is None]


def _mixer0(x, g, w_in, w_out, gn_g, w_s, bias_b, lb, og):
    bsz, t, d = x.shape
    const = lambda shape: pl.BlockSpec(shape, lambda i: (0,) * len(shape))
    return pl.pallas_call(
        _mixer0_kernel,
        grid=(t // (MIX_TM * MIX_NT),),
        in_specs=[
            pl.BlockSpec((bsz, MIX_TM * MIX_NT, d), lambda i: (0, i, 0)),
            const((1, d)),
            const((d, MIX_IN)),
            const((2 * MIX_HALF, d)),
            const((1, MIX_HALF)),
            const((GMLP_GROUPS, GMLP_CHUNK, GMLP_CHUNK)),
            const((GMLP_CHUNK, MIX_HALF)),
            const((1, MIX_HALF)),
            const((1, HGRN_DK)),
        ],
        out_specs=pl.BlockSpec((bsz, MIX_TM * MIX_NT, d), lambda i: (0, i, 0)),
        out_shape=jax.ShapeDtypeStruct((bsz, t, d), F32),
        scratch_shapes=[pltpu.VMEM((bsz, HGRN_HEADS, HGRN_DK, HGRN_DK), F32)],
        compiler_params=_cparams(("arbitrary",)),
        name="mixer0",
    )(x, g, w_in, w_out, gn_g, w_s, bias_b, lb, og)


def _swiglu(xb, wg, wu, wd):
    a = jnp.dot(xb, wg, preferred_element_type=F32)
    b = jnp.dot(xb, wu, preferred_element_type=F32)
    return _dot(_silu(a) * b, wd)


def _ffn_kernel(x_ref, g_ref, wg_ref, wu_ref, wd_ref, o_ref):
    x = x_ref[...]
    o_ref[...] = x + _swiglu(_rms(x, g_ref[...]).astype(BF16), wg_ref[...], wu_ref[...], wd_ref[...])


def _ffn(x, g, wg, wu, wd):
    n, d = x.shape
    tm = MLP_TM
    resident = lambda w: pl.BlockSpec(w.shape, lambda i: (0, 0), pipeline_mode=pl.Buffered(1))
    return pl.pallas_call(
        _ffn_kernel,
        grid=(n // tm,),
        in_specs=[pl.BlockSpec((tm, d), lambda i: (i, 0)),
                  pl.BlockSpec((1, d), lambda i: (0, 0)),
                  resident(wg), resident(wu), resident(wd)],
        out_specs=pl.BlockSpec((tm, d), lambda i: (i, 0)),
        out_shape=jax.ShapeDtypeStruct((n, d), F32),
        compiler_params=_cparams(("arbitrary",)),
        name="ffn_mlp",
    )(x, g, wg, wu, wd)


def _moe_router_kernel(x_ref, y_ref, wo_ref, g_ref, router_ref, rb_ref, tri_ref,
                       h_ref, hn_ref, posc_ref, posr_ref, cnt_ref):
    h = x_ref[...] + jnp.dot(y_ref[...], wo_ref[...], preferred_element_type=F32)
    h_ref[...] = h
    hn = _rms(h, g_ref[...])
    hn_ref[...] = hn.astype(BF16)
    hn_hi, hn_lo = _split(hn)
    r_hi, r_lo = _split(router_ref[...])
    nt = (((1,), (1,)), ((), ()))
    logits = (lax.dot_general(r_hi, hn_hi, nt, preferred_element_type=F32)
              + lax.dot_general(r_hi, hn_lo, nt, preferred_element_type=F32)
              + lax.dot_general(r_lo, hn_hi, nt, preferred_element_type=F32)) + rb_ref[...]
    n_rows = logits.shape[0]
    sub = lax.broadcasted_iota(jnp.int32, logits.shape, 0)
    m1 = jnp.max(logits, axis=0, keepdims=True)
    i1 = jnp.min(jnp.where(logits == m1, sub, n_rows), axis=0, keepdims=True)
    rest = jnp.where(sub == i1, -jnp.inf, logits)
    m2 = jnp.max(rest, axis=0, keepdims=True)
    i2 = jnp.min(jnp.where(rest == m2, sub, n_rows), axis=0, keepdims=True)
    e2 = jnp.exp(m2 - m1)
    den = 1.0 + e2
    sel1, sel2 = sub == i1, sub == i2
    sel_f = jnp.where(sel1 | sel2, 1.0, 0.0)
    rank = jnp.dot(sel_f.astype(BF16), tri_ref[...], preferred_element_type=F32)
    cnt = jnp.sum(sel_f, axis=1, keepdims=True)
    seg = jnp.floor((cnt + (MOE_ALIGN - 1)) * (1.0 / MOE_ALIGN)) * MOE_ALIGN
    starts, run = [], jnp.zeros((1, 1), F32)
    for e in range(n_rows):
        starts.append(run)
        run = run + seg[e:e + 1]
    pos = jnp.concatenate(starts, axis=0) + rank
    pos1 = jnp.sum(jnp.where(sel1, pos, 0.0), axis=0, keepdims=True)
    pos2 = jnp.sum(jnp.where(sel2, pos, 0.0), axis=0, keepdims=True)
    info = jnp.concatenate([pos1, pos2, 1.0 / den, e2 / den], axis=0)
    posr_ref[0] = jnp.concatenate([info, jnp.zeros((posr_ref.shape[1] - 4, info.shape[1]), F32)], axis=0)
    posc_ref[...] = jnp.concatenate([info, jnp.zeros((LANES - 4, info.shape[1]), F32)], axis=0).T
    cnt_ref[0] = jnp.broadcast_to(cnt[0:cnt_ref.shape[1]], cnt_ref.shape[1:])


def _moe_router(x, y, w_o, g, router, router_b, tri):
    n, d = x.shape
    tm = MOE_TT
    tile = lambda w: pl.BlockSpec((tm, w), lambda i: (i, 0))
    full = lambda arr: pl.BlockSpec(arr.shape, lambda i: (0,) * arr.ndim)
    small = lambda w: pl.BlockSpec((1, 8, w), lambda i: (i, 0, 0))
    return pl.pallas_call(
        _moe_router_kernel,
        grid=(n // tm,),
        in_specs=[tile(d), tile(d), full(w_o), full(g), full(router), full(router_b), full(tri)],
        out_specs=[tile(d), tile(d), tile(LANES), small(tm), small(LANES)],
        out_shape=[jax.ShapeDtypeStruct((n, d), F32), jax.ShapeDtypeStruct((n, d), BF16),
                   jax.ShapeDtypeStruct((n, LANES), F32), jax.ShapeDtypeStruct((n // tm, 8, tm), F32),
                   jax.ShapeDtypeStruct((n // tm, 8, LANES), F32)],
        compiler_params=_cparams(("arbitrary",)),
        name="moe_router",
    )(x, y, w_o, g, router, router_b, tri)


def _moe_expert_kernel(start_ref, cnt_ref, x_ref, hn_ref, posc_ref, posr_ref, wg_ref, wu_ref, wd_ref, gout_ref,
                       o_ref, xs_ref, ys_ref, *, n_e):
    i, e = pl.program_id(0), pl.program_id(1)
    tt, ch, blk = MOE_TT, MOE_CH, MOE_BLK
    n_sorted = xs_ref.shape[0]

    @pl.when(e == 0)
    def _():
        pos1, pos2 = posr_ref[0, 0:1, :], posr_ref[0, 1:2, :]
        hn = hn_ref[...]
        for rb in range(n_sorted // blk):
            srow = (lax.broadcasted_iota(jnp.int32, (blk, tt), 0) + rb * blk).astype(F32)
            p = jnp.where((srow == pos1) | (srow == pos2), 1.0, 0.0).astype(BF16)
            xs_ref[rb * blk:(rb + 1) * blk, :] = jnp.dot(p, hn, preferred_element_type=F32).astype(BF16)
        ys_ref[...] = jnp.zeros_like(ys_ref)

    start, count = start_ref[i, e], cnt_ref[i, e]
    row_in_chunk = lax.broadcasted_iota(jnp.int32, (ch, 1), 0)

    def chunk(k):
        r0 = pl.multiple_of(start + k * ch, MOE_ALIGN)
        xs = xs_ref[pl.ds(r0, ch), :]
        a = jnp.dot(xs, wg_ref[0], preferred_element_type=F32)
        b = jnp.dot(xs, wu_ref[0], preferred_element_type=F32)
        yield
        y = _dot(_silu(a) * b, wd_ref[0])
        yield
        ys_ref[pl.ds(r0, ch), :] = jnp.where(row_in_chunk < count - k * ch, y, 0.0).astype(BF16)

    def run(progs):
        while progs:
            progs = [p for p in progs if next(p, True) is None]

    def pair(kk, carry):
        run([chunk(2 * kk), chunk(2 * kk + 1)])
        return carry

    n_chunks = lax.shift_right_logical(count + (ch - 1), ch.bit_length() - 1)
    lax.fori_loop(0, lax.shift_right_logical(n_chunks, 1), pair, 0)

    @pl.when((n_chunks & 1) == 1)
    def _():
        run([chunk(n_chunks - 1)])

    @pl.when(e == n_e - 1)
    def _():
        ys = ys_ref[...]
        for tb in range(tt // blk):
            ts = slice(tb * blk, (tb + 1) * blk)
            pc = posc_ref[ts, :]
            scol = lax.broadcasted_iota(jnp.int32, (blk, n_sorted), 1).astype(F32)
            pw = (jnp.where(scol == pc[:, 0:1], pc[:, 2:3], 0.0)
                  + jnp.where(scol == pc[:, 1:2], pc[:, 3:4], 0.0)).astype(BF16)
            moe = jnp.dot(pw, ys, preferred_element_type=F32)
            o_ref[ts, :] = _rms(x_ref[ts, :] + moe, gout_ref[...])


def _moe_experts(start, count, x, hn, posc, posr, wg, wu, wd, g_out):
    n, d = x.shape
    n_e, _, dff = wg.shape
    tt = MOE_TT
    tile = lambda w: pl.BlockSpec((tt, w), lambda i, e, *_: (i, 0))
    grid_spec = pltpu.PrefetchScalarGridSpec(
        num_scalar_prefetch=2,
        grid=(n // tt, n_e),
        in_specs=[tile(d), tile(d), tile(LANES), pl.BlockSpec((1, 8, tt), lambda i, e, *_: (i, 0, 0)),
                  pl.BlockSpec((1, d, dff), lambda i, e, *_: (e, 0, 0)),
                  pl.BlockSpec((1, d, dff), lambda i, e, *_: (e, 0, 0)),
                  pl.BlockSpec((1, dff, d), lambda i, e, *_: (e, 0, 0)),
                  pl.BlockSpec((1, d), lambda i, e, *_: (0, 0))],
        out_specs=tile(d),
        scratch_shapes=[pltpu.VMEM((MOE_SORTED, d), BF16), pltpu.VMEM((MOE_SORTED, d), BF16)],
    )
    return pl.pallas_call(
        functools.partial(_moe_expert_kernel, n_e=n_e),
        grid_spec=grid_spec,
        out_shape=jax.ShapeDtypeStruct((n, d), F32),
        compiler_params=_cparams(("arbitrary", "arbitrary")),
        name="moe_experts",
    )(start, count, x, hn, posc, posr, wg, wu, wd, g_out)


def _rwkv_proj_kernel(h_ref, hp_ref, g_ref, mix_ref, wr_ref, wk_ref, wv_ref, w1_ref, w2_ref,
                      a1_ref, a2_ref, g1_ref, g2_ref, w0_ref, a0_ref, kk_ref, ka_ref, rk_ref,
                      hsum_ref,
                      r_out, k_out, v_out, kk_out, bb_out, lw_out, g_out, bonus_out,
                      *, tiles_per_seq):
    i = pl.program_id(0)
    g = g_ref[...]
    hn = _rms(h_ref[...], g)
    prev = _rms(hp_ref[7:8, :], g)
    prev = jnp.where(i % tiles_per_seq == 0, 0.0, prev)
    row = lax.broadcasted_iota(jnp.int32, hn.shape, 0)
    shifted = jnp.where(row == 0, prev, pltpu.roll(hn, 1, axis=0))
    dx = shifted - hn
    xr, xw, xk, xv, xa, xg = (hn + dx * mix_ref[m:m + 1, :] for m in range(6))
    r = _dot(xr, wr_ref[...])
    k = _dot(xk, wk_ref[...])
    v = _dot(xv, wv_ref[...])
    wl = w0_ref[...] + _dot(jnp.tanh(_dot(xw, w1_ref[...])), w2_ref[...])
    nwl = -wl
    w_log = -(jnp.maximum(nwl, 0.0) + jnp.log(1.0 + jnp.exp(-jnp.abs(nwl)))) - 0.5
    a = _sigmoid(a0_ref[...] + _dot(_dot(xa, a1_ref[...]), a2_ref[...]))
    gate = _dot(_sigmoid(_dot(xg, g1_ref[...])), g2_ref[...])
    hsum = hsum_ref[...]

    def head_sum(z):
        w = hsum.shape[0]
        return jnp.concatenate([_dot(z[:, s:s + w], hsum) for s in range(0, z.shape[1], w)], axis=1)

    kk = k * kk_ref[...]
    kk = kk / jnp.maximum(jnp.sqrt(head_sum(kk * kk)), 1e-12)
    k2 = k * (1.0 + (a - 1.0) * ka_ref[...])
    r_out[...] = r.astype(r_out.dtype)
    k_out[...] = k2.astype(k_out.dtype)
    v_out[...] = v.astype(v_out.dtype)
    kk_out[...] = kk.astype(kk_out.dtype)
    bb_out[...] = (kk * a).astype(bb_out.dtype)
    lw_out[...] = -jnp.exp(w_log)
    g_out[...] = gate.astype(g_out.dtype)
    bonus_out[...] = (head_sum(r * k2 * rk_ref[...]) * v).astype(bonus_out.dtype)


def _rwkv_proj(h, t, g, mix8, wr, wk, wv, w1, w2, a1, a2, g1, g2, w0, a0, k_k, k_a, r_k, hsum):
    n, d = h.shape
    tm = PROJ_TM
    full = lambda arr: pl.BlockSpec(arr.shape, lambda i: (0,) * arr.ndim)
    tile = pl.BlockSpec((tm, d), lambda i: (i, 0))
    prev = pl.BlockSpec((8, d), lambda i: (jnp.maximum(i * (tm // 8) - 1, 0), 0))
    consts = [g, mix8, wr, wk, wv, w1, w2, a1, a2, g1, g2, w0, a0, k_k, k_a, r_k, hsum]
    return pl.pallas_call(
        functools.partial(_rwkv_proj_kernel, tiles_per_seq=t // tm),
        grid=(n // tm,),
        in_specs=[tile, prev] + [full(c) for c in consts],
        out_specs=[tile] * 8,
        out_shape=[jax.ShapeDtypeStruct((n, d), F32 if idx == 5 else BF16) for idx in range(8)],
        compiler_params=_cparams(("arbitrary",)),
        name="rwkv_proj",
    )(h, h, *consts)


def _rwkv_rec_kernel(r_ref, k_ref, v_ref, kk_ref, bb_ref, lw_ref, bonus_ref, gate_ref,
                     lng_ref, lnb_ref, o_ref, st_ref):
    c, hd, w = REC_C, RWKV_HEAD, REC_W
    bsz, rows, d = r_ref.shape
    n_u = d // w
    chunks = [slice(ci * c, (ci + 1) * c) for ci in range(rows // c)]
    units = [(ci, b, u) for ci in range(len(chunks)) for b in range(bsz) for u in range(n_u)]

    @pl.when(pl.program_id(0) == 0)
    def _():
        st_ref[...] = jnp.zeros_like(st_ref)

    rr, cc = _iota2((c, c))
    cum_m = jnp.where(cc <= rr, 1.0, 0.0).astype(BF16)
    lane_c = lax.broadcasted_iota(jnp.int32, (c, w), 1)
    row_c = lax.broadcasted_iota(jnp.int32, (c, w), 0)
    head_c = lane_c >> (hd.bit_length() - 1)
    colm = lane_c & (hd - 1)
    strict = colm < row_c
    incl = colm <= row_c
    eye_cat = jnp.where(colm == row_c, 1.0, 0.0)
    r2, c2 = _iota2((w, w))
    bd = _same_block(r2, c2, hd)
    hmean = jnp.where(bd, 1.0 / hd, 0.0).astype(BF16)

    def stack(xm):
        return jnp.concatenate([jnp.where(head_c == h, xm, 0.0) for h in range(w // hd)], axis=0)

    prep = {}
    for ci, cs in enumerate(chunks):
        for b in range(bsz):
            lw = lw_ref[b, cs, :]
            lp = _dot_exact_lhs(cum_m, lw)
            pend = lp[c - 1:c, :]
            e_neg = jnp.exp(-lp)
            e_end = jnp.exp(pend - lp)
            kb, bbv = k_ref[b, cs, :].astype(F32), bb_ref[b, cs, :].astype(F32)
            prep[ci, b] = dict(rt=r_ref[b, cs, :].astype(F32) * jnp.exp(lp),
                               kt=kk_ref[b, cs, :].astype(F32) * jnp.exp(lp - lw),
                               kh=kb * e_neg, bh=bbv * e_neg, khp=kb * e_end, bhp=bbv * e_end,
                               p_end=jnp.exp(pend), v=v_ref[b, cs, :])

    def usl(name, ci, b, u):
        return prep[ci, b][name][:, u * w:(u + 1) * w]

    m_kk, m_kb, a_rk, a_rb = [], [], [], []
    for unit in units:
        lhs = jnp.concatenate([usl('kt', *unit), usl('rt', *unit)], axis=0)
        rhs = jnp.concatenate([stack(usl('kh', *unit)), stack(usl('bh', *unit))], axis=0)
        gram = _dot_nt(lhs, rhs)
        m_kk.append(jnp.where(strict, gram[0:c, 0:w], 0.0))
        m_kb.append(jnp.where(strict, gram[0:c, w:2 * w], 0.0))
        a_rk.append(jnp.where(incl, gram[c:2 * c, 0:w], 0.0))
        a_rb.append(jnp.where(incl, gram[c:2 * c, w:2 * w], 0.0))
    tinv = [eye_cat - m for m in m_kb]
    pw = [_dot(m, stack(m)) for m in m_kb]
    mv = [_dot(jnp.concatenate([m_kk[i], a_rk[i]], axis=0), stack(usl('v', *unit)))
          for i, unit in enumerate(units)]
    n_lvl = (c - 1).bit_length() - 1
    for lvl in range(n_lvl):
        last = lvl == n_lvl - 1
        for i in range(len(units)):
            spw = stack(pw[i])
            if last:
                tinv[i] = tinv[i] + _dot(tinv[i], spw)
            else:
                res = _dot(jnp.concatenate([tinv[i], pw[i]], axis=0), spw)
                tinv[i] = tinv[i] + res[0:c]
                pw[i] = res[c:2 * c]
    ys = []
    for ci in range(len(chunks)):
        mine = [(i, unit) for i, unit in enumerate(units) if unit[0] == ci]
        ka = {i: _dot_nt(jnp.concatenate([usl('kt', *unit), usl('rt', *unit)], axis=0), st_ref[unit[1], unit[2]])
              for i, unit in mine}
        uu = {i: _dot(tinv[i], stack(ka[i][0:c] + mv[i][0:c])) for i, _ in mine}
        for i, unit in mine:
            _, b, u = unit
            ys.append(ka[i][c:2 * c] + mv[i][c:2 * c] - _dot(a_rb[i], stack(uu[i])))
            inc = _dot_tn(jnp.concatenate([usl('v', *unit).astype(F32), uu[i]], axis=0),
                          jnp.concatenate([usl('khp', *unit), -usl('bhp', *unit)], axis=0))
            st_ref[b, u] = st_ref[b, u] * usl('p_end', *unit) + jnp.where(bd, inc, 0.0)
    y_all = jnp.concatenate(ys, axis=0)
    dlt = y_all - _dot(y_all, hmean)
    yn_all = dlt * lax.rsqrt(_dot(dlt * dlt, hmean) + GN_EPS)
    for i, (ci, b, u) in enumerate(units):
        ls, cs = slice(u * w, (u + 1) * w), chunks[ci]
        yn = yn_all[i * c:(i + 1) * c] * lng_ref[:, ls] + lnb_ref[:, ls]
        o_ref[b, cs, ls] = ((yn + bonus_ref[b, cs, ls].astype(F32))
                            * gate_ref[b, cs, ls].astype(F32)).astype(o_ref.dtype)


def _rwkv_rec(r, k, v, kk, bb, lw, bonus, gate, ln_g, ln_b):
    bsz, t, d = r.shape
    rows = REC_C * REC_NCH
    blk = pl.BlockSpec((bsz, rows, d), lambda i: (0, i, 0))
    vec = pl.BlockSpec((1, d), lambda i: (0, 0))
    return pl.pallas_call(
        _rwkv_rec_kernel,
        grid=(t // rows,),
        in_specs=[blk] * 8 + [vec, vec],
        out_specs=blk,
        out_shape=jax.ShapeDtypeStruct((bsz, t, d), BF16),
        scratch_shapes=[pltpu.VMEM((bsz, d // REC_W, REC_W, REC_W), F32)],
        compiler_params=_cparams(("arbitrary",)),
        name="rwkv_rec",
    )(r, k, v, kk, bb, lw, bonus, gate, ln_g, ln_b)


def kernel(x, norm_mix_g, norm_ffn_g, norm_out_g, mix_w_in, mix_w_out, gmlp_norm_g, gmlp_w_s, gmlp_b_s, hgrn_lb_logits, hgrn_onorm_g, ffn_w_gate, ffn_w_up, ffn_w_down, rwkv_mix, rwkv_w_r, rwkv_w_k, rwkv_w_v, rwkv_w_o, rwkv_w0, rwkv_w1, rwkv_w2, rwkv_a0, rwkv_a1, rwkv_a2, rwkv_g1, rwkv_g2, rwkv_k_k, rwkv_k_a, rwkv_r_k, rwkv_ln_g, rwkv_ln_b, moe_router, moe_router_b, moe_w_gate, moe_w_up, moe_w_down):
    bsz, t, d = x.shape
    n = bsz * t
    row = lambda vec: vec.reshape(1, -1).astype(F32)
    bf = lambda w: w.astype(BF16)

    lower_bounds = jnp.cumsum(jax.nn.softmax(hgrn_lb_logits.astype(F32), axis=0), axis=0)
    bias_b = jnp.repeat(gmlp_b_s[0].astype(F32).T, GMLP_DIM, axis=1)
    h = _mixer0(x, row(norm_mix_g[0]), bf(mix_w_in[0]), bf(mix_w_out[0]), row(gmlp_norm_g[0]),
                gmlp_w_s[0].astype(F32), bias_b, row(lower_bounds[0]), row(hgrn_onorm_g[0]))
    h = h.reshape(n, d)
    h = _ffn(h, row(norm_ffn_g[0]), bf(ffn_w_gate[0]), bf(ffn_w_up[0]), bf(ffn_w_down[0]))

    mix8 = jnp.concatenate([rwkv_mix[0].astype(F32), jnp.zeros((2, d), F32)], axis=0)
    head_id = jnp.arange(REC_W) // RWKV_HEAD
    hsum = (head_id[:, None] == head_id[None, :]).astype(BF16)
    r, k2, v, kk, bb, lw, gate, bonus = _rwkv_proj(
        h, t, row(norm_mix_g[1]), mix8, bf(rwkv_w_r[0]), bf(rwkv_w_k[0]), bf(rwkv_w_v[0]),
        bf(rwkv_w1[0]), bf(rwkv_w2[0]), bf(rwkv_a1[0]), bf(rwkv_a2[0]), bf(rwkv_g1[0]), bf(rwkv_g2[0]),
        row(rwkv_w0[0]), row(rwkv_a0[0]), row(rwkv_k_k[0]), row(rwkv_k_a[0]), row(rwkv_r_k[0]), hsum)
    sh = lambda z: z.reshape(bsz, t, d)
    yg = _rwkv_rec(sh(r), sh(k2), sh(v), sh(kk), sh(bb), sh(lw), sh(bonus), sh(gate),
                   row(rwkv_ln_g[0]), row(rwkv_ln_b[0]))
    router = jnp.zeros((MOE_ROWS, d), F32).at[:N_EXPERTS].set(moe_router[0].astype(F32).T)
    router_b = jnp.full((MOE_ROWS,), -1e30, F32).at[:N_EXPERTS].set(moe_router_b[0].astype(F32))
    router_b = jnp.broadcast_to(router_b[:, None], (MOE_ROWS, MOE_TT))
    tok = jnp.arange(MOE_TT)
    tri = (tok[:, None] < tok[None, :]).astype(BF16)
    h, hn, posc, posr, counts = _moe_router(h, yg.reshape(n, d), bf(rwkv_w_o[0]), row(norm_ffn_g[1]),
                                            router, router_b, tri)
    count = counts[:, :N_EXPERTS, 0].astype(jnp.int32)
    seg = (count + (MOE_ALIGN - 1)) // MOE_ALIGN * MOE_ALIGN
    start = jnp.cumsum(seg, axis=1) - seg
    out = _moe_experts(start, count, h, hn, posc, posr, bf(moe_w_gate[0]), bf(moe_w_up[0]), bf(moe_w_down[0]),
                       row(norm_out_g))
    return out.reshape(bsz, t, d)
```

```python
import functools

import jax
import jax.numpy as jnp
from jax import lax
from jax.experimental import pallas as pl
from jax.experimental.pallas import tpu as pltpu

F32 = jnp.float32
BF16 = jnp.bfloat16

GMLP_GROUPS = 4
GMLP_DIM = 128
GMLP_CHUNK = 128
HGRN_HEADS = 4
HGRN_DK = 128
HGRN_CHUNK = 32
MIX_HALF = 512
MIX_IN = 3072
RWKV_HEAD = 64
N_EXPERTS = 8
RMS_EPS = 1e-6
LN_EPS = 1e-5
GN_EPS = 64e-5

LANES = 128
VMEM_LIMIT_BYTES = 56 * 1024 * 1024

MIX_TM = 256
MIX_NT = 2
MIX_CUM_BLK = 256
MLP_TM = 512
MOE_TT = 1024
MOE_CH = 128
MOE_ALIGN = 16
MOE_ROWS = 16
MOE_BLK = 256
MOE_SORTED = -(-(2 * MOE_TT + N_EXPERTS * (MOE_ALIGN - 1) + MOE_CH - 1) // MOE_BLK) * MOE_BLK
PROJ_TM = 512
REC_C = 64
REC_NCH = 4
REC_HPU = 4
REC_W = REC_HPU * RWKV_HEAD


def _cparams(sem, n_in=0, cast_inputs=()):
    fuse = [i in cast_inputs for i in range(n_in)] if cast_inputs else None
    return pltpu.CompilerParams(dimension_semantics=sem, vmem_limit_bytes=VMEM_LIMIT_BYTES,
                                allow_input_fusion=fuse)


def _dot(a, b):
    return jnp.dot(a.astype(BF16), b.astype(BF16), preferred_element_type=F32)


def _dot_nt(a, b):
    return lax.dot_general(a.astype(BF16), b.astype(BF16), (((1,), (1,)), ((), ())),
                           preferred_element_type=F32)


def _dot_tn(a, b):
    return lax.dot_general(a.astype(BF16), b.astype(BF16), (((0,), (0,)), ((), ())),
                           preferred_element_type=F32)


def _split(a):
    hi = a.astype(BF16)
    lo = (a - hi.astype(F32)).astype(BF16)
    return hi, lo


def _dot_exact_lhs(m_bf16, a):
    hi, lo = _split(a)
    return (jnp.dot(m_bf16, hi, preferred_element_type=F32)
            + jnp.dot(m_bf16, lo, preferred_element_type=F32))


def _rms(x, g):
    return x * lax.rsqrt(jnp.mean(x * x, axis=-1, keepdims=True) + RMS_EPS) * g


def _sigmoid(x):
    return jax.nn.sigmoid(x)


def _silu(x):
    return x * jax.nn.sigmoid(x)


def _iota2(shape):
    return (lax.broadcasted_iota(jnp.int32, shape, 0), lax.broadcasted_iota(jnp.int32, shape, 1))


def _same_block(a, b, size):
    shift = size.bit_length() - 1
    assert 1 << shift == size
    return (a >> shift) == (b >> shift)


def _mixer0_row(row_idx, tile_idx, x_ref, g_ref, win_ref, wout_ref, gng_ref, ws_ref, bias_ref, lb_ref, og_ref,
                o_ref, st_ref):
    tm = MIX_TM
    rows = slice(tile_idx * tm, (tile_idx + 1) * tm)
    x = x_ref[row_idx, rows, :]
    z = _dot(_rms(x, g_ref[...]), win_ref[...])
    yield

    u = jax.nn.gelu(z[:, 0:MIX_HALF])
    v = jax.nn.gelu(z[:, MIX_HALF:2 * MIX_HALF])
    row, col = _iota2((GMLP_CHUNK, GMLP_CHUNK))
    tril = col <= row
    mixed_groups = []
    for g in range(GMLP_GROUPS):
        gs = slice(g * GMLP_DIM, (g + 1) * GMLP_DIM)
        vg = v[:, gs]
        mu = jnp.mean(vg, axis=-1, keepdims=True)
        d = vg - mu
        var = jnp.mean(d * d, axis=-1, keepdims=True)
        vn = (d * lax.rsqrt(var + LN_EPS) * gng_ref[:, gs]).astype(BF16)
        wg = jnp.where(tril, ws_ref[g], 0.0).astype(BF16)
        parts = [jnp.dot(wg, vn[c * GMLP_CHUNK:(c + 1) * GMLP_CHUNK], preferred_element_type=F32)
                 for c in range(tm // GMLP_CHUNK)]
        mixed_groups.append(jnp.concatenate(parts, axis=0))
    bias = jnp.concatenate([bias_ref[...]] * (tm // GMLP_CHUNK), axis=0)
    y_a = u * (jnp.concatenate(mixed_groups, axis=1) + bias)
    yield

    o0 = 2 * MIX_HALF
    zq = z[:, o0:o0 + MIX_HALF]
    zf = z[:, o0 + MIX_HALF:o0 + 2 * MIX_HALF]
    zi = z[:, o0 + 2 * MIX_HALF:o0 + 3 * MIX_HALF]
    zg = z[:, o0 + 3 * MIX_HALF:o0 + 4 * MIX_HALF]
    lb = lb_ref[...]
    q = _silu(zq)
    f = lb + (1.0 - lb) * _sigmoid(zf)
    k = 1.0 - f
    lf = jnp.log(f)
    cb = MIX_CUM_BLK
    rr, cc = _iota2((cb, cb))
    same = _same_block(rr, cc, HGRN_CHUNK)
    cum_m = jnp.where(same & (cc <= rr), 1.0, 0.0).astype(BF16)
    end_m = jnp.where(same, 1.0, 0.0).astype(BF16)
    b = jnp.concatenate([_dot_exact_lhs(cum_m, lf[s:s + cb]) for s in range(0, tm, cb)], axis=0)
    b_end = jnp.concatenate([_dot_exact_lhs(end_m, lf[s:s + cb]) for s in range(0, tm, cb)], axis=0)
    q_dec = (q * jnp.exp(b)).astype(BF16)
    k_inv = (k * jnp.exp(-b)).astype(BF16)
    k_end = (k * jnp.exp(b_end - b)).astype(BF16)
    dec = jnp.exp(b_end)
    vb = zi.astype(BF16)
    yield

    r128, c128 = _iota2((128, 128))
    intra_mask = _same_block(r128, c128, HGRN_CHUNK) & (c128 <= r128)
    og = og_ref[...]
    heads = [slice(h * HGRN_DK, (h + 1) * HGRN_DK) for h in range(HGRN_HEADS)]
    chunks = [slice(c * HGRN_CHUNK, (c + 1) * HGRN_CHUNK) for c in range(tm // HGRN_CHUNK)]
    inc_t = [[_dot_tn(vb[cs, hs], k_end[cs, hs]) for cs in chunks] for hs in heads]
    scores = [[jnp.where(intra_mask, _dot_nt(q_dec[rs, hs], k_inv[rs, hs]), 0.0)
               for rs in (slice(rb * 128, (rb + 1) * 128) for rb in range(tm // 128))] for hs in heads]
    yield
    st_in = []
    for h, hs in enumerate(heads):
        st, per_chunk = st_ref[row_idx, h], []
        for c, cs in enumerate(chunks):
            per_chunk.append(st)
            st = st * dec[cs.start:cs.start + 1, hs] + inc_t[h][c]
        st_ref[row_idx, h] = st
        st_in.append(per_chunk)
    yield
    o_heads = []
    for h, hs in enumerate(heads):
        o_intra = jnp.concatenate([_dot(sc, vb[rb * 128:(rb + 1) * 128, hs])
                                   for rb, sc in enumerate(scores[h])], axis=0)
        o_inter = jnp.concatenate([_dot_nt(q_dec[cs, hs], st_in[h][c]) for c, cs in enumerate(chunks)], axis=0)
        o_h = o_intra + o_inter
        o_h = o_h * lax.rsqrt(jnp.mean(o_h * o_h, axis=-1, keepdims=True) + RMS_EPS) * og
        o_heads.append(o_h * _silu(zg[:, hs]))

    yield
    y = jnp.concatenate([y_a] + o_heads, axis=1)
    o_ref[row_idx, rows, :] = x + _dot(y, wout_ref[...])


def _mixer0_kernel(x_ref, *refs):
    st_ref = refs[-1]

    @pl.when(pl.program_id(0) == 0)
    def _():
        st_ref[...] = jnp.zeros_like(st_ref)

    bsz, step_rows, _ = x_ref.shape
    tiles = [_mixer0_row(b, ti, x_ref, *refs) for b in range(bsz) for ti in range(step_rows // MIX_TM)]
    while tiles:
        tiles = [tile for tile in tiles if next(tile, True) is None]


def _mixer0(x, g, w_in, w_out, gn_g, w_s, bias_b, lb, og):
    bsz, t, d = x.shape
    const = lambda shape: pl.BlockSpec(shape, lambda i: (0,) * len(shape))
    return pl.pallas_call(
        _mixer0_kernel,
        grid=(t // (MIX_TM * MIX_NT),),
        in_specs=[
            pl.BlockSpec((bsz, MIX_TM * MIX_NT, d), lambda i: (0, i, 0)),
            const((1, d)),
            const((d, MIX_IN)),
            const((2 * MIX_HALF, d)),
            const((1, MIX_HALF)),
            const((GMLP_GROUPS, GMLP_CHUNK, GMLP_CHUNK)),
            const((GMLP_CHUNK, MIX_HALF)),
            const((1, MIX_HALF)),
            const((1, HGRN_DK)),
        ],
        out_specs=pl.BlockSpec((bsz, MIX_TM * MIX_NT, d), lambda i: (0, i, 0)),
        out_shape=jax.ShapeDtypeStruct((bsz, t, d), F32),
        scratch_shapes=[pltpu.VMEM((bsz, HGRN_HEADS, HGRN_DK, HGRN_DK), F32)],
        compiler_params=_cparams(("arbitrary",), n_in=9, cast_inputs=(2, 3)),
        name="mixer0",
    )(x, g, w_in, w_out, gn_g, w_s, bias_b, lb, og)


def _swiglu(xb, wg, wu, wd):
    a = jnp.dot(xb, wg, preferred_element_type=F32)
    b = jnp.dot(xb, wu, preferred_element_type=F32)
    return _dot(_silu(a) * b, wd)


def _ffn_kernel(x_ref, g_ref, wg_ref, wu_ref, wd_ref, o_ref):
    x = x_ref[...]
    o_ref[...] = x + _swiglu(_rms(x, g_ref[...]).astype(BF16), wg_ref[...], wu_ref[...], wd_ref[...])


def _ffn(x, g, wg, wu, wd):
    n, d = x.shape
    tm = MLP_TM
    resident = lambda w: pl.BlockSpec(w.shape, lambda i: (0, 0), pipeline_mode=pl.Buffered(1))
    return pl.pallas_call(
        _ffn_kernel,
        grid=(n // tm,),
        in_specs=[pl.BlockSpec((tm, d), lambda i: (i, 0)),
                  pl.BlockSpec((1, d), lambda i: (0, 0)),
                  resident(wg), resident(wu), resident(wd)],
        out_specs=pl.BlockSpec((tm, d), lambda i: (i, 0)),
        out_shape=jax.ShapeDtypeStruct((n, d), F32),
        compiler_params=_cparams(("arbitrary",), n_in=5, cast_inputs=(2, 3, 4)),
        name="ffn_mlp",
    )(x, g, wg, wu, wd)


def _moe_router_kernel(x_ref, y_ref, wo_ref, g_ref, router_ref, rb_ref, tri_ref,
                       h_ref, hn_ref, posc_ref, posr_ref, cnt_ref):
    h = x_ref[...] + jnp.dot(y_ref[...], wo_ref[...], preferred_element_type=F32)
    h_ref[...] = h
    hn = _rms(h, g_ref[...])
    hn_ref[...] = hn.astype(BF16)
    hn_hi, hn_lo = _split(hn)
    r_hi, r_lo = _split(router_ref[...])
    nt = (((1,), (1,)), ((), ()))
    logits = (lax.dot_general(r_hi, hn_hi, nt, preferred_element_type=F32)
              + lax.dot_general(r_hi, hn_lo, nt, preferred_element_type=F32)
              + lax.dot_general(r_lo, hn_hi, nt, preferred_element_type=F32)) + rb_ref[...]
    n_rows = logits.shape[0]
    sub = lax.broadcasted_iota(jnp.int32, logits.shape, 0)
    m1 = jnp.max(logits, axis=0, keepdims=True)
    i1 = jnp.min(jnp.where(logits == m1, sub, n_rows), axis=0, keepdims=True)
    rest = jnp.where(sub == i1, -jnp.inf, logits)
    m2 = jnp.max(rest, axis=0, keepdims=True)
    i2 = jnp.min(jnp.where(rest == m2, sub, n_rows), axis=0, keepdims=True)
    e2 = jnp.exp(m2 - m1)
    den = 1.0 + e2
    sel1, sel2 = sub == i1, sub == i2
    sel_f = jnp.where(sel1 | sel2, 1.0, 0.0)
    rank = jnp.dot(sel_f.astype(BF16), tri_ref[...], preferred_element_type=F32)
    cnt = jnp.sum(sel_f, axis=1, keepdims=True)
    seg = jnp.floor((cnt + (MOE_ALIGN - 1)) * (1.0 / MOE_ALIGN)) * MOE_ALIGN
    starts, run = [], jnp.zeros((1, 1), F32)
    for e in range(n_rows):
        starts.append(run)
        run = run + seg[e:e + 1]
    pos = jnp.concatenate(starts, axis=0) + rank
    pos1 = jnp.sum(jnp.where(sel1, pos, 0.0), axis=0, keepdims=True)
    pos2 = jnp.sum(jnp.where(sel2, pos, 0.0), axis=0, keepdims=True)
    info = jnp.concatenate([pos1, pos2, 1.0 / den, e2 / den], axis=0)
    posr_ref[0] = jnp.concatenate([info, jnp.zeros((posr_ref.shape[1] - 4, info.shape[1]), F32)], axis=0)
    posc_ref[...] = jnp.concatenate([info, jnp.zeros((LANES - 4, info.shape[1]), F32)], axis=0).T
    cnt_ref[0] = jnp.broadcast_to(cnt[0:cnt_ref.shape[1]], cnt_ref.shape[1:])


def _moe_router(x, y, w_o, g, router, router_b, tri):
    n, d = x.shape
    tm = MOE_TT
    tile = lambda w: pl.BlockSpec((tm, w), lambda i: (i, 0))
    full = lambda arr: pl.BlockSpec(arr.shape, lambda i: (0,) * arr.ndim)
    small = lambda w: pl.BlockSpec((1, 8, w), lambda i: (i, 0, 0))
    return pl.pallas_call(
        _moe_router_kernel,
        grid=(n // tm,),
        in_specs=[tile(d), tile(d), full(w_o), full(g), full(router), full(router_b), full(tri)],
        out_specs=[tile(d), tile(d), tile(LANES), small(tm), small(LANES)],
        out_shape=[jax.ShapeDtypeStruct((n, d), F32), jax.ShapeDtypeStruct((n, d), BF16),
                   jax.ShapeDtypeStruct((n, LANES), F32), jax.ShapeDtypeStruct((n // tm, 8, tm), F32),
                   jax.ShapeDtypeStruct((n // tm, 8, LANES), F32)],
        compiler_params=_cparams(("arbitrary",)),
        name="moe_router",
    )(x, y, w_o, g, router, router_b, tri)


def _moe_expert_kernel(start_ref, cnt_ref, x_ref, hn_ref, posc_ref, posr_ref, wg_ref, wu_ref, wd_ref, gout_ref,
                       o_ref, xs_ref, ys_ref, *, n_e):
    i, e = pl.program_id(0), pl.program_id(1)
    tt, ch, blk = MOE_TT, MOE_CH, MOE_BLK
    n_sorted = xs_ref.shape[0]

    @pl.when(e == 0)
    def _():
        pos1, pos2 = posr_ref[0, 0:1, :], posr_ref[0, 1:2, :]
        hn = hn_ref[...]
        for rb in range(n_sorted // blk):
            srow = (lax.broadcasted_iota(jnp.int32, (blk, tt), 0) + rb * blk).astype(F32)
            p = jnp.where((srow == pos1) | (srow == pos2), 1.0, 0.0).astype(BF16)
            xs_ref[rb * blk:(rb + 1) * blk, :] = jnp.dot(p, hn, preferred_element_type=F32).astype(BF16)
        ys_ref[...] = jnp.zeros_like(ys_ref)

    start, count = start_ref[i, e], cnt_ref[i, e]
    row_in_chunk = lax.broadcasted_iota(jnp.int32, (ch, 1), 0)

    def chunk(k):
        r0 = pl.multiple_of(start + k * ch, MOE_ALIGN)
        xs = xs_ref[pl.ds(r0, ch), :]
        a = jnp.dot(xs, wg_ref[0], preferred_element_type=F32)
        b = jnp.dot(xs, wu_ref[0], preferred_element_type=F32)
        yield
        y = _dot(_silu(a) * b, wd_ref[0])
        yield
        ys_ref[pl.ds(r0, ch), :] = jnp.where(row_in_chunk < count - k * ch, y, 0.0).astype(BF16)

    def run(progs):
        while progs:
            progs = [p for p in progs if next(p, True) is None]

    def pair(kk, carry):
        run([chunk(2 * kk), chunk(2 * kk + 1)])
        return carry

    n_chunks = lax.shift_right_logical(count + (ch - 1), ch.bit_length() - 1)
    lax.fori_loop(0, lax.shift_right_logical(n_chunks, 1), pair, 0)

    @pl.when((n_chunks & 1) == 1)
    def _():
        run([chunk(n_chunks - 1)])

    @pl.when(e == n_e - 1)
    def _():
        ys = ys_ref[...]
        for tb in range(tt // blk):
            ts = slice(tb * blk, (tb + 1) * blk)
            pc = posc_ref[ts, :]
            scol = lax.broadcasted_iota(jnp.int32, (blk, n_sorted), 1).astype(F32)
            pw = (jnp.where(scol == pc[:, 0:1], pc[:, 2:3], 0.0)
                  + jnp.where(scol == pc[:, 1:2], pc[:, 3:4], 0.0)).astype(BF16)
            moe = jnp.dot(pw, ys, preferred_element_type=F32)
            o_ref[ts, :] = _rms(x_ref[ts, :] + moe, gout_ref[...])


def _moe_experts(start, count, x, hn, posc, posr, wg, wu, wd, g_out):
    n, d = x.shape
    n_e, _, dff = wg.shape
    tt = MOE_TT
    tile = lambda w: pl.BlockSpec((tt, w), lambda i, e, *_: (i, 0))
    grid_spec = pltpu.PrefetchScalarGridSpec(
        num_scalar_prefetch=2,
        grid=(n // tt, n_e),
        in_specs=[tile(d), tile(d), tile(LANES), pl.BlockSpec((1, 8, tt), lambda i, e, *_: (i, 0, 0)),
                  pl.BlockSpec((1, d, dff), lambda i, e, *_: (e, 0, 0)),
                  pl.BlockSpec((1, d, dff), lambda i, e, *_: (e, 0, 0)),
                  pl.BlockSpec((1, dff, d), lambda i, e, *_: (e, 0, 0)),
                  pl.BlockSpec((1, d), lambda i, e, *_: (0, 0))],
        out_specs=tile(d),
        scratch_shapes=[pltpu.VMEM((MOE_SORTED, d), BF16), pltpu.VMEM((MOE_SORTED, d), BF16)],
    )
    return pl.pallas_call(
        functools.partial(_moe_expert_kernel, n_e=n_e),
        grid_spec=grid_spec,
        out_shape=jax.ShapeDtypeStruct((n, d), F32),
        compiler_params=_cparams(("arbitrary", "arbitrary")),
        name="moe_experts",
    )(start, count, x, hn, posc, posr, wg, wu, wd, g_out)


def _rwkv_proj_kernel(h_ref, hp_ref, g_ref, mix_ref, wr_ref, wk_ref, wv_ref, w1_ref, w2_ref,
                      a1_ref, a2_ref, g1_ref, g2_ref, w0_ref, a0_ref, kk_ref, ka_ref, rk_ref,
                      hsum_ref,
                      r_out, k_out, v_out, kk_out, bb_out, lw_out, g_out, bonus_out,
                      *, tiles_per_seq):
    i = pl.program_id(0)
    g = g_ref[...]
    hn = _rms(h_ref[...], g)
    prev = _rms(hp_ref[7:8, :], g)
    prev = jnp.where(i % tiles_per_seq == 0, 0.0, prev)
    row = lax.broadcasted_iota(jnp.int32, hn.shape, 0)
    shifted = jnp.where(row == 0, prev, pltpu.roll(hn, 1, axis=0))
    dx = shifted - hn
    xr, xw, xk, xv, xa, xg = (hn + dx * mix_ref[m:m + 1, :] for m in range(6))
    r = _dot(xr, wr_ref[...])
    k = _dot(xk, wk_ref[...])
    v = _dot(xv, wv_ref[...])
    wl = w0_ref[...] + _dot(jnp.tanh(_dot(xw, w1_ref[...])), w2_ref[...])
    nwl = -wl
    w_log = -(jnp.maximum(nwl, 0.0) + jnp.log(1.0 + jnp.exp(-jnp.abs(nwl)))) - 0.5
    a = _sigmoid(a0_ref[...] + _dot(_dot(xa, a1_ref[...]), a2_ref[...]))
    gate = _dot(_sigmoid(_dot(xg, g1_ref[...])), g2_ref[...])
    hsum = hsum_ref[...]

    def head_sum(z):
        w = hsum.shape[0]
        return jnp.concatenate([_dot(z[:, s:s + w], hsum) for s in range(0, z.shape[1], w)], axis=1)

    kk = k * kk_ref[...]
    kk = kk / jnp.maximum(jnp.sqrt(head_sum(kk * kk)), 1e-12)
    k2 = k * (1.0 + (a - 1.0) * ka_ref[...])
    r_out[...] = r.astype(r_out.dtype)
    k_out[...] = k2.astype(k_out.dtype)
    v_out[...] = v.astype(v_out.dtype)
    kk_out[...] = kk.astype(kk_out.dtype)
    bb_out[...] = (kk * a).astype(bb_out.dtype)
    lw_out[...] = -jnp.exp(w_log)
    g_out[...] = gate.astype(g_out.dtype)
    bonus_out[...] = (head_sum(r * k2 * rk_ref[...]) * v).astype(bonus_out.dtype)


def _rwkv_proj(h, t, g, mix8, wr, wk, wv, w1, w2, a1, a2, g1, g2, w0, a0, k_k, k_a, r_k, hsum):
    n, d = h.shape
    tm = PROJ_TM
    full = lambda arr: pl.BlockSpec(arr.shape, lambda i: (0,) * arr.ndim)
    tile = pl.BlockSpec((tm, d), lambda i: (i, 0))
    prev = pl.BlockSpec((8, d), lambda i: (jnp.maximum(i * (tm // 8) - 1, 0), 0))
    consts = [g, mix8, wr, wk, wv, w1, w2, a1, a2, g1, g2, w0, a0, k_k, k_a, r_k, hsum]
    return pl.pallas_call(
        functools.partial(_rwkv_proj_kernel, tiles_per_seq=t // tm),
        grid=(n // tm,),
        in_specs=[tile, prev] + [full(c) for c in consts],
        out_specs=[tile] * 8,
        out_shape=[jax.ShapeDtypeStruct((n, d), F32 if idx == 5 else BF16) for idx in range(8)],
        compiler_params=_cparams(("arbitrary",), n_in=2 + len(consts), cast_inputs=tuple(range(4, 13))),
        name="rwkv_proj",
    )(h, h, *consts)


def _rwkv_rec_kernel(r_ref, k_ref, v_ref, kk_ref, bb_ref, lw_ref, bonus_ref, gate_ref,
                     lng_ref, lnb_ref, o_ref, st_ref):
    c, hd, w = REC_C, RWKV_HEAD, REC_W
    bsz, rows, d = r_ref.shape
    n_u = d // w
    chunks = [slice(ci * c, (ci + 1) * c) for ci in range(rows // c)]
    units = [(ci, b, u) for ci in range(len(chunks)) for b in range(bsz) for u in range(n_u)]

    @pl.when(pl.program_id(0) == 0)
    def _():
        st_ref[...] = jnp.zeros_like(st_ref)

    rr, cc = _iota2((c, c))
    cum_m = jnp.where(cc <= rr, 1.0, 0.0).astype(BF16)
    lane_c = lax.broadcasted_iota(jnp.int32, (c, w), 1)
    row_c = lax.broadcasted_iota(jnp.int32, (c, w), 0)
    head_c = lane_c >> (hd.bit_length() - 1)
    colm = lane_c & (hd - 1)
    strict = colm < row_c
    incl = colm <= row_c
    eye_cat = jnp.where(colm == row_c, 1.0, 0.0)
    r2, c2 = _iota2((w, w))
    bd = _same_block(r2, c2, hd)
    hmean = jnp.where(bd, 1.0 / hd, 0.0).astype(BF16)

    def stack(xm):
        return jnp.concatenate([jnp.where(head_c == h, xm, 0.0) for h in range(w // hd)], axis=0)

    prep = {}
    for ci, cs in enumerate(chunks):
        for b in range(bsz):
            lw = lw_ref[b, cs, :]
            lp = _dot_exact_lhs(cum_m, lw)
            pend = lp[c - 1:c, :]
            e_neg = jnp.exp(-lp)
            e_end = jnp.exp(pend - lp)
            kb, bbv = k_ref[b, cs, :].astype(F32), bb_ref[b, cs, :].astype(F32)
            prep[ci, b] = dict(rt=r_ref[b, cs, :].astype(F32) * jnp.exp(lp),
                               kt=kk_ref[b, cs, :].astype(F32) * jnp.exp(lp - lw),
                               kh=kb * e_neg, bh=bbv * e_neg, khp=kb * e_end, bhp=bbv * e_end,
                               p_end=jnp.exp(pend), v=v_ref[b, cs, :])

    def usl(name, ci, b, u):
        return prep[ci, b][name][:, u * w:(u + 1) * w]

    m_kk, m_kb, a_rk, a_rb = [], [], [], []
    for unit in units:
        lhs = jnp.concatenate([usl('kt', *unit), usl('rt', *unit)], axis=0)
        rhs = jnp.concatenate([stack(usl('kh', *unit)), stack(usl('bh', *unit))], axis=0)
        gram = _dot_nt(lhs, rhs)
        m_kk.append(jnp.where(strict, gram[0:c, 0:w], 0.0))
        m_kb.append(jnp.where(strict, gram[0:c, w:2 * w], 0.0))
        a_rk.append(jnp.where(incl, gram[c:2 * c, 0:w], 0.0))
        a_rb.append(jnp.where(incl, gram[c:2 * c, w:2 * w], 0.0))
    tinv = [eye_cat - m for m in m_kb]
    pw = [_dot(m, stack(m)) for m in m_kb]
    mv = [_dot(jnp.concatenate([m_kk[i], a_rk[i]], axis=0), stack(usl('v', *unit)))
          for i, unit in enumerate(units)]
    n_lvl = (c - 1).bit_length() - 1
    for lvl in range(n_lvl):
        last = lvl == n_lvl - 1
        for i in range(len(units)):
            spw = stack(pw[i])
            if last:
                tinv[i] = tinv[i] + _dot(tinv[i], spw)
            else:
                res = _dot(jnp.concatenate([tinv[i], pw[i]], axis=0), spw)
                tinv[i] = tinv[i] + res[0:c]
                pw[i] = res[c:2 * c]
    ys = []
    for ci in range(len(chunks)):
        mine = [(i, unit) for i, unit in enumerate(units) if unit[0] == ci]
        ka = {i: _dot_nt(jnp.concatenate([usl('kt', *unit), usl('rt', *unit)], axis=0), st_ref[unit[1], unit[2]])
              for i, unit in mine}
        uu = {i: _dot(tinv[i], stack(ka[i][0:c] + mv[i][0:c])) for i, _ in mine}
        for i, unit in mine:
            _, b, u = unit
            ys.append(ka[i][c:2 * c] + mv[i][c:2 * c] - _dot(a_rb[i], stack(uu[i])))
            inc = _dot_tn(jnp.concatenate([usl('v', *unit).astype(F32), uu[i]], axis=0),
                          jnp.concatenate([usl('khp', *unit), -usl('bhp', *unit)], axis=0))
            st_ref[b, u] = st_ref[b, u] * usl('p_end', *unit) + jnp.where(bd, inc, 0.0)
    y_all = jnp.concatenate(ys, axis=0)
    dlt = y_all - _dot(y_all, hmean)
    yn_all = dlt * lax.rsqrt(_dot(dlt * dlt, hmean) + GN_EPS)
    for i, (ci, b, u) in enumerate(units):
        ls, cs = slice(u * w, (u + 1) * w), chunks[ci]
        yn = yn_all[i * c:(i + 1) * c] * lng_ref[:, ls] + lnb_ref[:, ls]
        o_ref[b, cs, ls] = ((yn + bonus_ref[b, cs, ls].astype(F32))
                            * gate_ref[b, cs, ls].astype(F32)).astype(o_ref.dtype)


def _rwkv_rec(r, k, v, kk, bb, lw, bonus, gate, ln_g, ln_b):
    bsz, t, d = r.shape
    rows = REC_C * REC_NCH
    blk = pl.BlockSpec((bsz, rows, d), lambda i: (0, i, 0))
    vec = pl.BlockSpec((1, d), lambda i: (0, 0))
    return pl.pallas_call(
        _rwkv_rec_kernel,
        grid=(t // rows,),
        in_specs=[blk] * 8 + [vec, vec],
        out_specs=blk,
        out_shape=jax.ShapeDtypeStruct((bsz, t, d), BF16),
        scratch_shapes=[pltpu.VMEM((bsz, d // REC_W, REC_W, REC_W), F32)],
        compiler_params=_cparams(("arbitrary",)),
        name="rwkv_rec",
    )(r, k, v, kk, bb, lw, bonus, gate, ln_g, ln_b)


def kernel(x, norm_mix_g, norm_ffn_g, norm_out_g, mix_w_in, mix_w_out, gmlp_norm_g, gmlp_w_s, gmlp_b_s, hgrn_lb_logits, hgrn_onorm_g, ffn_w_gate, ffn_w_up, ffn_w_down, rwkv_mix, rwkv_w_r, rwkv_w_k, rwkv_w_v, rwkv_w_o, rwkv_w0, rwkv_w1, rwkv_w2, rwkv_a0, rwkv_a1, rwkv_a2, rwkv_g1, rwkv_g2, rwkv_k_k, rwkv_k_a, rwkv_r_k, rwkv_ln_g, rwkv_ln_b, moe_router, moe_router_b, moe_w_gate, moe_w_up, moe_w_down):
    bsz, t, d = x.shape
    n = bsz * t
    row = lambda vec: vec.reshape(1, -1).astype(F32)
    bf = lambda w: w.astype(BF16)

    lower_bounds = jnp.cumsum(jax.nn.softmax(hgrn_lb_logits.astype(F32), axis=0), axis=0)
    bias_b = jnp.repeat(gmlp_b_s[0].astype(F32).T, GMLP_DIM, axis=1)
    h = _mixer0(x, row(norm_mix_g[0]), bf(mix_w_in[0]), bf(mix_w_out[0]), row(gmlp_norm_g[0]),
                gmlp_w_s[0].astype(F32), bias_b, row(lower_bounds[0]), row(hgrn_onorm_g[0]))
    h = h.reshape(n, d)
    h = _ffn(h, row(norm_ffn_g[0]), bf(ffn_w_gate[0]), bf(ffn_w_up[0]), bf(ffn_w_down[0]))

    mix8 = jnp.concatenate([rwkv_mix[0].astype(F32), jnp.zeros((2, d), F32)], axis=0)
    head_id = jnp.arange(REC_W) // RWKV_HEAD
    hsum = (head_id[:, None] == head_id[None, :]).astype(BF16)
    r, k2, v, kk, bb, lw, gate, bonus = _rwkv_proj(
        h, t, row(norm_mix_g[1]), mix8, bf(rwkv_w_r[0]), bf(rwkv_w_k[0]), bf(rwkv_w_v[0]),
        bf(rwkv_w1[0]), bf(rwkv_w2[0]), bf(rwkv_a1[0]), bf(rwkv_a2[0]), bf(rwkv_g1[0]), bf(rwkv_g2[0]),
        row(rwkv_w0[0]), row(rwkv_a0[0]), row(rwkv_k_k[0]), row(rwkv_k_a[0]), row(rwkv_r_k[0]), hsum)
    sh = lambda z: z.reshape(bsz, t, d)
    yg = _rwkv_rec(sh(r), sh(k2), sh(v), sh(kk), sh(bb), sh(lw), sh(bonus), sh(gate),
                   row(rwkv_ln_g[0]), row(rwkv_ln_b[0]))
    router = jnp.zeros((MOE_ROWS, d), F32).at[:N_EXPERTS].set(moe_router[0].astype(F32).T)
    router_b = jnp.full((MOE_ROWS,), -1e30, F32).at[:N_EXPERTS].set(moe_router_b[0].astype(F32))
    router_b = jnp.broadcast_to(router_b[:, None], (MOE_ROWS, MOE_TT))
    tok = jnp.arange(MOE_TT)
    tri = (tok[:, None] < tok[None, :]).astype(BF16)
    h, hn, posc, posr, counts = _moe_router(h, yg.reshape(n, d), bf(rwkv_w_o[0]), row(norm_ffn_g[1]),
                                            router, router_b, tri)
    count = counts[:, :N_EXPERTS, 0].astype(jnp.int32)
    seg = (count + (MOE_ALIGN - 1)) // MOE_ALIGN * MOE_ALIGN
    start = jnp.cumsum(seg, axis=1) - seg
    out = _moe_experts(start, count, h, hn, posc, posr, bf(moe_w_gate[0]), bf(moe_w_up[0]), bf(moe_w_down[0]),
                       row(norm_out_g))
    return out.reshape(bsz, t, d)
```
